```python
import math
import jax
import jax.numpy as jnp
from jax import lax
import numpy as np

D_MODEL = 1024
BATCH = 8
SEQ = 4096
DEPTH = 2
DEC_BATCH = 4
DEC_SEQ = 4096
PAST_LEN = 128

N_META = 16
NORM_EPS = 1e-6
N_BRANCH = 4
BRANCH_WIDTH = D_MODEL // 2
D_FF = 4 * D_MODEL

HG_WIDTH = BRANCH_WIDTH
HG_HEAD_DIM = 128
HG_HEADS = HG_WIDTH // HG_HEAD_DIM
HG_CHUNK = 64
LB_FLOOR = 1e-20

SC_WIDTH = BRANCH_WIDTH
SC_KSIZE = 3

DA_HEADS = 4
DA_QK_DIM = 64
DA_V_DIM = 2 * DA_QK_DIM
DA_QK_WIDTH = DA_HEADS * 2 * DA_QK_DIM
DA_WIDTH = DA_HEADS * DA_V_DIM
ROPE_THETA = 500000.0
ROPE_DIM = DA_QK_DIM // 4
Q_BLOCK = 128
SUBLN_EPS = 1e-5

RW_WIDTH = BRANCH_WIDTH
RW_HEAD_DIM = 64
RW_HEADS = RW_WIDTH // RW_HEAD_DIM
RW_DECAY_RANK = 64
RW_ICLR_RANK = 64
RW_GATE_RANK = 128
RW_LNX_EPS = 64e-5
RW_SIZES = (RW_WIDTH, RW_WIDTH, RW_WIDTH, RW_DECAY_RANK, RW_DECAY_RANK, RW_ICLR_RANK, RW_ICLR_RANK, RW_GATE_RANK)
RW_COLS = sum(RW_SIZES)
IN_SIZES = (HG_WIDTH,) * 5 + (SC_WIDTH,) * 3 + (DA_QK_WIDTH, DA_QK_WIDTH, DA_WIDTH, RW_COLS)
N_IN = sum(IN_SIZES)

kernel_name = 'hybrid_bidir_encoder_two_groups'


def _split(z, sizes):
    return jnp.split(z, [int(c) for c in np.cumsum(sizes)[:-1]], axis=-1)


def rms_norm(x, g, eps=NORM_EPS):
    xf = x.astype(jnp.float32)
    y = xf * lax.rsqrt(jnp.mean(xf * xf, axis=-1, keepdims=True) + eps)
    return (y * g.astype(jnp.float32)).astype(x.dtype)


def rope_partial(x, pos):
    half = ROPE_DIM // 2
    inv = ROPE_THETA ** (-jnp.arange(half, dtype=jnp.float32) / half)
    ang = pos[:, None] * inv[None, :]
    cos, sin = jnp.cos(ang), jnp.sin(ang)
    x1 = x[..., :half].astype(jnp.float32)
    x2 = x[..., half:ROPE_DIM].astype(jnp.float32)
    rot = jnp.concatenate([x1 * cos - x2 * sin, x2 * cos + x1 * sin], axis=-1).astype(x.dtype)
    return jnp.concatenate([rot, x[..., ROPE_DIM:]], axis=-1)


def centred_shift(u):
    up = jnp.pad(u, ((0, 0), (1, 1), (0, 0)))
    return 0.5 * (up[:, :-2] + up[:, 2:])


def gla_chunk_scan(q, k, logf, v):
    B, H, T, DK = q.shape
    DV = v.shape[-1]
    n = T // HG_CHUNK

    def to_chunks(a):
        return jnp.moveaxis(a.reshape(B, H, n, HG_CHUNK, a.shape[-1]), 2, 0)

    causal = jnp.tril(jnp.ones((HG_CHUNK, HG_CHUNK), dtype=bool))[:, :, None]

    def step(S, inp):
        qi, ki, gi, vi = inp
        b = jnp.cumsum(gi, axis=2)
        diff = b[:, :, :, None, :] - b[:, :, None, :, :]
        decay = jnp.where(causal, jnp.exp(jnp.where(causal, diff, 0.0)), 0.0)
        attn = jnp.einsum('bhtk,bhtsk,bhsk->bhts', qi, decay, ki)
        o = (jnp.einsum('bhts,bhsv->bhtv', attn, vi)
             + jnp.einsum('bhtk,bhkv->bhtv', qi * jnp.exp(b), S))
        b_last = b[:, :, -1:, :]
        S = (jnp.exp(b_last[:, :, 0, :, None]) * S
             + jnp.einsum('bhsk,bhsv->bhkv', ki * jnp.exp(b_last - b), vi))
        return S, o

    S0 = jnp.zeros((B, H, DK, DV), jnp.float32)
    _, o = lax.scan(step, S0, (to_chunks(q), to_chunks(k), to_chunks(logf), to_chunks(v)))
    return jnp.moveaxis(o, 0, 2).reshape(B, H, T, DV)


def hgrn2_mixer(q, f_fwd, f_bwd, i, g, lb, onorm_g):
    dt = q.dtype
    B, L, _ = q.shape
    pad = (-N_META) % HG_CHUNK

    def heads(t):
        t = jnp.pad(t.astype(jnp.float32), ((0, 0), (pad, 0), (0, 0)))
        return t.reshape(B, L + pad, HG_HEADS, HG_HEAD_DIM).transpose(0, 2, 1, 3)

    qh, vh = heads(q), heads(i)
    outs = []
    for d, f_raw in enumerate((f_fwd, f_bwd)):
        lb_d = lb[d].astype(jnp.float32)
        logf = jnp.logaddexp(jnp.log(jnp.maximum(lb_d, LB_FLOOR)),
                             jnp.log1p(-lb_d) + jax.nn.log_sigmoid(f_raw.astype(jnp.float32)))
        seqs = (qh, heads(-jnp.expm1(logf)), heads(logf), vh)
        if d == 1:
            seqs = tuple(jnp.flip(t, axis=2) for t in seqs)
        o = gla_chunk_scan(*seqs)
        if d == 1:
            o = jnp.flip(o, axis=2)
        outs.append(o)
    o = (outs[0] + outs[1]).transpose(0, 2, 1, 3)[:, pad:]
    o = rms_norm(o, onorm_g.reshape(HG_HEADS, HG_HEAD_DIM)).reshape(B, L, HG_WIDTH)
    return (o * jax.nn.silu(g.astype(jnp.float32))).astype(dt)


def shortconv_mixer(b, c, h, conv_w):
    L = h.shape[1]
    half = SC_KSIZE // 2
    u = jnp.pad(c * h, ((0, 0), (half, half), (0, 0)))
    y = u[:, 0:L] * conv_w[0]
    for j in range(1, SC_KSIZE):
        y = y + u[:, j:j + L] * conv_w[j]
    return (b * y).astype(h.dtype)


def diff_attention_mixer(q, k, v, qn_g, kn_g, lam_p, subln_g, lam_init, pos):
    dt = v.dtype
    B, L, _ = q.shape
    q = q.reshape(B, L, DA_HEADS, 2, DA_QK_DIM).transpose(0, 2, 3, 1, 4)
    k = k.reshape(B, L, DA_HEADS, 2, DA_QK_DIM).transpose(0, 2, 3, 1, 4)
    v = v.reshape(B, L, DA_HEADS, DA_V_DIM).transpose(0, 2, 1, 3)
    q = rope_partial(rms_norm(q, qn_g), pos) * (DA_QK_DIM ** -0.5)
    k = rope_partial(rms_norm(k, kn_g), pos)
    lam_p = lam_p.astype(jnp.float32)
    lam = jnp.exp(jnp.sum(lam_p[0] * lam_p[1])) - jnp.exp(jnp.sum(lam_p[2] * lam_p[3])) + lam_init
    n_blk = -(-L // Q_BLOCK)
    qp = jnp.pad(q, ((0, 0), (0, 0), (0, 0), (0, n_blk * Q_BLOCK - L), (0, 0)))
    qb = jnp.moveaxis(qp.reshape(B, DA_HEADS, 2, n_blk, Q_BLOCK, DA_QK_DIM), 3, 0)

    def block(qi):
        s = jnp.einsum('bhmqd,bhmkd->bhmqk', qi, k).astype(jnp.float32)
        p = jax.nn.softmax(s, axis=-1)
        w = p[:, :, 0] - lam * p[:, :, 1]
        return jnp.einsum('bhqk,bhkv->bhqv', w.astype(dt), v)

    o = lax.map(block, qb)
    o = jnp.moveaxis(o, 0, 2).reshape(B, DA_HEADS, n_blk * Q_BLOCK, DA_V_DIM)[:, :, :L]
    o = rms_norm(o, subln_g, SUBLN_EPS) * (1.0 - lam_init)
    return o.transpose(0, 2, 1, 3).reshape(B, L, DA_WIDTH).astype(dt)


def rwkv7_scan(r, w, k, v, kk, a):
    B, L, H, N = r.shape

    def step(S, inp):
        rt, wt, kt, vt, kkt, at = inp
        sa = jnp.einsum('bhvk,bhk->bhv', S, -kkt)
        S = (S * wt[:, :, None, :] + sa[..., None] * (kkt * at)[:, :, None, :]
             + vt[..., None] * kt[:, :, None, :])
        return S, jnp.einsum('bhvk,bhk->bhv', S, rt)

    xs = tuple(jnp.moveaxis(t, 1, 0) for t in (r, w, k, v, kk, a))
    _, o = lax.scan(step, jnp.zeros((B, H, N, N), jnp.float32), xs)
    return jnp.moveaxis(o, 0, 1)


def rwkv7_mixer(cols, mu, w0, w2, a0, a2, g2, k_k, k_a, r_k, lnx_g, lnx_b):
    dt = cols.dtype
    B, L, _ = cols.shape
    u = cols.astype(jnp.float32)
    xm = u + mu * (centred_shift(u) - u)
    r, k, v, wl_f, wl_b, al_f, al_b, gl = _split(xm, RW_SIZES)

    def heads(t):
        return t.reshape(B, L, RW_HEADS, RW_HEAD_DIM)

    kk = heads(k * k_k)
    kk = kk / jnp.maximum(jnp.sqrt(jnp.sum(kk * kk, axis=-1, keepdims=True)), 1e-12)
    rh, vh = heads(r), heads(v)
    outs, keys = [], []
    for d, (wl, al) in enumerate(((wl_f, al_f), (wl_b, al_b))):
        wlog = -jax.nn.softplus(-(w0[d] + jnp.tanh(wl) @ w2[d])) - 0.5
        decay = heads(jnp.exp(-jnp.exp(wlog)))
        a = jax.nn.sigmoid(a0[d] + al @ a2[d])
        kd = heads(k * (1.0 + (a - 1.0) * k_a))
        seqs = (rh, decay, kd, vh, kk, heads(a))
        if d == 1:
            seqs = tuple(jnp.flip(t, axis=1) for t in seqs)
        o = rwkv7_scan(*seqs)
        if d == 1:
            o = jnp.flip(o, axis=1)
        outs.append(o)
        keys.append(kd)
    o = outs[0] + outs[1]
    mean = jnp.mean(o, axis=-1, keepdims=True)
    var = jnp.mean(jnp.square(o - mean), axis=-1, keepdims=True)
    o = ((o - mean) * lax.rsqrt(var + RW_LNX_EPS)).reshape(B, L, RW_WIDTH) * lnx_g + lnx_b
    bonus = jnp.sum(rh * (keys[0] + keys[1]) * r_k, axis=-1, keepdims=True) * vh
    y = (o + bonus.reshape(B, L, RW_WIDTH)) * (jax.nn.sigmoid(gl) @ g2)
    return y.astype(dt)


def encoder_layer(x, l, p, lb, pos):
    B, L, _ = x.shape
    h = rms_norm(x, p['norm_mix_g'][l])
    z = h @ p['w_in'][l]
    hq, hf_f, hf_b, hi, hg, sb, sc, sh, dq, dk, dv, rw = _split(z, IN_SIZES)
    y_hg = hgrn2_mixer(hq, hf_f, hf_b, hi, hg, lb, p['hgrn_onorm_g'][l])
    y_sc = shortconv_mixer(sb, sc, sh, p['conv_w'][l])
    y_da = diff_attention_mixer(dq, dk, dv, p['diff_qnorm_g'][l], p['diff_knorm_g'][l],
                                p['diff_lambda'][l], p['diff_subln_g'][l],
                                0.8 - 0.6 * math.exp(-0.3 * l), pos)
    y_rw = rwkv7_mixer(rw, p['rwkv_mu'][l], p['rwkv_w0'][l], p['rwkv_w2'][l], p['rwkv_a0'][l],
                       p['rwkv_a2'][l], p['rwkv_g2'][l], p['rwkv_k_k'][l], p['rwkv_k_a'][l],
                       p['rwkv_r_k'][l], p['rwkv_lnx_g'][l], p['rwkv_lnx_b'][l])
    gates = jax.nn.sigmoid(h @ p['w_gate'][l]).reshape(B, L, N_BRANCH, D_MODEL)
    merged = gates[:, :, 0] * (y_hg @ p['branch_proj'][l, 0])
    for n, y_n in enumerate((y_sc, y_da, y_rw), start=1):
        merged = merged + gates[:, :, n] * (y_n @ p['branch_proj'][l, n])
    x = x + merged @ p['w_out'][l]
    h2 = rms_norm(x, p['norm_mlp_g'][l])
    x = x + jnp.square(jax.nn.relu(h2 @ p['mlp_w1'][l])) @ p['mlp_w2'][l]
    return x


def trunk(x, p):
    B, S, _ = x.shape
    meta = jnp.broadcast_to(p['meta_tokens'].astype(x.dtype)[None], (B, N_META, D_MODEL))
    h = jnp.concatenate([meta, x], axis=1)
    pos = jnp.arange(N_META + S, dtype=jnp.float32)
    sm = jax.nn.softmax(p['hgrn_lb_logits'].astype(jnp.float32), axis=1)
    lb = jnp.cumsum(sm, axis=1) - sm[:, :1]
    for l in range(DEPTH):
        h = encoder_layer(h, l, p, lb[:, l], pos)
    return h[:, N_META:]


def setup_inputs(seed: int = 0) -> dict:
    key = jax.random.key(seed)
    keys = jax.random.split(key, 40)
    ks = (keys[i] for i in range(40))

    def nrm(shape, scale):
        return scale * jax.random.normal(next(ks), shape, jnp.float32)

    def gain(shape):
        return 1.0 + 0.02 * jax.random.normal(next(ks), shape, jnp.float32)

    def unif(shape, lo, hi):
        return jax.random.uniform(next(ks), shape, jnp.float32, lo, hi)

    return {
        'x_prompt': nrm((BATCH, SEQ, D_MODEL), 1.0),
        'x_sample': nrm((DEC_BATCH, DEC_SEQ, D_MODEL), 1.0),
        'meta_tokens': nrm((N_META, D_MODEL), 1.0),
        'norm_mix_g': gain((DEPTH, D_MODEL)),
        'w_in': nrm((DEPTH, D_MODEL, N_IN), D_MODEL ** -0.5),
        'hgrn_lb_logits': nrm((2, DEPTH, HG_WIDTH), 0.5),
        'hgrn_onorm_g': gain((DEPTH, HG_WIDTH)),
        'conv_w': nrm((DEPTH, SC_KSIZE, SC_WIDTH), SC_KSIZE ** -0.5),
        'diff_qnorm_g': gain((DEPTH, DA_QK_DIM)),
        'diff_knorm_g': gain((DEPTH, DA_QK_DIM)),
        'diff_lambda': nrm((DEPTH, 4, DA_QK_DIM), 0.1),
        'diff_subln_g': gain((DEPTH, DA_V_DIM)),
        'rwkv_mu': unif((DEPTH, RW_COLS), 0.0, 1.0),
        'rwkv_w0': unif((DEPTH, 2, RW_WIDTH), -6.5, -1.5),
        'rwkv_w2': nrm((DEPTH, 2, RW_DECAY_RANK, RW_WIDTH), 0.1 * RW_DECAY_RANK ** -0.5),
        'rwkv_a0': nrm((DEPTH, 2, RW_WIDTH), 0.1),
        'rwkv_a2': nrm((DEPTH, 2, RW_ICLR_RANK, RW_WIDTH), RW_ICLR_RANK ** -0.5),
        'rwkv_g2': nrm((DEPTH, RW_GATE_RANK, RW_WIDTH), RW_GATE_RANK ** -0.5),
        'rwkv_k_k': 0.85 + nrm((DEPTH, RW_WIDTH), 0.02),
        'rwkv_k_a': gain((DEPTH, RW_WIDTH)),
        'rwkv_r_k': nrm((DEPTH, RW_HEADS, RW_HEAD_DIM), 0.1),
        'rwkv_lnx_g': gain((DEPTH, RW_WIDTH)),
        'rwkv_lnx_b': nrm((DEPTH, RW_WIDTH), 0.02),
        'w_gate': nrm((DEPTH, D_MODEL, N_BRANCH * D_MODEL), D_MODEL ** -0.5),
        'branch_proj': nrm((DEPTH, N_BRANCH, BRANCH_WIDTH, D_MODEL), BRANCH_WIDTH ** -0.5),
        'w_out': nrm((DEPTH, D_MODEL, D_MODEL), D_MODEL ** -0.5),
        'norm_mlp_g': gain((DEPTH, D_MODEL)),
        'mlp_w1': nrm((DEPTH, D_MODEL, D_FF), D_MODEL ** -0.5),
        'mlp_w2': nrm((DEPTH, D_FF, D_MODEL), D_FF ** -0.5),
    }


def reference(x_prompt, x_sample, meta_tokens, norm_mix_g, w_in, hgrn_lb_logits, hgrn_onorm_g,
              conv_w, diff_qnorm_g, diff_knorm_g, diff_lambda, diff_subln_g, rwkv_mu, rwkv_w0,
              rwkv_w2, rwkv_a0, rwkv_a2, rwkv_g2, rwkv_k_k, rwkv_k_a, rwkv_r_k, rwkv_lnx_g,
              rwkv_lnx_b, w_gate, branch_proj, w_out, norm_mlp_g, mlp_w1, mlp_w2):
    p = {
        'meta_tokens': meta_tokens, 'norm_mix_g': norm_mix_g, 'w_in': w_in,
        'hgrn_lb_logits': hgrn_lb_logits, 'hgrn_onorm_g': hgrn_onorm_g, 'conv_w': conv_w,
        'diff_qnorm_g': diff_qnorm_g, 'diff_knorm_g': diff_knorm_g, 'diff_lambda': diff_lambda,
        'diff_subln_g': diff_subln_g, 'rwkv_mu': rwkv_mu, 'rwkv_w0': rwkv_w0, 'rwkv_w2': rwkv_w2,
        'rwkv_a0': rwkv_a0, 'rwkv_a2': rwkv_a2, 'rwkv_g2': rwkv_g2, 'rwkv_k_k': rwkv_k_k,
        'rwkv_k_a': rwkv_k_a, 'rwkv_r_k': rwkv_r_k, 'rwkv_lnx_g': rwkv_lnx_g,
        'rwkv_lnx_b': rwkv_lnx_b, 'w_gate': w_gate, 'branch_proj': branch_proj, 'w_out': w_out,
        'norm_mlp_g': norm_mlp_g, 'mlp_w1': mlp_w1, 'mlp_w2': mlp_w2,
    }
    y_prompt = trunk(x_prompt, p)
    y_sample = trunk(x_sample, p)
    return (y_prompt, y_sample)
```

```python
import functools
import math

import jax
import jax.numpy as jnp
from jax import lax
from jax.experimental import pallas as pl
from jax.experimental.pallas import tpu as pltpu

F32 = jnp.float32
BF16 = jnp.bfloat16
HI = lax.Precision.HIGHEST

V7X_LANES = 128
V7X_VMEM_BYTES = 64 * 1024 * 1024
VMEM_LIMIT = V7X_VMEM_BYTES - 8 * 1024 * 1024

N_META = 16
NORM_EPS = 1e-6
N_BRANCH = 4
HG_HEAD_DIM = 128
HG_CHUNK = 64
HG_SUB = 16
LB_FLOOR = 1e-20
SC_KSIZE = 3
DA_HEADS = 4
DA_QK_DIM = 64
DA_V_DIM = 128
ROPE_THETA = 500000.0
ROPE_DIM = DA_QK_DIM // 4
SUBLN_EPS = 1e-5
RW_HEAD_DIM = 64
RW_CHUNK = 64
RW_LNX_EPS = 64e-5
FRONT_PAD = (-N_META) % HG_CHUNK
NEG_BIG = -1e30


def _dot(a, b, precision=None):
    return jnp.dot(a, b, preferred_element_type=F32, precision=precision)


def _dot_nt(a, b, precision=None):
    return lax.dot_general(a, b, (((1,), (1,)), ((), ())),
                           preferred_element_type=F32, precision=precision)


def _dot_tn(a, b, precision=None):
    return lax.dot_general(a, b, (((0,), (0,)), ((), ())),
                           preferred_element_type=F32, precision=precision)


def _pick(n, target, mult):
    best = None
    for d in range(mult, min(n, target) + 1, mult):
        if n % d == 0:
            best = d
    assert best is not None, (n, target, mult)
    return best


def _params(sem):
    return pltpu.CompilerParams(dimension_semantics=sem, vmem_limit_bytes=VMEM_LIMIT)


def _log_sigmoid(x):
    return jnp.minimum(x, 0.0) - jnp.log1p(jnp.exp(-jnp.abs(x)))


def _sigmoid(x):
    return 1.0 / (1.0 + jnp.exp(-x))


def _masked_rms_norm(x, g, valid):
    ms = jnp.mean(x * x, axis=-1, keepdims=True)
    h = x * lax.rsqrt(ms + NORM_EPS) * g
    return jnp.where(valid > 0.0, h, 0.0)


def _norm_matmul_kernel(x_ref, g_ref, valid_ref, w_ref, o_ref, h_scr):
    @pl.when(pl.program_id(1) == 0)
    def _():
        h_scr[...] = _masked_rms_norm(x_ref[...], g_ref[...], valid_ref[...]).astype(BF16)

    o_ref[...] = _dot(h_scr[...], w_ref[...])


def norm_matmul(x2d, g, valid, w_bf16, tm, tn):
    m, d = x2d.shape
    n = w_bf16.shape[1]
    return pl.pallas_call(
        _norm_matmul_kernel,
        grid=(m // tm, n // tn),
        in_specs=[
            pl.BlockSpec((tm, d), lambda i, j: (i, 0)),
            pl.BlockSpec((1, d), lambda i, j: (0, 0)),
            pl.BlockSpec((tm, 1), lambda i, j: (i, 0)),
            pl.BlockSpec((d, tn), lambda i, j: (0, j)),
        ],
        out_specs=pl.BlockSpec((tm, tn), lambda i, j: (i, j)),
        out_shape=jax.ShapeDtypeStruct((m, n), F32),
        scratch_shapes=[pltpu.VMEM((tm, d), BF16)],
        compiler_params=_params(("parallel", "arbitrary")),
        name="norm_matmul",
    )(x2d, g, valid, w_bf16)


def _shortconv_kernel(b_ref, c_ref, h_ref, w_ref, o_ref):
    u = c_ref[0] * h_ref[0]
    p = u.shape[0]
    w = w_ref[...]
    y = pltpu.roll(u, 1, 0) * w[0:1] + u * w[1:2] + pltpu.roll(u, p - 1, 0) * w[2:3]
    o_ref[0] = b_ref[0] * y


def shortconv(z_sc, conv_w):
    bt, p, w3 = z_sc.shape
    width = w3 // 3
    nb = width // V7X_LANES
    blk = (1, p, V7X_LANES)
    return pl.pallas_call(
        _shortconv_kernel,
        grid=(bt, nb),
        in_specs=[
            pl.BlockSpec(blk, lambda b, c: (b, 0, c)),
            pl.BlockSpec(blk, lambda b, c: (b, 0, nb + c)),
            pl.BlockSpec(blk, lambda b, c: (b, 0, 2 * nb + c)),
            pl.BlockSpec((SC_KSIZE, V7X_LANES), lambda b, c: (0, c)),
        ],
        out_specs=pl.BlockSpec(blk, lambda b, c: (b, 0, c)),
        out_shape=jax.ShapeDtypeStruct((bt, p, width), F32),
        compiler_params=_params(("parallel", "parallel")),
        name="shortconv",
    )(z_sc, z_sc, z_sc, conv_w)


def _attn_kernel(q_ref, k_ref, v_ref, cq_ref, s1q_ref, s2q_ref, ck_ref, s1k_ref, s2k_ref,
                 bd_ref, qg_ref, kg_ref, lam_ref, sg_ref, bias_ref, o_ref, k_scr, v_scr,
                 *, lam_init):
    bd = bd_ref[...]

    def norm_rope(x, g, c, s1, s2):
        ms = _dot(x * x, bd, HI)
        y = x * lax.rsqrt(ms + NORM_EPS) * g
        return (y * c + pltpu.roll(y, V7X_LANES - ROPE_DIM // 2, 1) * s1
                + pltpu.roll(y, ROPE_DIM // 2, 1) * s2)

    @pl.when(pl.program_id(2) == 0)
    def _():
        k = norm_rope(k_ref[0], kg_ref[...], ck_ref[...], s1k_ref[...], s2k_ref[...])
        k_scr[...] = k.astype(BF16)
        v_scr[...] = v_ref[0].astype(BF16)

    q = norm_rope(q_ref[0], qg_ref[...], cq_ref[...], s1q_ref[...], s2q_ref[...])
    q = q * (DA_QK_DIM ** -0.5)
    lane = lax.broadcasted_iota(jnp.int32, q.shape, 1)
    lp = lam_ref[...]
    lam = (jnp.exp(jnp.sum(lp[0:1] * lp[1:2], axis=-1, keepdims=True))
           - jnp.exp(jnp.sum(lp[2:3] * lp[3:4], axis=-1, keepdims=True)) + lam_init)
    k = k_scr[...]
    bias = bias_ref[...]
    probs = []
    for m in range(2):
        qm = jnp.where(lane // DA_QK_DIM == m, q, 0.0).astype(BF16)
        s = _dot_nt(qm, k) + bias
        p = jnp.exp(s - jnp.max(s, axis=-1, keepdims=True))
        probs.append((p, jnp.sum(p, axis=-1, keepdims=True)))
    w = probs[0][0] * (1.0 / probs[0][1]) - probs[1][0] * (lam / probs[1][1])
    o = _dot(w.astype(BF16), v_scr[...])
    ms = jnp.mean(o * o, axis=-1, keepdims=True)
    o_ref[0] = o * lax.rsqrt(ms + SUBLN_EPS) * sg_ref[...] * (1.0 - lam_init)


def diff_attention(z_da, tables, bd64, qn_g, kn_g, lam_p, subln_g, key_bias, lam_init, tq):
    bt, p, _ = z_da.shape
    c_t, s1_t, s2_t = tables
    nh = DA_HEADS
    qspec = pl.BlockSpec((1, tq, V7X_LANES), lambda b, h, i: (b, i, h))
    kspec = pl.BlockSpec((1, p, V7X_LANES), lambda b, h, i: (b, 0, nh + h))
    vspec = pl.BlockSpec((1, p, V7X_LANES), lambda b, h, i: (b, 0, 2 * nh + h))
    tq_spec = pl.BlockSpec((tq, V7X_LANES), lambda b, h, i: (i, 0))
    tk_spec = pl.BlockSpec((p, V7X_LANES), lambda b, h, i: (0, 0))
    full = lambda shape: pl.BlockSpec(shape, lambda b, h, i: (0,) * len(shape))
    return pl.pallas_call(
        functools.partial(_attn_kernel, lam_init=lam_init),
        grid=(bt, nh, p // tq),
        in_specs=[qspec, kspec, vspec, tq_spec, tq_spec, tq_spec, tk_spec, tk_spec, tk_spec,
                  full((V7X_LANES, V7X_LANES)), full((1, V7X_LANES)), full((1, V7X_LANES)),
                  full((4, DA_QK_DIM)), full((1, V7X_LANES)), full((1, p))],
        out_specs=pl.BlockSpec((1, tq, V7X_LANES), lambda b, h, i: (b, i, h)),
        out_shape=jax.ShapeDtypeStruct((bt, p, nh * DA_V_DIM), F32),
        scratch_shapes=[pltpu.VMEM((p, V7X_LANES), BF16), pltpu.VMEM((p, V7X_LANES), BF16)],
        compiler_params=_params(("parallel", "parallel", "arbitrary")),
        name="diff_attention",
    )(z_da, z_da, z_da, c_t, s1_t, s2_t, c_t, s1_t, s2_t, bd64, qn_g, kn_g, lam_p, subln_g,
      key_bias)


def _gla_chunk(q, k, g, v, st, reverse):
    c, sub = HG_CHUNK, HG_SUB
    nsub = c // sub
    t_idx = lax.broadcasted_iota(jnp.int32, (c, c), 0)
    s_idx = lax.broadcasted_iota(jnp.int32, (c, c), 1)
    if reverse:
        tri = s_idx >= t_idx
        piv = s_idx >= (t_idx // sub + 1) * sub
    else:
        tri = s_idx <= t_idx
        piv = s_idx < (t_idx // sub) * sub
    cum = _dot(jnp.concatenate([tri.astype(F32), piv.astype(F32)], axis=0), g, HI)
    b, bpiv = cum[:c], cum[c:]
    qt = q * jnp.exp(b - bpiv)

    row = lax.broadcasted_iota(jnp.int32, (c, 1), 0)
    attn = jnp.zeros((c, c), F32)
    for i in range(nsub):
        if (reverse and i == nsub - 1) or (not reverse and i == 0):
            continue
        bi = bpiv[i * sub:i * sub + 1]
        kmask = (row >= (i + 1) * sub) if reverse else (row < i * sub)
        kt = jnp.where(kmask, k * jnp.exp(jnp.minimum(bi - b, 0.0)), 0.0)
        attn = jnp.where(row // sub == i, _dot_nt(qt, kt, HI), attn)
    o = _dot(attn, v, HI)

    tloc = lax.broadcasted_iota(jnp.int32, (sub, 1), 0)
    parts = []
    for i in range(nsub):
        bt, qi = b[i * sub:(i + 1) * sub], q[i * sub:(i + 1) * sub]
        acc = jnp.zeros((sub, HG_HEAD_DIM), F32)
        for j in range(sub):
            s = i * sub + j
            m = (tloc <= j) if reverse else (tloc >= j)
            e = jnp.exp(jnp.where(m, bt - b[s:s + 1], NEG_BIG))
            a = jnp.sum(qi * k[s:s + 1] * e, axis=-1, keepdims=True)
            acc = acc + a * v[s:s + 1]
        parts.append(acc)
    o = o + jnp.concatenate(parts, axis=0)

    o = o + _dot_nt(q * jnp.exp(b), st, HI)
    btot = b[0:1] if reverse else b[c - 1:c]
    st_new = st * jnp.exp(btot) + _dot_tn(v, k * jnp.exp(btot - b), HI)
    return o, st_new


def _hgrn_kernel(qf_ref, ff_ref, vf_ref, mf_ref, qb_ref, fb_ref, vb_ref, mb_ref, lb_ref,
                 of_ref, ob_ref, st_scr):
    @pl.when(pl.program_id(2) == 0)
    def _():
        st_scr[...] = jnp.zeros_like(st_scr)

    dirs = ((qf_ref, ff_ref, vf_ref, mf_ref, of_ref), (qb_ref, fb_ref, vb_ref, mb_ref, ob_ref))
    for d, (q_ref, f_ref, v_ref, m_ref, o_ref) in enumerate(dirs):
        valid = m_ref[...] > 0.0
        lb = lb_ref[d:d + 1, :]
        la = jnp.log(jnp.maximum(lb, LB_FLOOR))
        lc = jnp.log1p(-lb) + _log_sigmoid(f_ref[0])
        logf = jnp.maximum(la, lc) + jnp.log1p(jnp.exp(-jnp.abs(la - lc)))
        logf = jnp.where(valid, logf, 0.0)
        k = jnp.where(valid, 1.0 - jnp.exp(logf), 0.0)
        o, st_new = _gla_chunk(q_ref[0], k, logf, v_ref[0], st_scr[d], reverse=(d == 1))
        o_ref[0] = o
        st_scr[d] = st_new


def hgrn_scan(z_hg, valid_seq, lb):
    bt, p, w5 = z_hg.shape
    nh = w5 // 5 // HG_HEAD_DIM
    n = p // HG_CHUNK
    blk = (1, HG_CHUNK, HG_HEAD_DIM)

    def spec(group, rev):
        if rev:
            return pl.BlockSpec(blk, lambda b, h, c: (b, n - 1 - c, group * nh + h))
        return pl.BlockSpec(blk, lambda b, h, c: (b, c, group * nh + h))

    mf = pl.BlockSpec((HG_CHUNK, 1), lambda b, h, c: (c, 0))
    mb = pl.BlockSpec((HG_CHUNK, 1), lambda b, h, c: (n - 1 - c, 0))
    out = jax.ShapeDtypeStruct((bt, p, nh * HG_HEAD_DIM), F32)
    return pl.pallas_call(
        _hgrn_kernel,
        grid=(bt, nh, n),
        in_specs=[spec(0, False), spec(1, False), spec(3, False), mf,
                  spec(0, True), spec(2, True), spec(3, True), mb,
                  pl.BlockSpec((2, HG_HEAD_DIM), lambda b, h, c: (0, h))],
        out_specs=[pl.BlockSpec(blk, lambda b, h, c: (b, c, h)),
                   pl.BlockSpec(blk, lambda b, h, c: (b, n - 1 - c, h))],
        out_shape=[out, out],
        scratch_shapes=[pltpu.VMEM((2, HG_HEAD_DIM, HG_HEAD_DIM), F32)],
        compiler_params=_params(("parallel", "parallel", "arbitrary")),
        name="hgrn_scan",
    )(z_hg, z_hg, z_hg, valid_seq, z_hg, z_hg, z_hg, valid_seq, lb)


def _hgrn_post_kernel(of_ref, ob_ref, g_ref, gain_ref, o_ref):
    o = of_ref[0] + ob_ref[0]
    ms = jnp.mean(o * o, axis=-1, keepdims=True)
    g = g_ref[0]
    o_ref[0] = o * lax.rsqrt(ms + NORM_EPS) * gain_ref[...] * (g * _sigmoid(g))


def hgrn_post(o_f, o_b, z_hg, onorm_g, tr):
    bt, p, width = o_f.shape
    nh = width // HG_HEAD_DIM
    blk = (1, tr, HG_HEAD_DIM)
    spec = pl.BlockSpec(blk, lambda b, i, h: (b, i, h))
    return pl.pallas_call(
        _hgrn_post_kernel,
        grid=(bt, p // tr, nh),
        in_specs=[spec, spec, pl.BlockSpec(blk, lambda b, i, h: (b, i, 4 * nh + h)),
                  pl.BlockSpec((1, HG_HEAD_DIM), lambda b, i, h: (0, h))],
        out_specs=spec,
        out_shape=jax.ShapeDtypeStruct((bt, p, width), F32),
        compiler_params=_params(("parallel", "parallel", "parallel")),
        name="hgrn_post",
    )(o_f, o_b, z_hg, onorm_g)


def _group_sum(x, bd):
    cols = [_dot(x[:, j:j + V7X_LANES], bd, HI) for j in range(0, x.shape[1], V7X_LANES)]
    return jnp.concatenate(cols, axis=1)


def _rwkv_prep_kernel(z_ref, zp_ref, zn_ref, valid_ref, mu_ref, w0_ref, w2_ref, a0_ref, a2_ref,
                      g2_ref, kk_ref, ka_ref, rk_ref, bd_ref,
                      r_out, v_out, nkk_out, lwf_out, bef_out, kdf_out, lwb_out, beb_out, kdb_out,
                      gate_out, bonus_out):
    u = z_ref[0]
    tr = u.shape[0]
    width = r_out.shape[-1]
    row = lax.broadcasted_iota(jnp.int32, (tr, 1), 0)
    u_prev = jnp.where(row == 0, zp_ref[0][7:8], pltpu.roll(u, 1, 0))
    u_next = jnp.where(row == tr - 1, zn_ref[0][0:1], pltpu.roll(u, tr - 1, 0))
    xm = u + mu_ref[...] * (0.5 * (u_prev + u_next) - u)
    valid = valid_ref[...] > 0.0
    bd = bd_ref[...]
    r = xm[:, 0:width]
    k = xm[:, width:2 * width]
    v = jnp.where(valid, xm[:, 2 * width:3 * width], 0.0)
    lr = 3 * width
    wl = jnp.tanh(xm[:, lr:lr + V7X_LANES])
    al = xm[:, lr + V7X_LANES:lr + 2 * V7X_LANES]
    gl = xm[:, lr + 2 * V7X_LANES:lr + 3 * V7X_LANES]
    kk = k * kk_ref[...]
    kk = kk / jnp.maximum(jnp.sqrt(_group_sum(kk * kk, bd)), 1e-12)
    kk = jnp.where(valid, kk, 0.0)
    r_out[0] = r
    v_out[0] = v
    nkk_out[0] = -kk
    kd_sum = jnp.zeros_like(k)
    for d, (lw_out, be_out, kd_out) in enumerate(((lwf_out, bef_out, kdf_out),
                                                   (lwb_out, beb_out, kdb_out))):
        wlog = _log_sigmoid(w0_ref[d:d + 1, :] + _dot(wl, w2_ref[d], HI)) - 0.5
        lw_out[0] = jnp.where(valid, -jnp.exp(wlog), 0.0)
        a = _sigmoid(a0_ref[d:d + 1, :] + _dot(al, a2_ref[d], HI))
        kd = k * (1.0 + (a - 1.0) * ka_ref[...])
        kd_sum = kd_sum + kd
        kd_out[0] = jnp.where(valid, kd, 0.0)
        be_out[0] = kk * a
    gate_out[0] = _dot(_sigmoid(gl), g2_ref[...], HI)
    bonus_out[0] = _group_sum(r * kd_sum * rk_ref[...], bd) * v


def rwkv_prep(z_rw, valid2d, mu, w0, w2pad, a0, a2pad, g2, k_k, k_a, r_k, bd_ones, tr):
    bt, p, cols = z_rw.shape
    width = w0.shape[1]
    nt = p // tr
    r8 = tr // 8
    last8 = p // 8 - 1
    full = lambda a: pl.BlockSpec(a.shape, lambda b, i: (0,) * a.ndim)
    out = jax.ShapeDtypeStruct((bt, p, width), F32)
    ospec = pl.BlockSpec((1, tr, width), lambda b, i: (b, i, 0))
    params = (mu, w0, w2pad, a0, a2pad, g2, k_k, k_a, r_k, bd_ones)
    return pl.pallas_call(
        _rwkv_prep_kernel,
        grid=(bt, nt),
        in_specs=[pl.BlockSpec((1, tr, cols), lambda b, i: (b, i, 0)),
                  pl.BlockSpec((1, 8, cols), lambda b, i: (b, jnp.maximum(i * r8 - 1, 0), 0)),
                  pl.BlockSpec((1, 8, cols), lambda b, i: (b, jnp.minimum((i + 1) * r8, last8), 0)),
                  pl.BlockSpec((tr, 1), lambda b, i: (i, 0))] + [full(a) for a in params],
        out_specs=[ospec] * 11,
        out_shape=[out] * 11,
        compiler_params=_params(("parallel", "parallel")),
        name="rwkv_prep",
    )(z_rw, z_rw, z_rw, valid2d, *params)


def _rwkv_pair_chunk(r, v, al, be, kd, lg, lgx, hpt, reverse):
    c = RW_CHUNK
    lane = lax.broadcasted_iota(jnp.int32, (c, V7X_LANES), 1)
    first = lane < RW_HEAD_DIM

    def stack(x):
        return jnp.concatenate([jnp.where(first, x, 0.0), jnp.where(first, 0.0, x)], axis=0)

    ltot = lg[0:1] if reverse else lg[c - 1:c]
    a_s = stack(al * jnp.exp(lgx))
    r_s = stack(r * jnp.exp(lg))
    ginv = jnp.exp(-lg)
    b_s = stack(be * ginv)
    k_s = stack(kd * ginv)
    gend = jnp.exp(ltot - lg)
    bh_s = stack(be * gend)
    kh_s = stack(kd * gend)
    v_s = stack(v)

    ar = jnp.concatenate([a_s, r_s], axis=0)
    sc = _dot_nt(ar, jnp.concatenate([b_s, k_s], axis=0), HI)
    n2 = 2 * c
    ti = lax.broadcasted_iota(jnp.int32, (n2, n2), 0)
    si = lax.broadcasted_iota(jnp.int32, (n2, n2), 1)
    same = (ti // c) == (si // c)
    tl, sl = ti % c, si % c
    strict = same & ((sl > tl) if reverse else (sl < tl))
    incl = same & ((sl >= tl) if reverse else (sl <= tl))
    l_ab = jnp.where(strict, sc[:n2, :n2], 0.0)
    l_ak = jnp.where(strict, sc[:n2, n2:], 0.0)
    m_rb = jnp.where(incl, sc[n2:, :n2], 0.0)
    m_rk = jnp.where(incl, sc[n2:, n2:], 0.0)

    pw = l_ab
    tinv = jnp.where(ti == si, 1.0, 0.0) + l_ab
    for _ in range(int(math.log2(c)) - 1):
        pw = _dot(pw, pw, HI)
        tinv = tinv + _dot(tinv, pw, HI)

    ah = _dot_nt(ar, hpt, HI)
    u_s = _dot(tinv, ah[:n2] + _dot(l_ak, v_s, HI), HI)
    uv = jnp.concatenate([u_s, v_s], axis=0)
    o_s = ah[n2:] + _dot(jnp.concatenate([m_rb, m_rk], axis=1), uv, HI)
    hpt_new = hpt * jnp.exp(ltot) + _dot_tn(uv, jnp.concatenate([bh_s, kh_s], axis=0), HI)
    return o_s[:c] + o_s[c:], hpt_new


def _rwkv_chunk_kernel(rf_ref, vf_ref, af_ref, lwf_ref, bef_ref, kdf_ref,
                       rb_ref, vb_ref, ab_ref, lwb_ref, beb_ref, kdb_ref,
                       of_ref, ob_ref, h_scr):
    @pl.when(pl.program_id(1) == 0)
    def _():
        h_scr[...] = jnp.zeros_like(h_scr)

    c = RW_CHUNK
    t_idx = lax.broadcasted_iota(jnp.int32, (c, c), 0)
    s_idx = lax.broadcasted_iota(jnp.int32, (c, c), 1)
    dirs = ((rf_ref, vf_ref, af_ref, lwf_ref, bef_ref, kdf_ref, of_ref),
            (rb_ref, vb_ref, ab_ref, lwb_ref, beb_ref, kdb_ref, ob_ref))
    for d, (r_ref, v_ref, a_ref, lw_ref, be_ref, kd_ref, o_ref) in enumerate(dirs):
        reverse = d == 1
        tri = ((s_idx >= t_idx) if reverse else (s_idx <= t_idx)).astype(F32)
        lw = lw_ref[0]
        lg = _dot(tri, lw, HI)
        lgx = lg - lw
        for p in range(lw.shape[1] // V7X_LANES):
            sl = slice(p * V7X_LANES, (p + 1) * V7X_LANES)
            o, h_new = _rwkv_pair_chunk(r_ref[0, :, sl], v_ref[0, :, sl], a_ref[0, :, sl],
                                        be_ref[0, :, sl], kd_ref[0, :, sl], lg[:, sl], lgx[:, sl],
                                        h_scr[d, p], reverse)
            o_ref[0, :, sl] = o
            h_scr[d, p] = h_new


def rwkv_scan(r, v, nkk, lw_f, be_f, kd_f, lw_b, be_b, kd_b):
    bt, p, width = r.shape
    n = p // RW_CHUNK
    blk = (1, RW_CHUNK, width)
    fwd = pl.BlockSpec(blk, lambda b, c: (b, c, 0))
    bwd = pl.BlockSpec(blk, lambda b, c: (b, n - 1 - c, 0))
    out = jax.ShapeDtypeStruct((bt, p, width), F32)
    return pl.pallas_call(
        _rwkv_chunk_kernel,
        grid=(bt, n),
        in_specs=[fwd] * 6 + [bwd] * 6,
        out_specs=[fwd, bwd],
        out_shape=[out, out],
        scratch_shapes=[pltpu.VMEM((2, width // V7X_LANES, V7X_LANES, V7X_LANES), F32)],
        compiler_params=_params(("parallel", "arbitrary")),
        name="rwkv_scan",
    )(r, v, nkk, lw_f, be_f, kd_f, r, v, nkk, lw_b, be_b, kd_b)


def _rwkv_post_kernel(of_ref, ob_ref, bonus_ref, gate_ref, g_ref, b_ref, bd_ref, o_ref):
    o = of_ref[0] + ob_ref[0]
    bd = bd_ref[...]
    inv = 1.0 / RW_HEAD_DIM
    mean = _group_sum(o, bd) * inv
    cen = o - mean
    var = _group_sum(cen * cen, bd) * inv
    y = cen * lax.rsqrt(var + RW_LNX_EPS) * g_ref[...] + b_ref[...]
    o_ref[0] = (y + bonus_ref[0]) * gate_ref[0]


def rwkv_post(o_f, o_b, bonus, gate, lnx_g, lnx_b, bd_ones, tr):
    bt, p, width = o_f.shape
    spec = pl.BlockSpec((1, tr, width), lambda b, i: (b, i, 0))
    full = lambda a: pl.BlockSpec(a.shape, lambda b, i: (0,) * a.ndim)
    return pl.pallas_call(
        _rwkv_post_kernel,
        grid=(bt, p // tr),
        in_specs=[spec, spec, spec, spec, full(lnx_g), full(lnx_b), full(bd_ones)],
        out_specs=spec,
        out_shape=jax.ShapeDtypeStruct((bt, p, width), F32),
        compiler_params=_params(("parallel", "parallel")),
        name="rwkv_post",
    )(o_f, o_b, bonus, gate, lnx_g, lnx_b, bd_ones)


def _merge_kernel(x_ref, g_ref, valid_ref, y0_ref, y1_ref, y2_ref, y3_ref, wg_ref, bp_ref, wo_ref,
                  o_ref):
    x = x_ref[...]
    d = x.shape[1]
    h = _masked_rms_norm(x, g_ref[...], valid_ref[...]).astype(BF16)
    merged = jnp.zeros_like(x)
    for n, y_ref in enumerate((y0_ref, y1_ref, y2_ref, y3_ref)):
        gate = _sigmoid(_dot(h, wg_ref[:, n * d:(n + 1) * d]))
        merged = merged + gate * _dot(y_ref[...].astype(BF16), bp_ref[n])
    o_ref[...] = x + _dot(merged.astype(BF16), wo_ref[...])


def merge(x2d, g, valid, ys, w_gate, branch_proj, w_out, tm):
    m, d = x2d.shape
    bw = ys[0].shape[1]
    row = lambda width: pl.BlockSpec((tm, width), lambda i: (i, 0))
    full = lambda a: pl.BlockSpec(a.shape, lambda i: (0,) * a.ndim)
    return pl.pallas_call(
        _merge_kernel,
        grid=(m // tm,),
        in_specs=[row(d), full(g), row(1)] + [row(bw)] * N_BRANCH
                 + [full(w_gate), full(branch_proj), full(w_out)],
        out_specs=row(d),
        out_shape=jax.ShapeDtypeStruct((m, d), F32),
        compiler_params=_params(("parallel",)),
        name="merge",
    )(x2d, g, valid, *ys, w_gate, branch_proj, w_out)


def _mlp_kernel(x_ref, g_ref, w1_ref, w2_ref, o_ref, h_scr, acc_scr):
    f = pl.program_id(1)

    @pl.when(f == 0)
    def _():
        x = x_ref[...]
        ms = jnp.mean(x * x, axis=-1, keepdims=True)
        h_scr[...] = (x * lax.rsqrt(ms + NORM_EPS) * g_ref[...]).astype(BF16)
        acc_scr[...] = jnp.zeros_like(acc_scr)

    a = jnp.maximum(_dot(h_scr[...], w1_ref[...]), 0.0)
    acc_scr[...] += _dot((a * a).astype(BF16), w2_ref[...])

    @pl.when(f == pl.num_programs(1) - 1)
    def _():
        o_ref[...] = x_ref[...] + acc_scr[...]


def mlp(x2d, g, w1, w2, tm, tf):
    m, d = x2d.shape
    dff = w1.shape[1]
    return pl.pallas_call(
        _mlp_kernel,
        grid=(m // tm, dff // tf),
        in_specs=[pl.BlockSpec((tm, d), lambda i, f: (i, 0)),
                  pl.BlockSpec((1, d), lambda i, f: (0, 0)),
                  pl.BlockSpec((d, tf), lambda i, f: (0, f)),
                  pl.BlockSpec((tf, d), lambda i, f: (f, 0))],
        out_specs=pl.BlockSpec((tm, d), lambda i, f: (i, 0)),
        out_shape=jax.ShapeDtypeStruct((m, d), F32),
        scratch_shapes=[pltpu.VMEM((tm, d), BF16), pltpu.VMEM((tm, d), F32)],
        compiler_params=_params(("parallel", "arbitrary")),
        name="mlp",
    )(x2d, g, w1, w2)


def _rope_tables(p):
    half = ROPE_DIM // 2
    pos = jnp.arange(p, dtype=F32) - FRONT_PAD
    inv = ROPE_THETA ** (-jnp.arange(half, dtype=F32) / half)
    ang = pos[:, None] * inv[None, :]
    cos, sin = jnp.cos(ang), jnp.sin(ang)
    ones = jnp.ones((p, DA_QK_DIM - ROPE_DIM), F32)
    zeros = jnp.zeros((p, DA_QK_DIM - ROPE_DIM), F32)
    zh = jnp.zeros((p, half), F32)
    c = jnp.concatenate([cos, cos, ones], axis=1)
    s1 = jnp.concatenate([-sin, zh, zeros], axis=1)
    s2 = jnp.concatenate([zh, sin, zeros], axis=1)
    return tuple(jnp.tile(t, (1, 2)) for t in (c, s1, s2))


def _block_diag_ones(n, blk):
    i = jnp.arange(n)
    return ((i[:, None] // blk) == (i[None, :] // blk)).astype(F32)


def kernel(x_prompt, x_sample, meta_tokens, norm_mix_g, w_in, hgrn_lb_logits, hgrn_onorm_g, conv_w,
           diff_qnorm_g, diff_knorm_g, diff_lambda, diff_subln_g, rwkv_mu, rwkv_w0, rwkv_w2, rwkv_a0,
           rwkv_a2, rwkv_g2, rwkv_k_k, rwkv_k_a, rwkv_r_k, rwkv_lnx_g, rwkv_lnx_b, w_gate,
           branch_proj, w_out, norm_mlp_g, mlp_w1, mlp_w2):
    assert x_prompt.shape[1:] == x_sample.shape[1:]
    x = jnp.concatenate([x_prompt, x_sample], axis=0)
    bt, seq, d = x.shape
    depth = w_in.shape[0]
    length = N_META + seq
    p = -(-(FRONT_PAD + length) // V7X_LANES) * V7X_LANES
    m = bt * p
    bw = branch_proj.shape[2]
    rw_cols = rwkv_mu.shape[1]
    sizes = (5 * bw, 3 * bw, 3 * bw, rw_cols)
    offs = [0]
    for s in sizes:
        offs.append(offs[-1] + s)

    meta = jnp.broadcast_to(meta_tokens.astype(x.dtype)[None], (bt, N_META, d))
    xp = jnp.concatenate([jnp.zeros((bt, FRONT_PAD, d), x.dtype), meta, x,
                          jnp.zeros((bt, p - FRONT_PAD - length, d), x.dtype)], axis=1)
    x2d = xp.reshape(m, d)

    rows = jnp.arange(p)
    valid_seq = ((rows >= FRONT_PAD) & (rows < FRONT_PAD + length)).astype(F32)[:, None]
    valid = jnp.tile(valid_seq, (bt, 1))
    key_bias = jnp.where(valid_seq[:, 0] > 0, 0.0, NEG_BIG).astype(F32)[None, :]
    tables = _rope_tables(p)
    bd64_mean = _block_diag_ones(V7X_LANES, DA_QK_DIM) / DA_QK_DIM
    bd64_ones = _block_diag_ones(V7X_LANES, RW_HEAD_DIM)

    sm = jax.nn.softmax(hgrn_lb_logits.astype(F32), axis=1)
    lb_all = jnp.cumsum(sm, axis=1) - sm[:, :1]

    tm_proj = _pick(m, 1536, 128)
    tm_merge = _pick(m, 256, 128)
    tm_mlp = _pick(m, 768, 128)
    tq = _pick(p, 128, 128)
    tr = _pick(p, 384, 8)

    half = V7X_LANES // 2
    for l in range(depth):
        g_mix = norm_mix_g[l][None, :]
        w_l = w_in[l].astype(BF16)
        z = []
        for gi, size in enumerate(sizes):
            tn = _pick(size, 640, V7X_LANES)
            z.append(norm_matmul(x2d, g_mix, valid, w_l[:, offs[gi]:offs[gi + 1]], tm_proj, tn)
                     .reshape(bt, p, size))
        z_hg, z_sc, z_da, z_rw = z

        of, ob = hgrn_scan(z_hg, valid_seq, lb_all[:, l])
        y_hg = hgrn_post(of, ob, z_hg, hgrn_onorm_g[l][None, :], tr)

        y_sc = shortconv(z_sc, conv_w[l])

        lam_init = 0.8 - 0.6 * math.exp(-0.3 * l)
        y_da = diff_attention(z_da, tables, bd64_mean,
                              jnp.tile(diff_qnorm_g[l], 2)[None, :], jnp.tile(diff_knorm_g[l], 2)[None, :],
                              diff_lambda[l], diff_subln_g[l][None, :], key_bias, lam_init, tq)

        zpad = jnp.zeros((2, half, bw), F32)
        w2pad = jnp.stack([jnp.concatenate([rwkv_w2[l, 0], zpad[0]], 0),
                           jnp.concatenate([zpad[0], rwkv_w2[l, 1]], 0)])
        a2pad = jnp.stack([jnp.concatenate([rwkv_a2[l, 0], zpad[0]], 0),
                           jnp.concatenate([zpad[0], rwkv_a2[l, 1]], 0)])
        (r, v, nkk, lwf, bef, kdf, lwb, beb, kdb, gate, bonus) = rwkv_prep(
            z_rw, valid_seq, rwkv_mu[l][None, :], rwkv_w0[l], w2pad, rwkv_a0[l], a2pad, rwkv_g2[l],
            rwkv_k_k[l][None, :], rwkv_k_a[l][None, :], rwkv_r_k[l].reshape(1, bw), bd64_ones, tr)
        orf, orb = rwkv_scan(r, v, nkk, lwf, bef, kdf, lwb, beb, kdb)
        y_rw = rwkv_post(orf, orb, bonus, gate, rwkv_lnx_g[l][None, :], rwkv_lnx_b[l][None, :],
                         bd64_ones, tr)

        ys = [y.reshape(m, bw) for y in (y_hg, y_sc, y_da, y_rw)]
        x2d = merge(x2d, g_mix, valid, ys, w_gate[l].astype(BF16), branch_proj[l].astype(BF16),
                    w_out[l].astype(BF16), tm_merge)
        x2d = mlp(x2d, norm_mlp_g[l][None, :], mlp_w1[l].astype(BF16), mlp_w2[l].astype(BF16),
                  tm_mlp, _pick(mlp_w1.shape[2], 512, V7X_LANES))

    y = x2d.reshape(bt, p, d)[:, FRONT_PAD + N_META:FRONT_PAD + length]
    nb = x_prompt.shape[0]
    return (y[:nb], y[nb:])
```

```python
import functools
import math

import jax
import jax.numpy as jnp
from jax import lax
from jax.experimental import pallas as pl
from jax.experimental.pallas import tpu as pltpu

F32 = jnp.float32
BF16 = jnp.bfloat16
HI = lax.Precision.HIGHEST

V7X_LANES = 128
V7X_VMEM_BYTES = 64 * 1024 * 1024
VMEM_LIMIT = V7X_VMEM_BYTES - 8 * 1024 * 1024

N_META = 16
NORM_EPS = 1e-6
N_BRANCH = 4
HG_HEAD_DIM = 128
HG_CHUNK = 64
HG_SUB = 16
LB_FLOOR = 1e-20
SC_KSIZE = 3
DA_HEADS = 4
DA_QK_DIM = 64
DA_V_DIM = 128
ROPE_THETA = 500000.0
ROPE_DIM = DA_QK_DIM // 4
SUBLN_EPS = 1e-5
RW_HEAD_DIM = 64
RW_CHUNK = 64
RW_LNX_EPS = 64e-5
FRONT_PAD = (-N_META) % HG_CHUNK
NEG_BIG = -1e30

P_HG = "bf16"
P_RW_LR = "bf16"
RW_CHUNKS_PER_STEP = 2


def _split_bf16(a):
    hi = a.astype(BF16)
    return hi, (a - hi.astype(F32)).astype(BF16)


def _mm(a, b, dims, precision):
    dg = functools.partial(lax.dot_general, dimension_numbers=(dims, ((), ())),
                           preferred_element_type=F32)
    if precision == "bf16":
        return dg(a.astype(BF16), b.astype(BF16))
    if precision == "x3":
        ah, al = _split_bf16(a)
        bh, bl = _split_bf16(b)
        return dg(ah, bh) + (dg(al, bh) + dg(ah, bl))
    return dg(a, b, precision=precision)


def _dot(a, b, precision=None):
    return _mm(a, b, ((1,), (0,)), precision)


def _dot_nt(a, b, precision=None):
    return _mm(a, b, ((1,), (1,)), precision)


def _dot_tn(a, b, precision=None):
    return _mm(a, b, ((0,), (0,)), precision)


def _pick(n, target, mult):
    best = None
    for d in range(mult, min(n, target) + 1, mult):
        if n % d == 0:
            best = d
    assert best is not None, (n, target, mult)
    return best


def _params(sem):
    return pltpu.CompilerParams(dimension_semantics=sem, vmem_limit_bytes=VMEM_LIMIT)


def _log_sigmoid(x):
    return jnp.minimum(x, 0.0) - jnp.log1p(jnp.exp(-jnp.abs(x)))


def _sigmoid(x):
    return 1.0 / (1.0 + jnp.exp(-x))


def _masked_rms_norm(x, g, valid):
    ms = jnp.mean(x * x, axis=-1, keepdims=True)
    h = x * lax.rsqrt(ms + NORM_EPS) * g
    return jnp.where(valid > 0.0, h, 0.0)


def _norm_matmul_kernel(x_ref, g_ref, valid_ref, w_ref, o_ref, h_scr):
    @pl.when(pl.program_id(1) == 0)
    def _():
        h_scr[...] = _masked_rms_norm(x_ref[...], g_ref[...], valid_ref[...]).astype(BF16)

    o_ref[...] = _dot(h_scr[...], w_ref[...])


def norm_matmul(x2d, g, valid, w_bf16, tm, tn):
    m, d = x2d.shape
    n = w_bf16.shape[1]
    return pl.pallas_call(
        _norm_matmul_kernel,
        grid=(m // tm, n // tn),
        in_specs=[
            pl.BlockSpec((tm, d), lambda i, j: (i, 0)),
            pl.BlockSpec((1, d), lambda i, j: (0, 0)),
            pl.BlockSpec((tm, 1), lambda i, j: (i, 0)),
            pl.BlockSpec((d, tn), lambda i, j: (0, j)),
        ],
        out_specs=pl.BlockSpec((tm, tn), lambda i, j: (i, j)),
        out_shape=jax.ShapeDtypeStruct((m, n), F32),
        scratch_shapes=[pltpu.VMEM((tm, d), BF16)],
        compiler_params=_params(("parallel", "arbitrary")),
        name="norm_matmul",
    )(x2d, g, valid, w_bf16)


def _shortconv_kernel(b_ref, c_ref, h_ref, w_ref, o_ref):
    u = c_ref[0] * h_ref[0]
    p = u.shape[0]
    w = w_ref[...]
    y = pltpu.roll(u, 1, 0) * w[0:1] + u * w[1:2] + pltpu.roll(u, p - 1, 0) * w[2:3]
    o_ref[0] = b_ref[0] * y


def shortconv(z_sc, conv_w):
    bt, p, w3 = z_sc.shape
    width = w3 // 3
    nb = width // V7X_LANES
    blk = (1, p, V7X_LANES)
    return pl.pallas_call(
        _shortconv_kernel,
        grid=(bt, nb),
        in_specs=[
            pl.BlockSpec(blk, lambda b, c: (b, 0, c)),
            pl.BlockSpec(blk, lambda b, c: (b, 0, nb + c)),
            pl.BlockSpec(blk, lambda b, c: (b, 0, 2 * nb + c)),
            pl.BlockSpec((SC_KSIZE, V7X_LANES), lambda b, c: (0, c)),
        ],
        out_specs=pl.BlockSpec(blk, lambda b, c: (b, 0, c)),
        out_shape=jax.ShapeDtypeStruct((bt, p, width), F32),
        compiler_params=_params(("parallel", "parallel")),
        name="shortconv",
    )(z_sc, z_sc, z_sc, conv_w)


def _attn_kernel(q_ref, k_ref, v_ref, cq_ref, s1q_ref, s2q_ref, ck_ref, s1k_ref, s2k_ref,
                 bd_ref, qg_ref, kg_ref, lam_ref, sg_ref, bias_ref, o_ref, k_scr, v_scr,
                 *, lam_init):
    bd = bd_ref[...]

    def norm_rope(x, g, c, s1, s2):
        ms = _dot(x * x, bd, HI)
        y = x * lax.rsqrt(ms + NORM_EPS) * g
        return (y * c + pltpu.roll(y, V7X_LANES - ROPE_DIM // 2, 1) * s1
                + pltpu.roll(y, ROPE_DIM // 2, 1) * s2)

    @pl.when(pl.program_id(2) == 0)
    def _():
        k = norm_rope(k_ref[0], kg_ref[...], ck_ref[...], s1k_ref[...], s2k_ref[...])
        k_scr[...] = k.astype(BF16)
        v_scr[...] = v_ref[0].astype(BF16)

    q = norm_rope(q_ref[0], qg_ref[...], cq_ref[...], s1q_ref[...], s2q_ref[...])
    q = q * (DA_QK_DIM ** -0.5)
    lane = lax.broadcasted_iota(jnp.int32, q.shape, 1)
    lp = lam_ref[...]
    lam = (jnp.exp(jnp.sum(lp[0:1] * lp[1:2], axis=-1, keepdims=True))
           - jnp.exp(jnp.sum(lp[2:3] * lp[3:4], axis=-1, keepdims=True)) + lam_init)
    k = k_scr[...]
    bias = bias_ref[...]
    probs = []
    for m in range(2):
        qm = jnp.where(lane // DA_QK_DIM == m, q, 0.0).astype(BF16)
        s = _dot_nt(qm, k) + bias
        p = jnp.exp(s - jnp.max(s, axis=-1, keepdims=True))
        probs.append((p, jnp.sum(p, axis=-1, keepdims=True)))
    w = probs[0][0] * (1.0 / probs[0][1]) - probs[1][0] * (lam / probs[1][1])
    o = _dot(w.astype(BF16), v_scr[...])
    ms = jnp.mean(o * o, axis=-1, keepdims=True)
    o_ref[0] = o * lax.rsqrt(ms + SUBLN_EPS) * sg_ref[...] * (1.0 - lam_init)


def diff_attention(z_da, tables, bd64, qn_g, kn_g, lam_p, subln_g, key_bias, lam_init, tq):
    bt, p, _ = z_da.shape
    c_t, s1_t, s2_t = tables
    nh = DA_HEADS
    qspec = pl.BlockSpec((1, tq, V7X_LANES), lambda b, h, i: (b, i, h))
    kspec = pl.BlockSpec((1, p, V7X_LANES), lambda b, h, i: (b, 0, nh + h))
    vspec = pl.BlockSpec((1, p, V7X_LANES), lambda b, h, i: (b, 0, 2 * nh + h))
    tq_spec = pl.BlockSpec((tq, V7X_LANES), lambda b, h, i: (i, 0))
    tk_spec = pl.BlockSpec((p, V7X_LANES), lambda b, h, i: (0, 0))
    full = lambda shape: pl.BlockSpec(shape, lambda b, h, i: (0,) * len(shape))
    return pl.pallas_call(
        functools.partial(_attn_kernel, lam_init=lam_init),
        grid=(bt, nh, p // tq),
        in_specs=[qspec, kspec, vspec, tq_spec, tq_spec, tq_spec, tk_spec, tk_spec, tk_spec,
                  full((V7X_LANES, V7X_LANES)), full((1, V7X_LANES)), full((1, V7X_LANES)),
                  full((4, DA_QK_DIM)), full((1, V7X_LANES)), full((1, p))],
        out_specs=pl.BlockSpec((1, tq, V7X_LANES), lambda b, h, i: (b, i, h)),
        out_shape=jax.ShapeDtypeStruct((bt, p, nh * DA_V_DIM), F32),
        scratch_shapes=[pltpu.VMEM((p, V7X_LANES), BF16), pltpu.VMEM((p, V7X_LANES), BF16)],
        compiler_params=_params(("parallel", "parallel", "arbitrary")),
        name="diff_attention",
    )(z_da, z_da, z_da, c_t, s1_t, s2_t, c_t, s1_t, s2_t, bd64, qn_g, kn_g, lam_p, subln_g,
      key_bias)


def _gla_chunk(q, k, g, v, st, reverse):
    c, sub = HG_CHUNK, HG_SUB
    nsub = c // sub
    t_idx = lax.broadcasted_iota(jnp.int32, (c, c), 0)
    s_idx = lax.broadcasted_iota(jnp.int32, (c, c), 1)
    if reverse:
        tri = s_idx >= t_idx
        piv = s_idx >= (t_idx // sub + 1) * sub
    else:
        tri = s_idx <= t_idx
        piv = s_idx < (t_idx // sub) * sub
    cum = _dot(jnp.concatenate([tri.astype(F32), piv.astype(F32)], axis=0), g, HI)
    b, bpiv = cum[:c], cum[c:]
    qt = q * jnp.exp(b - bpiv)

    row = lax.broadcasted_iota(jnp.int32, (c, 1), 0)
    attn = jnp.zeros((c, c), F32)
    for i in range(nsub):
        if (reverse and i == nsub - 1) or (not reverse and i == 0):
            continue
        bi = bpiv[i * sub:i * sub + 1]
        kmask = (row >= (i + 1) * sub) if reverse else (row < i * sub)
        kt = jnp.where(kmask, k * jnp.exp(jnp.minimum(bi - b, 0.0)), 0.0)
        attn = jnp.where(row // sub == i, _dot_nt(qt, kt, P_HG), attn)
    o = _dot(attn, v, P_HG)

    tloc = lax.broadcasted_iota(jnp.int32, (sub, 1), 0)
    parts = []
    for i in range(nsub):
        bt, qi = b[i * sub:(i + 1) * sub], q[i * sub:(i + 1) * sub]
        acc = jnp.zeros((sub, HG_HEAD_DIM), F32)
        for j in range(sub):
            s = i * sub + j
            m = (tloc <= j) if reverse else (tloc >= j)
            e = jnp.exp(jnp.where(m, bt - b[s:s + 1], NEG_BIG))
            a = jnp.sum(qi * k[s:s + 1] * e, axis=-1, keepdims=True)
            acc = acc + a * v[s:s + 1]
        parts.append(acc)
    o = o + jnp.concatenate(parts, axis=0)

    o = o + _dot_nt(q * jnp.exp(b), st, P_HG)
    btot = b[0:1] if reverse else b[c - 1:c]
    st_new = st * jnp.exp(btot) + _dot_tn(v, k * jnp.exp(btot - b), P_HG)
    return o, st_new


def _hgrn_kernel(qf_ref, ff_ref, vf_ref, mf_ref, qb_ref, fb_ref, vb_ref, mb_ref, lb_ref,
                 of_ref, ob_ref, st_scr):
    @pl.when(pl.program_id(2) == 0)
    def _():
        st_scr[...] = jnp.zeros_like(st_scr)

    dirs = ((qf_ref, ff_ref, vf_ref, mf_ref, of_ref), (qb_ref, fb_ref, vb_ref, mb_ref, ob_ref))
    for d, (q_ref, f_ref, v_ref, m_ref, o_ref) in enumerate(dirs):
        valid = m_ref[...] > 0.0
        lb = lb_ref[d:d + 1, :]
        la = jnp.log(jnp.maximum(lb, LB_FLOOR))
        lc = jnp.log1p(-lb) + _log_sigmoid(f_ref[0])
        logf = jnp.maximum(la, lc) + jnp.log1p(jnp.exp(-jnp.abs(la - lc)))
        logf = jnp.where(valid, logf, 0.0)
        k = jnp.where(valid, 1.0 - jnp.exp(logf), 0.0)
        o, st_new = _gla_chunk(q_ref[0], k, logf, v_ref[0], st_scr[d], reverse=(d == 1))
        o_ref[0] = o
        st_scr[d] = st_new


def hgrn_scan(z_hg, valid_seq, lb):
    bt, p, w5 = z_hg.shape
    nh = w5 // 5 // HG_HEAD_DIM
    n = p // HG_CHUNK
    blk = (1, HG_CHUNK, HG_HEAD_DIM)

    def spec(group, rev):
        if rev:
            return pl.BlockSpec(blk, lambda b, h, c: (b, n - 1 - c, group * nh + h))
        return pl.BlockSpec(blk, lambda b, h, c: (b, c, group * nh + h))

    mf = pl.BlockSpec((HG_CHUNK, 1), lambda b, h, c: (c, 0))
    mb = pl.BlockSpec((HG_CHUNK, 1), lambda b, h, c: (n - 1 - c, 0))
    out = jax.ShapeDtypeStruct((bt, p, nh * HG_HEAD_DIM), F32)
    return pl.pallas_call(
        _hgrn_kernel,
        grid=(bt, nh, n),
        in_specs=[spec(0, False), spec(1, False), spec(3, False), mf,
                  spec(0, True), spec(2, True), spec(3, True), mb,
                  pl.BlockSpec((2, HG_HEAD_DIM), lambda b, h, c: (0, h))],
        out_specs=[pl.BlockSpec(blk, lambda b, h, c: (b, c, h)),
                   pl.BlockSpec(blk, lambda b, h, c: (b, n - 1 - c, h))],
        out_shape=[out, out],
        scratch_shapes=[pltpu.VMEM((2, HG_HEAD_DIM, HG_HEAD_DIM), F32)],
        compiler_params=_params(("parallel", "parallel", "arbitrary")),
        name="hgrn_scan",
    )(z_hg, z_hg, z_hg, valid_seq, z_hg, z_hg, z_hg, valid_seq, lb)


def _hgrn_post_kernel(of_ref, ob_ref, g_ref, gain_ref, o_ref):
    o = of_ref[0] + ob_ref[0]
    ms = jnp.mean(o * o, axis=-1, keepdims=True)
    g = g_ref[0]
    o_ref[0] = o * lax.rsqrt(ms + NORM_EPS) * gain_ref[...] * (g * _sigmoid(g))


def hgrn_post(o_f, o_b, z_hg, onorm_g, tr):
    bt, p, width = o_f.shape
    nh = width // HG_HEAD_DIM
    blk = (1, tr, HG_HEAD_DIM)
    spec = pl.BlockSpec(blk, lambda b, i, h: (b, i, h))
    return pl.pallas_call(
        _hgrn_post_kernel,
        grid=(bt, p // tr, nh),
        in_specs=[spec, spec, pl.BlockSpec(blk, lambda b, i, h: (b, i, 4 * nh + h)),
                  pl.BlockSpec((1, HG_HEAD_DIM), lambda b, i, h: (0, h))],
        out_specs=spec,
        out_shape=jax.ShapeDtypeStruct((bt, p, width), F32),
        compiler_params=_params(("parallel", "parallel", "parallel")),
        name="hgrn_post",
    )(o_f, o_b, z_hg, onorm_g)


def _group_sum(x, bd):
    cols = [_dot(x[:, j:j + V7X_LANES], bd, HI) for j in range(0, x.shape[1], V7X_LANES)]
    return jnp.concatenate(cols, axis=1)


def _rwkv_prep_kernel(z_ref, zp_ref, zn_ref, valid_ref, mu_ref, w0_ref, w2_ref, a0_ref, a2_ref,
                      g2_ref, kk_ref, ka_ref, rk_ref, bd_ref,
                      r_out, v_out, nkk_out, lwf_out, bef_out, kdf_out, lwb_out, beb_out, kdb_out,
                      gate_out, bonus_out):
    u = z_ref[0]
    tr = u.shape[0]
    width = r_out.shape[-1]
    row = lax.broadcasted_iota(jnp.int32, (tr, 1), 0)
    u_prev = jnp.where(row == 0, zp_ref[0][7:8], pltpu.roll(u, 1, 0))
    u_next = jnp.where(row == tr - 1, zn_ref[0][0:1], pltpu.roll(u, tr - 1, 0))
    xm = u + mu_ref[...] * (0.5 * (u_prev + u_next) - u)
    valid = valid_ref[...] > 0.0
    bd = bd_ref[...]
    r = xm[:, 0:width]
    k = xm[:, width:2 * width]
    v = jnp.where(valid, xm[:, 2 * width:3 * width], 0.0)
    lr = 3 * width
    wl = jnp.tanh(xm[:, lr:lr + V7X_LANES])
    al = xm[:, lr + V7X_LANES:lr + 2 * V7X_LANES]
    gl = xm[:, lr + 2 * V7X_LANES:lr + 3 * V7X_LANES]
    kk = k * kk_ref[...]
    kk = kk / jnp.maximum(jnp.sqrt(_group_sum(kk * kk, bd)), 1e-12)
    kk = jnp.where(valid, kk, 0.0)
    r_out[0] = r
    v_out[0] = v
    nkk_out[0] = -kk
    kd_sum = jnp.zeros_like(k)
    for d, (lw_out, be_out, kd_out) in enumerate(((lwf_out, bef_out, kdf_out),
                                                   (lwb_out, beb_out, kdb_out))):
        wlog = _log_sigmoid(w0_ref[d:d + 1, :] + _dot(wl, w2_ref[d], P_RW_LR)) - 0.5
        lw_out[0] = jnp.where(valid, -jnp.exp(wlog), 0.0)
        a = _sigmoid(a0_ref[d:d + 1, :] + _dot(al, a2_ref[d], P_RW_LR))
        kd = k * (1.0 + (a - 1.0) * ka_ref[...])
        kd_sum = kd_sum + kd
        kd_out[0] = jnp.where(valid, kd, 0.0)
        be_out[0] = kk * a
    gate_out[0] = _dot(_sigmoid(gl), g2_ref[...], P_RW_LR)
    bonus_out[0] = _group_sum(r * kd_sum * rk_ref[...], bd) * v


def rwkv_prep(z_rw, valid2d, mu, w0, w2pad, a0, a2pad, g2, k_k, k_a, r_k, bd_ones, tr):
    bt, p, cols = z_rw.shape
    width = w0.shape[1]
    nt = p // tr
    r8 = tr // 8
    last8 = p // 8 - 1
    full = lambda a: pl.BlockSpec(a.shape, lambda b, i: (0,) * a.ndim)
    out = jax.ShapeDtypeStruct((bt, p, width), F32)
    ospec = pl.BlockSpec((1, tr, width), lambda b, i: (b, i, 0))
    params = (mu, w0, w2pad, a0, a2pad, g2, k_k, k_a, r_k, bd_ones)
    return pl.pallas_call(
        _rwkv_prep_kernel,
        grid=(bt, nt),
        in_specs=[pl.BlockSpec((1, tr, cols), lambda b, i: (b, i, 0)),
                  pl.BlockSpec((1, 8, cols), lambda b, i: (b, jnp.maximum(i * r8 - 1, 0), 0)),
                  pl.BlockSpec((1, 8, cols), lambda b, i: (b, jnp.minimum((i + 1) * r8, last8), 0)),
                  pl.BlockSpec((tr, 1), lambda b, i: (i, 0))] + [full(a) for a in params],
        out_specs=[ospec] * 11,
        out_shape=[out] * 11,
        compiler_params=_params(("parallel", "parallel")),
        name="rwkv_prep",
    )(z_rw, z_rw, z_rw, valid2d, *params)


def _tri_inverse_all(l_list, eye):
    n = range(len(l_list))
    l_hi_lo = [_split_bf16(l) for l in l_list]
    pw = [hl[0] for hl in l_hi_lo]
    tinv = [eye + l for l in l_list]
    for _ in range(int(math.log2(RW_CHUNK)) - 1):
        pw = [_dot(pw[u], pw[u]).astype(BF16) for u in n]
        tinv = [tinv[u] + _dot(tinv[u].astype(BF16), pw[u]) for u in n]
    out = []
    for u in n:
        lh, ll = l_hi_lo[u]
        th, tl = _split_bf16(tinv[u])
        resid = (eye - tinv[u]) + (_dot(lh, th) + (_dot(ll, th) + _dot(lh, tl)))
        out.append(tinv[u] + _dot(th, resid.astype(BF16)))
    return out


def _rwkv_chunk_kernel(rf_ref, vf_ref, af_ref, lwf_ref, bef_ref, kdf_ref,
                       rb_ref, vb_ref, ab_ref, lwb_ref, beb_ref, kdb_ref,
                       of_ref, ob_ref, h_scr, *, nc):
    @pl.when(pl.program_id(1) == 0)
    def _():
        h_scr[...] = jnp.zeros_like(h_scr)

    c = RW_CHUNK
    n2 = 2 * c
    npair = h_scr.shape[1]
    t_idx = lax.broadcasted_iota(jnp.int32, (c, c), 0)
    s_idx = lax.broadcasted_iota(jnp.int32, (c, c), 1)
    ti = lax.broadcasted_iota(jnp.int32, (n2, n2), 0)
    si = lax.broadcasted_iota(jnp.int32, (n2, n2), 1)
    same = (ti // c) == (si // c)
    tl, sl = ti % c, si % c
    eye = jnp.where(ti == si, 1.0, 0.0)
    first = lax.broadcasted_iota(jnp.int32, (c, V7X_LANES), 1) < RW_HEAD_DIM

    def stack(x):
        return jnp.concatenate([jnp.where(first, x, 0.0), jnp.where(first, 0.0, x)], axis=0)

    dirs = ((rf_ref, vf_ref, af_ref, lwf_ref, bef_ref, kdf_ref, of_ref),
            (rb_ref, vb_ref, ab_ref, lwb_ref, beb_ref, kdb_ref, ob_ref))
    units = []
    for d, (r_ref, v_ref, a_ref, lw_ref, be_ref, kd_ref, _) in enumerate(dirs):
        reverse = d == 1
        tri = ((s_idx >= t_idx) if reverse else (s_idx <= t_idx)).astype(F32)
        strict = same & ((sl > tl) if reverse else (sl < tl))
        incl = same & ((sl >= tl) if reverse else (sl <= tl))
        for j in range(nc):
            rows = slice(j * c, (j + 1) * c)
            lw = lw_ref[0, rows, :]
            lg = _dot(tri, lw, HI)
            ltot = lg[0:1] if reverse else lg[c - 1:c]
            ginv = jnp.exp(-lg)
            gend = jnp.exp(ltot - lg)
            be, kd = be_ref[0, rows, :], kd_ref[0, rows, :]
            cols = (a_ref[0, rows, :] * jnp.exp(lg - lw), r_ref[0, rows, :] * jnp.exp(lg),
                    be * ginv, kd * ginv, be * gend, kd * gend, v_ref[0, rows, :])
            gtot = jnp.exp(ltot)
            for p in range(npair):
                ps = slice(p * V7X_LANES, (p + 1) * V7X_LANES)
                a_s, r_s, b_s, k_s, bh_s, kh_s, v_s = (stack(x[:, ps]).astype(BF16) for x in cols)
                units.append(dict(d=d, j=j, p=p, rows=rows, ps=ps, strict=strict, incl=incl,
                                  ar=jnp.concatenate([a_s, r_s], axis=0),
                                  bk=jnp.concatenate([b_s, k_s], axis=0),
                                  bh=bh_s, kh=kh_s, v=v_s, gtot=gtot[:, ps]))
    for u in units:
        sc = _dot_nt(u["ar"], u["bk"])
        u["l_ab"] = jnp.where(u["strict"], sc[:n2, :n2], 0.0)
        u["l_ak"] = jnp.where(u["strict"], sc[:n2, n2:], 0.0).astype(BF16)
        u["m_rb"] = jnp.where(u["incl"], sc[n2:, :n2], 0.0).astype(BF16)
        u["m_rk"] = jnp.where(u["incl"], sc[n2:, n2:], 0.0).astype(BF16)
    for u, tinv in zip(units, _tri_inverse_all([u["l_ab"] for u in units], eye)):
        u["tinv"] = tinv.astype(BF16)
    for u in units:
        u["lakv"] = _dot(u["l_ak"], u["v"])
        u["mv"] = _dot(u["m_rk"], u["v"])
        u["kv"] = _dot_tn(u["v"], u["kh"])

    state = {(d, p): h_scr[d, p] for d in range(2) for p in range(npair)}
    for step in range(nc):
        live = [u for u in units if u["j"] == (nc - 1 - step if u["d"] == 1 else step)]
        for u in live:
            u["ah"] = _dot_nt(u["ar"], state[u["d"], u["p"]].astype(BF16))
        for u in live:
            u["u"] = _dot(u["tinv"], (u["ah"][:n2] + u["lakv"]).astype(BF16)).astype(BF16)
        for u in live:
            o_s = u["ah"][n2:] + u["mv"] + _dot(u["m_rb"], u["u"])
            dirs[u["d"]][6][0, u["rows"], u["ps"]] = o_s[:c] + o_s[c:]
            key = (u["d"], u["p"])
            state[key] = state[key] * u["gtot"] + u["kv"] + _dot_tn(u["u"], u["bh"])
    for (d, p), h in state.items():
        h_scr[d, p] = h


def rwkv_scan(r, v, nkk, lw_f, be_f, kd_f, lw_b, be_b, kd_b, nc):
    bt, p, width = r.shape
    rows = nc * RW_CHUNK
    n = p // rows
    blk = (1, rows, width)
    fwd = pl.BlockSpec(blk, lambda b, c: (b, c, 0))
    bwd = pl.BlockSpec(blk, lambda b, c: (b, n - 1 - c, 0))
    out = jax.ShapeDtypeStruct((bt, p, width), F32)
    return pl.pallas_call(
        functools.partial(_rwkv_chunk_kernel, nc=nc),
        grid=(bt, n),
        in_specs=[fwd] * 6 + [bwd] * 6,
        out_specs=[fwd, bwd],
        out_shape=[out, out],
        scratch_shapes=[pltpu.VMEM((2, width // V7X_LANES, V7X_LANES, V7X_LANES), F32)],
        compiler_params=_params(("parallel", "arbitrary")),
        name="rwkv_scan",
    )(r, v, nkk, lw_f, be_f, kd_f, r, v, nkk, lw_b, be_b, kd_b)


def _rwkv_post_kernel(of_ref, ob_ref, bonus_ref, gate_ref, g_ref, b_ref, bd_ref, o_ref):
    o = of_ref[0] + ob_ref[0]
    bd = bd_ref[...]
    inv = 1.0 / RW_HEAD_DIM
    mean = _group_sum(o, bd) * inv
    cen = o - mean
    var = _group_sum(cen * cen, bd) * inv
    y = cen * lax.rsqrt(var + RW_LNX_EPS) * g_ref[...] + b_ref[...]
    o_ref[0] = (y + bonus_ref[0]) * gate_ref[0]


def rwkv_post(o_f, o_b, bonus, gate, lnx_g, lnx_b, bd_ones, tr):
    bt, p, width = o_f.shape
    spec = pl.BlockSpec((1, tr, width), lambda b, i: (b, i, 0))
    full = lambda a: pl.BlockSpec(a.shape, lambda b, i: (0,) * a.ndim)
    return pl.pallas_call(
        _rwkv_post_kernel,
        grid=(bt, p // tr),
        in_specs=[spec, spec, spec, spec, full(lnx_g), full(lnx_b), full(bd_ones)],
        out_specs=spec,
        out_shape=jax.ShapeDtypeStruct((bt, p, width), F32),
        compiler_params=_params(("parallel", "parallel")),
        name="rwkv_post",
    )(o_f, o_b, bonus, gate, lnx_g, lnx_b, bd_ones)


def _merge_kernel(x_ref, g_ref, valid_ref, y0_ref, y1_ref, y2_ref, y3_ref, wg_ref, bp_ref, wo_ref,
                  o_ref):
    x = x_ref[...]
    d = x.shape[1]
    h = _masked_rms_norm(x, g_ref[...], valid_ref[...]).astype(BF16)
    merged = jnp.zeros_like(x)
    for n, y_ref in enumerate((y0_ref, y1_ref, y2_ref, y3_ref)):
        gate = _sigmoid(_dot(h, wg_ref[:, n * d:(n + 1) * d]))
        merged = merged + gate * _dot(y_ref[...].astype(BF16), bp_ref[n])
    o_ref[...] = x + _dot(merged.astype(BF16), wo_ref[...])


def merge(x2d, g, valid, ys, w_gate, branch_proj, w_out, tm):
    m, d = x2d.shape
    bw = ys[0].shape[1]
    row = lambda width: pl.BlockSpec((tm, width), lambda i: (i, 0))
    full = lambda a: pl.BlockSpec(a.shape, lambda i: (0,) * a.ndim)
    return pl.pallas_call(
        _merge_kernel,
        grid=(m // tm,),
        in_specs=[row(d), full(g), row(1)] + [row(bw)] * N_BRANCH
                 + [full(w_gate), full(branch_proj), full(w_out)],
        out_specs=row(d),
        out_shape=jax.ShapeDtypeStruct((m, d), F32),
        compiler_params=_params(("parallel",)),
        name="merge",
    )(x2d, g, valid, *ys, w_gate, branch_proj, w_out)


def _mlp_kernel(x_ref, g_ref, w1_ref, w2_ref, o_ref, h_scr, acc_scr):
    f = pl.program_id(1)

    @pl.when(f == 0)
    def _():
        x = x_ref[...]
        ms = jnp.mean(x * x, axis=-1, keepdims=True)
        h_scr[...] = (x * lax.rsqrt(ms + NORM_EPS) * g_ref[...]).astype(BF16)
        acc_scr[...] = jnp.zeros_like(acc_scr)

    a = jnp.maximum(_dot(h_scr[...], w1_ref[...]), 0.0)
    acc_scr[...] += _dot((a * a).astype(BF16), w2_ref[...])

    @pl.when(f == pl.num_programs(1) - 1)
    def _():
        o_ref[...] = x_ref[...] + acc_scr[...]


def mlp(x2d, g, w1, w2, tm, tf):
    m, d = x2d.shape
    dff = w1.shape[1]
    return pl.pallas_call(
        _mlp_kernel,
        grid=(m // tm, dff // tf),
        in_specs=[pl.BlockSpec((tm, d), lambda i, f: (i, 0)),
                  pl.BlockSpec((1, d), lambda i, f: (0, 0)),
                  pl.BlockSpec((d, tf), lambda i, f: (0, f)),
                  pl.BlockSpec((tf, d), lambda i, f: (f, 0))],
        out_specs=pl.BlockSpec((tm, d), lambda i, f: (i, 0)),
        out_shape=jax.ShapeDtypeStruct((m, d), F32),
        scratch_shapes=[pltpu.VMEM((tm, d), BF16), pltpu.VMEM((tm, d), F32)],
        compiler_params=_params(("parallel", "arbitrary")),
        name="mlp",
    )(x2d, g, w1, w2)


def _rope_tables(p):
    half = ROPE_DIM // 2
    pos = jnp.arange(p, dtype=F32) - FRONT_PAD
    inv = ROPE_THETA ** (-jnp.arange(half, dtype=F32) / half)
    ang = pos[:, None] * inv[None, :]
    cos, sin = jnp.cos(ang), jnp.sin(ang)
    ones = jnp.ones((p, DA_QK_DIM - ROPE_DIM), F32)
    zeros = jnp.zeros((p, DA_QK_DIM - ROPE_DIM), F32)
    zh = jnp.zeros((p, half), F32)
    c = jnp.concatenate([cos, cos, ones], axis=1)
    s1 = jnp.concatenate([-sin, zh, zeros], axis=1)
    s2 = jnp.concatenate([zh, sin, zeros], axis=1)
    return tuple(jnp.tile(t, (1, 2)) for t in (c, s1, s2))


def _block_diag_ones(n, blk):
    i = jnp.arange(n)
    return ((i[:, None] // blk) == (i[None, :] // blk)).astype(F32)


def kernel(x_prompt, x_sample, meta_tokens, norm_mix_g, w_in, hgrn_lb_logits, hgrn_onorm_g, conv_w,
           diff_qnorm_g, diff_knorm_g, diff_lambda, diff_subln_g, rwkv_mu, rwkv_w0, rwkv_w2, rwkv_a0,
           rwkv_a2, rwkv_g2, rwkv_k_k, rwkv_k_a, rwkv_r_k, rwkv_lnx_g, rwkv_lnx_b, w_gate,
           branch_proj, w_out, norm_mlp_g, mlp_w1, mlp_w2):
    assert x_prompt.shape[1:] == x_sample.shape[1:]
    x = jnp.concatenate([x_prompt, x_sample], axis=0)
    bt, seq, d = x.shape
    depth = w_in.shape[0]
    length = N_META + seq
    p = -(-(FRONT_PAD + length) // V7X_LANES) * V7X_LANES
    m = bt * p
    bw = branch_proj.shape[2]
    rw_cols = rwkv_mu.shape[1]
    sizes = (5 * bw, 3 * bw, 3 * bw, rw_cols)
    offs = [0]
    for s in sizes:
        offs.append(offs[-1] + s)

    meta = jnp.broadcast_to(meta_tokens.astype(x.dtype)[None], (bt, N_META, d))
    xp = jnp.concatenate([jnp.zeros((bt, FRONT_PAD, d), x.dtype), meta, x,
                          jnp.zeros((bt, p - FRONT_PAD - length, d), x.dtype)], axis=1)
    x2d = xp.reshape(m, d)

    rows = jnp.arange(p)
    valid_seq = ((rows >= FRONT_PAD) & (rows < FRONT_PAD + length)).astype(F32)[:, None]
    valid = jnp.tile(valid_seq, (bt, 1))
    key_bias = jnp.where(valid_seq[:, 0] > 0, 0.0, NEG_BIG).astype(F32)[None, :]
    tables = _rope_tables(p)
    bd64_mean = _block_diag_ones(V7X_LANES, DA_QK_DIM) / DA_QK_DIM
    bd64_ones = _block_diag_ones(V7X_LANES, RW_HEAD_DIM)

    sm = jax.nn.softmax(hgrn_lb_logits.astype(F32), axis=1)
    lb_all = jnp.cumsum(sm, axis=1) - sm[:, :1]

    tm_proj = _pick(m, 1536, 128)
    tm_merge = _pick(m, 256, 128)
    tm_mlp = _pick(m, 768, 128)
    tq = _pick(p, 128, 128)
    tr = _pick(p, 384, 8)

    half = V7X_LANES // 2
    for l in range(depth):
        g_mix = norm_mix_g[l][None, :]
        w_l = w_in[l].astype(BF16)
        z = []
        for gi, size in enumerate(sizes):
            tn = _pick(size, 640, V7X_LANES)
            z.append(norm_matmul(x2d, g_mix, valid, w_l[:, offs[gi]:offs[gi + 1]], tm_proj, tn)
                     .reshape(bt, p, size))
        z_hg, z_sc, z_da, z_rw = z

        of, ob = hgrn_scan(z_hg, valid_seq, lb_all[:, l])
        y_hg = hgrn_post(of, ob, z_hg, hgrn_onorm_g[l][None, :], tr)

        y_sc = shortconv(z_sc, conv_w[l])

        lam_init = 0.8 - 0.6 * math.exp(-0.3 * l)
        y_da = diff_attention(z_da, tables, bd64_mean,
                              jnp.tile(diff_qnorm_g[l], 2)[None, :], jnp.tile(diff_knorm_g[l], 2)[None, :],
                              diff_lambda[l], diff_subln_g[l][None, :], key_bias, lam_init, tq)

        zpad = jnp.zeros((2, half, bw), F32)
        w2pad = jnp.stack([jnp.concatenate([rwkv_w2[l, 0], zpad[0]], 0),
                           jnp.concatenate([zpad[0], rwkv_w2[l, 1]], 0)])
        a2pad = jnp.stack([jnp.concatenate([rwkv_a2[l, 0], zpad[0]], 0),
                           jnp.concatenate([zpad[0], rwkv_a2[l, 1]], 0)])
        (r, v, nkk, lwf, bef, kdf, lwb, beb, kdb, gate, bonus) = rwkv_prep(
            z_rw, valid_seq, rwkv_mu[l][None, :], rwkv_w0[l], w2pad, rwkv_a0[l], a2pad, rwkv_g2[l],
            rwkv_k_k[l][None, :], rwkv_k_a[l][None, :], rwkv_r_k[l].reshape(1, bw), bd64_ones, tr)
        orf, orb = rwkv_scan(r, v, nkk, lwf, bef, kdf, lwb, beb, kdb,
                             _pick(p // RW_CHUNK, RW_CHUNKS_PER_STEP, 1))
        y_rw = rwkv_post(orf, orb, bonus, gate, rwkv_lnx_g[l][None, :], rwkv_lnx_b[l][None, :],
                         bd64_ones, tr)

        ys = [y.reshape(m, bw) for y in (y_hg, y_sc, y_da, y_rw)]
        x2d = merge(x2d, g_mix, valid, ys, w_gate[l].astype(BF16), branch_proj[l].astype(BF16),
                    w_out[l].astype(BF16), tm_merge)
        x2d = mlp(x2d, norm_mlp_g[l][None, :], mlp_w1[l].astype(BF16), mlp_w2[l].astype(BF16),
                  tm_mlp, _pick(mlp_w1.shape[2], 512, V7X_LANES))

    y = x2d.reshape(bt, p, d)[:, FRONT_PAD + N_META:FRONT_PAD + length]
    nb = x_prompt.shape[0]
    return (y[:nb], y[nb:])
```

```python
import functools
import math

import jax
import jax.numpy as jnp
from jax import lax
from jax.experimental import pallas as pl
from jax.experimental.pallas import tpu as pltpu

F32 = jnp.float32
BF16 = jnp.bfloat16
HI = lax.Precision.HIGHEST

V7X_LANES = 128
V7X_VMEM_BYTES = 64 * 1024 * 1024
VMEM_LIMIT = V7X_VMEM_BYTES - 8 * 1024 * 1024

N_META = 16
NORM_EPS = 1e-6
N_BRANCH = 4
HG_HEAD_DIM = 128
HG_CHUNK = 64
HG_SUB = 16
LB_FLOOR = 1e-20
SC_KSIZE = 3
DA_HEADS = 4
DA_QK_DIM = 64
DA_V_DIM = 128
ROPE_THETA = 500000.0
ROPE_DIM = DA_QK_DIM // 4
SUBLN_EPS = 1e-5
RW_HEAD_DIM = 64
RW_CHUNK = 64
RW_LNX_EPS = 64e-5
FRONT_PAD = (-N_META) % HG_CHUNK
NEG_BIG = -1e30
LOG2_E = 1.4426950408889634

P_HG = "bf16"
P_RW_LR = "bf16"
RW_CHUNKS_PER_STEP = 2
HG_CHUNKS_PER_STEP = 2


def _split_bf16(a):
    hi = a.astype(BF16)
    return hi, (a - hi.astype(F32)).astype(BF16)


def _mm(a, b, dims, precision):
    dg = functools.partial(lax.dot_general, dimension_numbers=(dims, ((), ())),
                           preferred_element_type=F32)
    if precision == "bf16":
        return dg(a.astype(BF16), b.astype(BF16))
    if precision == "x3":
        ah, al = _split_bf16(a)
        bh, bl = _split_bf16(b)
        return dg(ah, bh) + (dg(al, bh) + dg(ah, bl))
    return dg(a, b, precision=precision)


def _dot(a, b, precision=None):
    return _mm(a, b, ((1,), (0,)), precision)


def _dot_split_lhs(a, b_bf16):
    hi, lo = _split_bf16(a)
    return _dot(hi, b_bf16) + _dot(lo, b_bf16)


def _dot_nt(a, b, precision=None):
    return _mm(a, b, ((1,), (1,)), precision)


def _dot_tn(a, b, precision=None):
    return _mm(a, b, ((0,), (0,)), precision)


def _pick(n, target, mult):
    best = None
    for d in range(mult, min(n, target) + 1, mult):
        if n % d == 0:
            best = d
    assert best is not None, (n, target, mult)
    return best


def _params(sem):
    return pltpu.CompilerParams(dimension_semantics=sem, vmem_limit_bytes=VMEM_LIMIT)


def _log_sigmoid(x):
    return jnp.minimum(x, 0.0) - jnp.log1p(jnp.exp(-jnp.abs(x)))


def _sigmoid(x):
    return 1.0 / (1.0 + jnp.exp(-x))


def _masked_rms_norm(x, g, valid):
    ms = jnp.mean(x * x, axis=-1, keepdims=True)
    h = x * lax.rsqrt(ms + NORM_EPS) * g
    return jnp.where(valid > 0.0, h, 0.0)


def _norm_matmul_kernel(x_ref, g_ref, valid_ref, w_ref, o_ref, h_scr):
    @pl.when(pl.program_id(1) == 0)
    def _():
        h_scr[...] = _masked_rms_norm(x_ref[...], g_ref[...], valid_ref[...]).astype(BF16)

    o_ref[...] = _dot(h_scr[...], w_ref[...])


def norm_matmul(x2d, g, valid, w_bf16, tm, tn):
    m, d = x2d.shape
    n = w_bf16.shape[1]
    return pl.pallas_call(
        _norm_matmul_kernel,
        grid=(m // tm, n // tn),
        in_specs=[
            pl.BlockSpec((tm, d), lambda i, j: (i, 0)),
            pl.BlockSpec((1, d), lambda i, j: (0, 0)),
            pl.BlockSpec((tm, 1), lambda i, j: (i, 0)),
            pl.BlockSpec((d, tn), lambda i, j: (0, j)),
        ],
        out_specs=pl.BlockSpec((tm, tn), lambda i, j: (i, j)),
        out_shape=jax.ShapeDtypeStruct((m, n), F32),
        scratch_shapes=[pltpu.VMEM((tm, d), BF16)],
        compiler_params=_params(("parallel", "arbitrary")),
        name="norm_matmul",
    )(x2d, g, valid, w_bf16)


def _shortconv_kernel(b_ref, c_ref, h_ref, w_ref, o_ref):
    u = c_ref[0] * h_ref[0]
    p = u.shape[0]
    w = w_ref[...]
    y = pltpu.roll(u, 1, 0) * w[0:1] + u * w[1:2] + pltpu.roll(u, p - 1, 0) * w[2:3]
    o_ref[0] = b_ref[0] * y


def shortconv(z_sc, conv_w):
    bt, p, w3 = z_sc.shape
    width = w3 // 3
    nb = width // V7X_LANES
    blk = (1, p, V7X_LANES)
    return pl.pallas_call(
        _shortconv_kernel,
        grid=(bt, nb),
        in_specs=[
            pl.BlockSpec(blk, lambda b, c: (b, 0, c)),
            pl.BlockSpec(blk, lambda b, c: (b, 0, nb + c)),
            pl.BlockSpec(blk, lambda b, c: (b, 0, 2 * nb + c)),
            pl.BlockSpec((SC_KSIZE, V7X_LANES), lambda b, c: (0, c)),
        ],
        out_specs=pl.BlockSpec(blk, lambda b, c: (b, 0, c)),
        out_shape=jax.ShapeDtypeStruct((bt, p, width), F32),
        compiler_params=_params(("parallel", "parallel")),
        name="shortconv",
    )(z_sc, z_sc, z_sc, conv_w)


def _attn_kernel(q_ref, k_ref, v_ref, cq_ref, s1q_ref, s2q_ref, ck_ref, s1k_ref, s2k_ref,
                 bd_ref, qg_ref, kg_ref, lam_ref, sg_ref, bias_ref, o_ref, k_scr, v_scr, s_scr,
                 *, lam_init, tk):
    bd = bd_ref[...]

    def norm_rope(x, g, c, s1, s2):
        ms = _dot_split_lhs(x * x, bd)
        y = x * lax.rsqrt(ms + NORM_EPS) * g
        return (y * c + pltpu.roll(y, V7X_LANES - ROPE_DIM // 2, 1) * s1
                + pltpu.roll(y, ROPE_DIM // 2, 1) * s2)

    @pl.when(pl.program_id(2) == 0)
    def _():
        k = norm_rope(k_ref[0], kg_ref[...], ck_ref[...], s1k_ref[...], s2k_ref[...])
        k_scr[...] = k.astype(BF16)
        v_scr[...] = v_ref[0].astype(BF16)

    q = norm_rope(q_ref[0], qg_ref[...], cq_ref[...], s1q_ref[...], s2q_ref[...])
    q = q * (DA_QK_DIM ** -0.5 * LOG2_E)
    tq = q.shape[0]
    first = lax.broadcasted_iota(jnp.int32, q.shape, 1) < DA_QK_DIM
    lp = lam_ref[...]
    lam = (jnp.exp(jnp.sum(lp[0:1] * lp[1:2], axis=-1, keepdims=True))
           - jnp.exp(jnp.sum(lp[2:3] * lp[3:4], axis=-1, keepdims=True)) + lam_init)
    qs = jnp.concatenate([jnp.where(first, q, 0.0), jnp.where(first, 0.0, q)], axis=0).astype(BF16)
    nk = k_scr.shape[0] // tk
    lanes = range(0, tk, V7X_LANES)
    m128 = None
    for j in range(nk):
        cols = slice(j * tk, (j + 1) * tk)
        s_j = _dot_nt(qs, k_scr[cols, :])
        if j in (0, nk - 1):
            s_j = s_j + bias_ref[:, cols]
        s_scr[:, cols] = s_j
        for c in lanes:
            piece = s_j[:, c:c + V7X_LANES]
            m128 = piece if m128 is None else jnp.maximum(m128, piece)
    m = jnp.max(m128, axis=-1, keepdims=True)
    l128 = jnp.zeros_like(m128)
    o2 = jnp.zeros((2 * tq, DA_V_DIM), F32)
    for j in range(nk):
        cols = slice(j * tk, (j + 1) * tk)
        pr = jnp.exp2(s_scr[:, cols] - m)
        for c in lanes:
            l128 = l128 + pr[:, c:c + V7X_LANES]
        o2 = o2 + _dot(pr.astype(BF16), v_scr[cols, :])
    inv = 1.0 / jnp.sum(l128, axis=-1, keepdims=True)
    o = o2[:tq] * inv[:tq] - o2[tq:] * (lam * inv[tq:])
    ms = jnp.mean(o * o, axis=-1, keepdims=True)
    o_ref[0] = o * lax.rsqrt(ms + SUBLN_EPS) * sg_ref[...] * (1.0 - lam_init)


def diff_attention(z_da, tables, bd64, qn_g, kn_g, lam_p, subln_g, key_bias, lam_init, tq, tk):
    bt, p, _ = z_da.shape
    c_t, s1_t, s2_t = tables
    nh = DA_HEADS
    qspec = pl.BlockSpec((1, tq, V7X_LANES), lambda b, h, i: (b, i, h))
    kspec = pl.BlockSpec((1, p, V7X_LANES), lambda b, h, i: (b, 0, nh + h))
    vspec = pl.BlockSpec((1, p, V7X_LANES), lambda b, h, i: (b, 0, 2 * nh + h))
    tq_spec = pl.BlockSpec((tq, V7X_LANES), lambda b, h, i: (i, 0))
    tk_spec = pl.BlockSpec((p, V7X_LANES), lambda b, h, i: (0, 0))
    full = lambda shape: pl.BlockSpec(shape, lambda b, h, i: (0,) * len(shape))
    return pl.pallas_call(
        functools.partial(_attn_kernel, lam_init=lam_init, tk=tk),
        grid=(bt, nh, p // tq),
        in_specs=[qspec, kspec, vspec, tq_spec, tq_spec, tq_spec, tk_spec, tk_spec, tk_spec,
                  full((V7X_LANES, V7X_LANES)), full((1, V7X_LANES)), full((1, V7X_LANES)),
                  full((4, DA_QK_DIM)), full((1, V7X_LANES)), full((1, p))],
        out_specs=pl.BlockSpec((1, tq, V7X_LANES), lambda b, h, i: (b, i, h)),
        out_shape=jax.ShapeDtypeStruct((bt, p, nh * DA_V_DIM), F32),
        scratch_shapes=[pltpu.VMEM((p, V7X_LANES), BF16), pltpu.VMEM((p, V7X_LANES), BF16),
                        pltpu.VMEM((2 * tq, p), F32)],
        compiler_params=_params(("parallel", "parallel", "arbitrary")),
        name="diff_attention",
    )(z_da, z_da, z_da, c_t, s1_t, s2_t, c_t, s1_t, s2_t, bd64, qn_g, kn_g, lam_p, subln_g,
      key_bias)


def _gla_local(q, qt, k, v, b, bpiv, reverse):
    c, sub = HG_CHUNK, HG_SUB
    nsub = c // sub
    row = lax.broadcasted_iota(jnp.int32, (c, 1), 0)
    attn = jnp.zeros((c, c), F32)
    for i in range(nsub):
        if (reverse and i == nsub - 1) or (not reverse and i == 0):
            continue
        bi = bpiv[i * sub:i * sub + 1]
        kmask = (row >= (i + 1) * sub) if reverse else (row < i * sub)
        kt = jnp.where(kmask, k * jnp.exp(jnp.minimum(bi - b, 0.0)), 0.0)
        attn = jnp.where(row // sub == i, _dot_nt(qt, kt, P_HG), attn)

    b2 = b * LOG2_E
    tloc = lax.broadcasted_iota(jnp.int32, (sub, c), 0)
    scol = lax.broadcasted_iota(jnp.int32, (sub, c), 1)
    blocks = []
    for i in range(nsub):
        bt, qi = b2[i * sub:(i + 1) * sub], q[i * sub:(i + 1) * sub]
        blk = attn[i * sub:(i + 1) * sub]
        for j in range(sub):
            s = i * sub + j
            a = jnp.sum(qi * k[s:s + 1] * jnp.exp2(bt - b2[s:s + 1]), axis=-1, keepdims=True)
            keep = (scol == s) & ((tloc <= j) if reverse else (tloc >= j))
            blk = jnp.where(keep, a, blk)
        blocks.append(blk)
    return _dot(jnp.concatenate(blocks, axis=0), v, P_HG)


def _hgrn_kernel(qf_ref, ff_ref, vf_ref, mf_ref, qb_ref, fb_ref, vb_ref, mb_ref, lb_ref,
                 of_ref, ob_ref, st_scr, *, nc):
    @pl.when(pl.program_id(1) == 0)
    def _():
        st_scr[...] = jnp.zeros_like(st_scr)

    c, sub = HG_CHUNK, HG_SUB
    nh = st_scr.shape[1]
    t_idx = lax.broadcasted_iota(jnp.int32, (c, c), 0)
    s_idx = lax.broadcasted_iota(jnp.int32, (c, c), 1)
    dirs = ((qf_ref, ff_ref, vf_ref, mf_ref, of_ref), (qb_ref, fb_ref, vb_ref, mb_ref, ob_ref))
    units = []
    for d, (q_ref, f_ref, v_ref, m_ref, _) in enumerate(dirs):
        reverse = d == 1
        if reverse:
            tri, piv = s_idx >= t_idx, s_idx >= (t_idx // sub + 1) * sub
        else:
            tri, piv = s_idx <= t_idx, s_idx < (t_idx // sub) * sub
        cmat = jnp.concatenate([tri.astype(F32), piv.astype(F32)], axis=0)
        lb = lb_ref[d:d + 1, :]
        la = jnp.log(jnp.maximum(lb, LB_FLOOR))
        l1 = jnp.log1p(-lb)
        for j in range(nc):
            rows = slice(j * c, (j + 1) * c)
            valid = m_ref[rows, :] > 0.0
            lc = l1 + _log_sigmoid(f_ref[0, rows, :])
            logf = jnp.maximum(la, lc) + jnp.log1p(jnp.exp(-jnp.abs(la - lc)))
            logf = jnp.where(valid, logf, 0.0)
            k = jnp.where(valid, 1.0 - jnp.exp(logf), 0.0)
            cum = _dot(cmat, logf, HI)
            b, bpiv = cum[:c], cum[c:]
            btot = b[0:1] if reverse else b[c - 1:c]
            q, v = q_ref[0, rows, :], v_ref[0, rows, :]
            qt = q * jnp.exp(b - bpiv)
            qe = (q * jnp.exp(b)).astype(BF16)
            k2 = k * jnp.exp(btot - b)
            dec = jnp.exp(btot)
            for h in range(nh):
                hs = slice(h * HG_HEAD_DIM, (h + 1) * HG_HEAD_DIM)
                units.append(dict(
                    d=d, j=j, h=h, rows=rows, hs=hs, qe=qe[:, hs], dec=dec[:, hs],
                    o=_gla_local(q[:, hs], qt[:, hs], k[:, hs], v[:, hs], b[:, hs], bpiv[:, hs],
                                 reverse),
                    kv=_dot_tn(v[:, hs], k2[:, hs], P_HG)))

    state = {(d, h): st_scr[d, h] for d in range(2) for h in range(nh)}
    for step in range(nc):
        for u in units:
            if u["j"] != (nc - 1 - step if u["d"] == 1 else step):
                continue
            key = (u["d"], u["h"])
            dirs[u["d"]][4][0, u["rows"], u["hs"]] = u["o"] + _dot_nt(u["qe"], state[key].astype(BF16))
            state[key] = state[key] * u["dec"] + u["kv"]
    for (d, h), s in state.items():
        st_scr[d, h] = s


def hgrn_scan(z_hg, valid_seq, lb, nc):
    bt, p, w5 = z_hg.shape
    width = w5 // 5
    nh = width // HG_HEAD_DIM
    rows = nc * HG_CHUNK
    n = p // rows
    blk = (1, rows, width)

    def spec(group, rev):
        if rev:
            return pl.BlockSpec(blk, lambda b, i: (b, n - 1 - i, group))
        return pl.BlockSpec(blk, lambda b, i: (b, i, group))

    mf = pl.BlockSpec((rows, 1), lambda b, i: (i, 0))
    mb = pl.BlockSpec((rows, 1), lambda b, i: (n - 1 - i, 0))
    out = jax.ShapeDtypeStruct((bt, p, width), F32)
    return pl.pallas_call(
        functools.partial(_hgrn_kernel, nc=nc),
        grid=(bt, n),
        in_specs=[spec(0, False), spec(1, False), spec(3, False), mf,
                  spec(0, True), spec(2, True), spec(3, True), mb,
                  pl.BlockSpec((2, width), lambda b, i: (0, 0))],
        out_specs=[pl.BlockSpec(blk, lambda b, i: (b, i, 0)),
                   pl.BlockSpec(blk, lambda b, i: (b, n - 1 - i, 0))],
        out_shape=[out, out],
        scratch_shapes=[pltpu.VMEM((2, nh, HG_HEAD_DIM, HG_HEAD_DIM), F32)],
        compiler_params=_params(("parallel", "arbitrary")),
        name="hgrn_scan",
    )(z_hg, z_hg, z_hg, valid_seq, z_hg, z_hg, z_hg, valid_seq, lb)


def _hgrn_post_kernel(of_ref, ob_ref, g_ref, gain_ref, o_ref):
    o = of_ref[0] + ob_ref[0]
    ms = jnp.mean(o * o, axis=-1, keepdims=True)
    g = g_ref[0]
    o_ref[0] = o * lax.rsqrt(ms + NORM_EPS) * gain_ref[...] * (g * _sigmoid(g))


def hgrn_post(o_f, o_b, z_hg, onorm_g, tr):
    bt, p, width = o_f.shape
    nh = width // HG_HEAD_DIM
    blk = (1, tr, HG_HEAD_DIM)
    spec = pl.BlockSpec(blk, lambda b, i, h: (b, i, h))
    return pl.pallas_call(
        _hgrn_post_kernel,
        grid=(bt, p // tr, nh),
        in_specs=[spec, spec, pl.BlockSpec(blk, lambda b, i, h: (b, i, 4 * nh + h)),
                  pl.BlockSpec((1, HG_HEAD_DIM), lambda b, i, h: (0, h))],
        out_specs=spec,
        out_shape=jax.ShapeDtypeStruct((bt, p, width), F32),
        compiler_params=_params(("parallel", "parallel", "parallel")),
        name="hgrn_post",
    )(o_f, o_b, z_hg, onorm_g)


def _group_sum(x, bd):
    cols = [_dot_split_lhs(x[:, j:j + V7X_LANES], bd) for j in range(0, x.shape[1], V7X_LANES)]
    return jnp.concatenate(cols, axis=1)


def _rwkv_prep_kernel(z_ref, zp_ref, zn_ref, valid_ref, mu_ref, w0_ref, w2_ref, a0_ref, a2_ref,
                      g2_ref, kk_ref, ka_ref, rk_ref, bd_ref,
                      r_out, v_out, nkk_out, lwf_out, bef_out, kdf_out, lwb_out, beb_out, kdb_out,
                      gate_out, bonus_out):
    u = z_ref[0]
    tr = u.shape[0]
    width = r_out.shape[-1]
    row = lax.broadcasted_iota(jnp.int32, (tr, 1), 0)
    u_prev = jnp.where(row == 0, zp_ref[0][7:8], pltpu.roll(u, 1, 0))
    u_next = jnp.where(row == tr - 1, zn_ref[0][0:1], pltpu.roll(u, tr - 1, 0))
    xm = u + mu_ref[...] * (0.5 * (u_prev + u_next) - u)
    valid = valid_ref[...] > 0.0
    bd = bd_ref[...]
    r = xm[:, 0:width]
    k = xm[:, width:2 * width]
    v = jnp.where(valid, xm[:, 2 * width:3 * width], 0.0)
    lr = 3 * width
    wl = jnp.tanh(xm[:, lr:lr + V7X_LANES])
    al = xm[:, lr + V7X_LANES:lr + 2 * V7X_LANES]
    gl = xm[:, lr + 2 * V7X_LANES:lr + 3 * V7X_LANES]
    kk = k * kk_ref[...]
    kk = kk / jnp.maximum(jnp.sqrt(_group_sum(kk * kk, bd)), 1e-12)
    kk = jnp.where(valid, kk, 0.0)
    r_out[0] = r
    v_out[0] = v
    nkk_out[0] = -kk
    kd_sum = jnp.zeros_like(k)
    for d, (lw_out, be_out, kd_out) in enumerate(((lwf_out, bef_out, kdf_out),
                                                   (lwb_out, beb_out, kdb_out))):
        wlog = _log_sigmoid(w0_ref[d:d + 1, :] + _dot(wl, w2_ref[d], P_RW_LR)) - 0.5
        lw_out[0] = jnp.where(valid, -jnp.exp(wlog), 0.0)
        a = _sigmoid(a0_ref[d:d + 1, :] + _dot(al, a2_ref[d], P_RW_LR))
        kd = k * (1.0 + (a - 1.0) * ka_ref[...])
        kd_sum = kd_sum + kd
        kd_out[0] = jnp.where(valid, kd, 0.0)
        be_out[0] = kk * a
    gate_out[0] = _dot(_sigmoid(gl), g2_ref[...], P_RW_LR)
    bonus_out[0] = _group_sum(r * kd_sum * rk_ref[...], bd) * v


def rwkv_prep(z_rw, valid2d, mu, w0, w2pad, a0, a2pad, g2, k_k, k_a, r_k, bd_ones, tr):
    bt, p, cols = z_rw.shape
    width = w0.shape[1]
    nt = p // tr
    r8 = tr // 8
    last8 = p // 8 - 1
    full = lambda a: pl.BlockSpec(a.shape, lambda b, i: (0,) * a.ndim)
    out = jax.ShapeDtypeStruct((bt, p, width), F32)
    ospec = pl.BlockSpec((1, tr, width), lambda b, i: (b, i, 0))
    params = (mu, w0, w2pad, a0, a2pad, g2, k_k, k_a, r_k, bd_ones)
    return pl.pallas_call(
        _rwkv_prep_kernel,
        grid=(bt, nt),
        in_specs=[pl.BlockSpec((1, tr, cols), lambda b, i: (b, i, 0)),
                  pl.BlockSpec((1, 8, cols), lambda b, i: (b, jnp.maximum(i * r8 - 1, 0), 0)),
                  pl.BlockSpec((1, 8, cols), lambda b, i: (b, jnp.minimum((i + 1) * r8, last8), 0)),
                  pl.BlockSpec((tr, 1), lambda b, i: (i, 0))] + [full(a) for a in params],
        out_specs=[ospec] * 11,
        out_shape=[out] * 11,
        compiler_params=_params(("parallel", "parallel")),
        name="rwkv_prep",
    )(z_rw, z_rw, z_rw, valid2d, *params)


def _tri_inverse_all(l_list, eye):
    n = range(len(l_list))
    l_hi_lo = [_split_bf16(l) for l in l_list]
    pw = [hl[0] for hl in l_hi_lo]
    tinv = [eye + l for l in l_list]
    for _ in range(int(math.log2(RW_CHUNK)) - 1):
        pw = [_dot(pw[u], pw[u]).astype(BF16) for u in n]
        tinv = [tinv[u] + _dot(tinv[u].astype(BF16), pw[u]) for u in n]
    out = []
    for u in n:
        lh, ll = l_hi_lo[u]
        th, tl = _split_bf16(tinv[u])
        resid = (eye - tinv[u]) + (_dot(lh, th) + (_dot(ll, th) + _dot(lh, tl)))
        out.append(tinv[u] + _dot(th, resid.astype(BF16)))
    return out


def _rwkv_chunk_kernel(rf_ref, vf_ref, af_ref, lwf_ref, bef_ref, kdf_ref,
                       rb_ref, vb_ref, ab_ref, lwb_ref, beb_ref, kdb_ref,
                       of_ref, ob_ref, h_scr, *, nc):
    @pl.when(pl.program_id(1) == 0)
    def _():
        h_scr[...] = jnp.zeros_like(h_scr)

    c = RW_CHUNK
    n2 = 2 * c
    npair = h_scr.shape[1]
    t_idx = lax.broadcasted_iota(jnp.int32, (c, c), 0)
    s_idx = lax.broadcasted_iota(jnp.int32, (c, c), 1)
    ti = lax.broadcasted_iota(jnp.int32, (n2, n2), 0)
    si = lax.broadcasted_iota(jnp.int32, (n2, n2), 1)
    same = (ti // c) == (si // c)
    tl, sl = ti % c, si % c
    eye = jnp.where(ti == si, 1.0, 0.0)
    first = lax.broadcasted_iota(jnp.int32, (c, V7X_LANES), 1) < RW_HEAD_DIM

    def stack(x):
        return jnp.concatenate([jnp.where(first, x, 0.0), jnp.where(first, 0.0, x)], axis=0)

    dirs = ((rf_ref, vf_ref, af_ref, lwf_ref, bef_ref, kdf_ref, of_ref),
            (rb_ref, vb_ref, ab_ref, lwb_ref, beb_ref, kdb_ref, ob_ref))
    units = []
    for d, (r_ref, v_ref, a_ref, lw_ref, be_ref, kd_ref, _) in enumerate(dirs):
        reverse = d == 1
        tri = ((s_idx >= t_idx) if reverse else (s_idx <= t_idx)).astype(F32)
        strict = same & ((sl > tl) if reverse else (sl < tl))
        incl = same & ((sl >= tl) if reverse else (sl <= tl))
        for j in range(nc):
            rows = slice(j * c, (j + 1) * c)
            lw = lw_ref[0, rows, :]
            lg = _dot(tri, lw, HI)
            ltot = lg[0:1] if reverse else lg[c - 1:c]
            ginv = jnp.exp(-lg)
            gend = jnp.exp(ltot - lg)
            be, kd = be_ref[0, rows, :], kd_ref[0, rows, :]
            cols = (a_ref[0, rows, :] * jnp.exp(lg - lw), r_ref[0, rows, :] * jnp.exp(lg),
                    be * ginv, kd * ginv, be * gend, kd * gend, v_ref[0, rows, :])
            gtot = jnp.exp(ltot)
            for p in range(npair):
                ps = slice(p * V7X_LANES, (p + 1) * V7X_LANES)
                a_s, r_s, b_s, k_s, bh_s, kh_s, v_s = (stack(x[:, ps]).astype(BF16) for x in cols)
                units.append(dict(d=d, j=j, p=p, rows=rows, ps=ps, strict=strict, incl=incl,
                                  ar=jnp.concatenate([a_s, r_s], axis=0),
                                  bk=jnp.concatenate([b_s, k_s], axis=0),
                                  bh=bh_s, kh=kh_s, v=v_s, gtot=gtot[:, ps]))
    for u in units:
        sc = _dot_nt(u["ar"], u["bk"])
        u["l_ab"] = jnp.where(u["strict"], sc[:n2, :n2], 0.0)
        u["l_ak"] = jnp.where(u["strict"], sc[:n2, n2:], 0.0).astype(BF16)
        u["m_rb"] = jnp.where(u["incl"], sc[n2:, :n2], 0.0).astype(BF16)
        u["m_rk"] = jnp.where(u["incl"], sc[n2:, n2:], 0.0).astype(BF16)
    for u, tinv in zip(units, _tri_inverse_all([u["l_ab"] for u in units], eye)):
        u["tinv"] = tinv.astype(BF16)
    for u in units:
        u["lakv"] = _dot(u["l_ak"], u["v"])
        u["mv"] = _dot(u["m_rk"], u["v"])
        u["kv"] = _dot_tn(u["v"], u["kh"])

    state = {(d, p): h_scr[d, p] for d in range(2) for p in range(npair)}
    for step in range(nc):
        live = [u for u in units if u["j"] == (nc - 1 - step if u["d"] == 1 else step)]
        for u in live:
            u["ah"] = _dot_nt(u["ar"], state[u["d"], u["p"]].astype(BF16))
        for u in live:
            u["u"] = _dot(u["tinv"], (u["ah"][:n2] + u["lakv"]).astype(BF16)).astype(BF16)
        for u in live:
            o_s = u["ah"][n2:] + u["mv"] + _dot(u["m_rb"], u["u"])
            dirs[u["d"]][6][0, u["rows"], u["ps"]] = o_s[:c] + o_s[c:]
            key = (u["d"], u["p"])
            state[key] = state[key] * u["gtot"] + u["kv"] + _dot_tn(u["u"], u["bh"])
    for (d, p), h in state.items():
        h_scr[d, p] = h


def rwkv_scan(r, v, nkk, lw_f, be_f, kd_f, lw_b, be_b, kd_b, nc):
    bt, p, width = r.shape
    rows = nc * RW_CHUNK
    n = p // rows
    blk = (1, rows, width)
    fwd = pl.BlockSpec(blk, lambda b, c: (b, c, 0))
    bwd = pl.BlockSpec(blk, lambda b, c: (b, n - 1 - c, 0))
    out = jax.ShapeDtypeStruct((bt, p, width), F32)
    return pl.pallas_call(
        functools.partial(_rwkv_chunk_kernel, nc=nc),
        grid=(bt, n),
        in_specs=[fwd] * 6 + [bwd] * 6,
        out_specs=[fwd, bwd],
        out_shape=[out, out],
        scratch_shapes=[pltpu.VMEM((2, width // V7X_LANES, V7X_LANES, V7X_LANES), F32)],
        compiler_params=_params(("parallel", "arbitrary")),
        name="rwkv_scan",
    )(r, v, nkk, lw_f, be_f, kd_f, r, v, nkk, lw_b, be_b, kd_b)


def _rwkv_post_kernel(of_ref, ob_ref, bonus_ref, gate_ref, g_ref, b_ref, bd_ref, o_ref):
    o = of_ref[0] + ob_ref[0]
    bd = bd_ref[...]
    inv = 1.0 / RW_HEAD_DIM
    mean = _group_sum(o, bd) * inv
    cen = o - mean
    var = _group_sum(cen * cen, bd) * inv
    y = cen * lax.rsqrt(var + RW_LNX_EPS) * g_ref[...] + b_ref[...]
    o_ref[0] = (y + bonus_ref[0]) * gate_ref[0]


def rwkv_post(o_f, o_b, bonus, gate, lnx_g, lnx_b, bd_ones, tr):
    bt, p, width = o_f.shape
    spec = pl.BlockSpec((1, tr, width), lambda b, i: (b, i, 0))
    full = lambda a: pl.BlockSpec(a.shape, lambda b, i: (0,) * a.ndim)
    return pl.pallas_call(
        _rwkv_post_kernel,
        grid=(bt, p // tr),
        in_specs=[spec, spec, spec, spec, full(lnx_g), full(lnx_b), full(bd_ones)],
        out_specs=spec,
        out_shape=jax.ShapeDtypeStruct((bt, p, width), F32),
        compiler_params=_params(("parallel", "parallel")),
        name="rwkv_post",
    )(o_f, o_b, bonus, gate, lnx_g, lnx_b, bd_ones)


def _merge_kernel(x_ref, g_ref, valid_ref, y0_ref, y1_ref, y2_ref, y3_ref, wg_ref, bp_ref, wo_ref,
                  o_ref):
    x = x_ref[...]
    d = x.shape[1]
    h = _masked_rms_norm(x, g_ref[...], valid_ref[...]).astype(BF16)
    merged = jnp.zeros_like(x)
    for n, y_ref in enumerate((y0_ref, y1_ref, y2_ref, y3_ref)):
        gate = _sigmoid(_dot(h, wg_ref[:, n * d:(n + 1) * d]))
        merged = merged + gate * _dot(y_ref[...].astype(BF16), bp_ref[n])
    o_ref[...] = x + _dot(merged.astype(BF16), wo_ref[...])


def merge(x2d, g, valid, ys, w_gate, branch_proj, w_out, tm):
    m, d = x2d.shape
    bw = ys[0].shape[1]
    row = lambda width: pl.BlockSpec((tm, width), lambda i: (i, 0))
    full = lambda a: pl.BlockSpec(a.shape, lambda i: (0,) * a.ndim)
    return pl.pallas_call(
        _merge_kernel,
        grid=(m // tm,),
        in_specs=[row(d), full(g), row(1)] + [row(bw)] * N_BRANCH
                 + [full(w_gate), full(branch_proj), full(w_out)],
        out_specs=row(d),
        out_shape=jax.ShapeDtypeStruct((m, d), F32),
        compiler_params=_params(("parallel",)),
        name="merge",
    )(x2d, g, valid, *ys, w_gate, branch_proj, w_out)


def _mlp_kernel(x_ref, g_ref, w1_ref, w2_ref, o_ref, h_scr, acc_scr):
    f = pl.program_id(1)

    @pl.when(f == 0)
    def _():
        x = x_ref[...]
        ms = jnp.mean(x * x, axis=-1, keepdims=True)
        h_scr[...] = (x * lax.rsqrt(ms + NORM_EPS) * g_ref[...]).astype(BF16)
        acc_scr[...] = jnp.zeros_like(acc_scr)

    a = jnp.maximum(_dot(h_scr[...], w1_ref[...]), 0.0)
    acc_scr[...] += _dot((a * a).astype(BF16), w2_ref[...])

    @pl.when(f == pl.num_programs(1) - 1)
    def _():
        o_ref[...] = x_ref[...] + acc_scr[...]


def mlp(x2d, g, w1, w2, tm, tf):
    m, d = x2d.shape
    dff = w1.shape[1]
    return pl.pallas_call(
        _mlp_kernel,
        grid=(m // tm, dff // tf),
        in_specs=[pl.BlockSpec((tm, d), lambda i, f: (i, 0)),
                  pl.BlockSpec((1, d), lambda i, f: (0, 0)),
                  pl.BlockSpec((d, tf), lambda i, f: (0, f)),
                  pl.BlockSpec((tf, d), lambda i, f: (f, 0))],
        out_specs=pl.BlockSpec((tm, d), lambda i, f: (i, 0)),
        out_shape=jax.ShapeDtypeStruct((m, d), F32),
        scratch_shapes=[pltpu.VMEM((tm, d), BF16), pltpu.VMEM((tm, d), F32)],
        compiler_params=_params(("parallel", "arbitrary")),
        name="mlp",
    )(x2d, g, w1, w2)


def _rope_tables(p):
    half = ROPE_DIM // 2
    pos = jnp.arange(p, dtype=F32) - FRONT_PAD
    inv = ROPE_THETA ** (-jnp.arange(half, dtype=F32) / half)
    ang = pos[:, None] * inv[None, :]
    cos, sin = jnp.cos(ang), jnp.sin(ang)
    ones = jnp.ones((p, DA_QK_DIM - ROPE_DIM), F32)
    zeros = jnp.zeros((p, DA_QK_DIM - ROPE_DIM), F32)
    zh = jnp.zeros((p, half), F32)
    c = jnp.concatenate([cos, cos, ones], axis=1)
    s1 = jnp.concatenate([-sin, zh, zeros], axis=1)
    s2 = jnp.concatenate([zh, sin, zeros], axis=1)
    return tuple(jnp.tile(t, (1, 2)) for t in (c, s1, s2))


def _block_diag_ones(n, blk):
    i = jnp.arange(n)
    return ((i[:, None] // blk) == (i[None, :] // blk)).astype(F32)


def kernel(x_prompt, x_sample, meta_tokens, norm_mix_g, w_in, hgrn_lb_logits, hgrn_onorm_g, conv_w,
           diff_qnorm_g, diff_knorm_g, diff_lambda, diff_subln_g, rwkv_mu, rwkv_w0, rwkv_w2, rwkv_a0,
           rwkv_a2, rwkv_g2, rwkv_k_k, rwkv_k_a, rwkv_r_k, rwkv_lnx_g, rwkv_lnx_b, w_gate,
           branch_proj, w_out, norm_mlp_g, mlp_w1, mlp_w2):
    assert x_prompt.shape[1:] == x_sample.shape[1:]
    x = jnp.concatenate([x_prompt, x_sample], axis=0)
    bt, seq, d = x.shape
    depth = w_in.shape[0]
    length = N_META + seq
    p = -(-(FRONT_PAD + length) // V7X_LANES) * V7X_LANES
    m = bt * p
    bw = branch_proj.shape[2]
    rw_cols = rwkv_mu.shape[1]
    sizes = (5 * bw, 3 * bw, 3 * bw, rw_cols)
    offs = [0]
    for s in sizes:
        offs.append(offs[-1] + s)

    meta = jnp.broadcast_to(meta_tokens.astype(x.dtype)[None], (bt, N_META, d))
    xp = jnp.concatenate([jnp.zeros((bt, FRONT_PAD, d), x.dtype), meta, x,
                          jnp.zeros((bt, p - FRONT_PAD - length, d), x.dtype)], axis=1)
    x2d = xp.reshape(m, d)

    rows = jnp.arange(p)
    valid_seq = ((rows >= FRONT_PAD) & (rows < FRONT_PAD + length)).astype(F32)[:, None]
    valid = jnp.tile(valid_seq, (bt, 1))
    key_bias = jnp.where(valid_seq[:, 0] > 0, 0.0, NEG_BIG).astype(F32)[None, :]
    tables = _rope_tables(p)
    bd64_mean = (_block_diag_ones(V7X_LANES, DA_QK_DIM) / DA_QK_DIM).astype(BF16)
    bd64_ones = _block_diag_ones(V7X_LANES, RW_HEAD_DIM).astype(BF16)

    sm = jax.nn.softmax(hgrn_lb_logits.astype(F32), axis=1)
    lb_all = jnp.cumsum(sm, axis=1) - sm[:, :1]

    tm_proj = _pick(m, 1536, 128)
    tm_merge = _pick(m, 256, 128)
    tm_mlp = _pick(m, 768, 128)
    tq = _pick(p, 384, 128)
    tk = _pick(p, 384, 128)
    assert FRONT_PAD <= tk and p - (FRONT_PAD + length) <= tk
    tr = _pick(p, 384, 8)

    half = V7X_LANES // 2
    for l in range(depth):
        g_mix = norm_mix_g[l][None, :]
        w_l = w_in[l].astype(BF16)
        z = []
        for gi, size in enumerate(sizes):
            tn = _pick(size, 640, V7X_LANES)
            z.append(norm_matmul(x2d, g_mix, valid, w_l[:, offs[gi]:offs[gi + 1]], tm_proj, tn)
                     .reshape(bt, p, size))
        z_hg, z_sc, z_da, z_rw = z

        of, ob = hgrn_scan(z_hg, valid_seq, lb_all[:, l],
                           _pick(p // HG_CHUNK, HG_CHUNKS_PER_STEP, 1))
        y_hg = hgrn_post(of, ob, z_hg, hgrn_onorm_g[l][None, :], tr)

        y_sc = shortconv(z_sc, conv_w[l])

        lam_init = 0.8 - 0.6 * math.exp(-0.3 * l)
        y_da = diff_attention(z_da, tables, bd64_mean,
                              jnp.tile(diff_qnorm_g[l], 2)[None, :], jnp.tile(diff_knorm_g[l], 2)[None, :],
                              diff_lambda[l], diff_subln_g[l][None, :], key_bias, lam_init, tq, tk)

        zpad = jnp.zeros((2, half, bw), F32)
        w2pad = jnp.stack([jnp.concatenate([rwkv_w2[l, 0], zpad[0]], 0),
                           jnp.concatenate([zpad[0], rwkv_w2[l, 1]], 0)])
        a2pad = jnp.stack([jnp.concatenate([rwkv_a2[l, 0], zpad[0]], 0),
                           jnp.concatenate([zpad[0], rwkv_a2[l, 1]], 0)])
        (r, v, nkk, lwf, bef, kdf, lwb, beb, kdb, gate, bonus) = rwkv_prep(
            z_rw, valid_seq, rwkv_mu[l][None, :], rwkv_w0[l], w2pad, rwkv_a0[l], a2pad, rwkv_g2[l],
            rwkv_k_k[l][None, :], rwkv_k_a[l][None, :], rwkv_r_k[l].reshape(1, bw), bd64_ones, tr)
        orf, orb = rwkv_scan(r, v, nkk, lwf, bef, kdf, lwb, beb, kdb,
                             _pick(p // RW_CHUNK, RW_CHUNKS_PER_STEP, 1))
        y_rw = rwkv_post(orf, orb, bonus, gate, rwkv_lnx_g[l][None, :], rwkv_lnx_b[l][None, :],
                         bd64_ones, tr)

        ys = [y.reshape(m, bw) for y in (y_hg, y_sc, y_da, y_rw)]
        x2d = merge(x2d, g_mix, valid, ys, w_gate[l].astype(BF16), branch_proj[l].astype(BF16),
                    w_out[l].astype(BF16), tm_merge)
        x2d = mlp(x2d, norm_mlp_g[l][None, :], mlp_w1[l].astype(BF16), mlp_w2[l].astype(BF16),
                  tm_mlp, _pick(mlp_w1.shape[2], 512, V7X_LANES))

    y = x2d.reshape(bt, p, d)[:, FRONT_PAD + N_META:FRONT_PAD + length]
    nb = x_prompt.shape[0]
    return (y[:nb], y[nb:])
```

```python
import functools
import math

import jax
import jax.numpy as jnp
from jax import lax
from jax.experimental import pallas as pl
from jax.experimental.pallas import tpu as pltpu

F32 = jnp.float32
BF16 = jnp.bfloat16
HI = lax.Precision.HIGHEST

V7X_LANES = 128
V7X_VMEM_BYTES = 64 * 1024 * 1024
VMEM_LIMIT = V7X_VMEM_BYTES - 8 * 1024 * 1024

N_META = 16
NORM_EPS = 1e-6
N_BRANCH = 4
HG_HEAD_DIM = 128
HG_CHUNK = 64
HG_SUB = 16
LB_FLOOR = 1e-20
SC_KSIZE = 3
DA_HEADS = 4
DA_QK_DIM = 64
DA_V_DIM = 128
ROPE_THETA = 500000.0
ROPE_DIM = DA_QK_DIM // 4
SUBLN_EPS = 1e-5
RW_HEAD_DIM = 64
RW_CHUNK = 64
RW_LNX_EPS = 64e-5
FRONT_PAD = (-N_META) % HG_CHUNK
NEG_BIG = -1e30
LOG2_E = 1.4426950408889634

P_HG = "bf16"
P_RW_LR = "bf16"
RW_CHUNKS_PER_STEP = 2
HG_CHUNKS_PER_STEP = 2


def _split_bf16(a):
    hi = a.astype(BF16)
    return hi, (a - hi.astype(F32)).astype(BF16)


def _mm(a, b, dims, precision):
    dg = functools.partial(lax.dot_general, dimension_numbers=(dims, ((), ())),
                           preferred_element_type=F32)
    if precision == "bf16":
        return dg(a.astype(BF16), b.astype(BF16))
    if precision == "x3":
        ah, al = _split_bf16(a)
        bh, bl = _split_bf16(b)
        return dg(ah, bh) + (dg(al, bh) + dg(ah, bl))
    return dg(a, b, precision=precision)


def _dot(a, b, precision=None):
    return _mm(a, b, ((1,), (0,)), precision)


def _dot_split_rhs3(a_bf16, b):
    b1 = b.astype(BF16)
    r1 = b - b1.astype(F32)
    b2 = r1.astype(BF16)
    b3 = (r1 - b2.astype(F32)).astype(BF16)
    return _dot(a_bf16, b1) + (_dot(a_bf16, b2) + _dot(a_bf16, b3))


def _dot_split_lhs(a, b_bf16):
    hi, lo = _split_bf16(a)
    return _dot(hi, b_bf16) + _dot(lo, b_bf16)


def _dot_nt(a, b, precision=None):
    return _mm(a, b, ((1,), (1,)), precision)


def _dot_tn(a, b, precision=None):
    return _mm(a, b, ((0,), (0,)), precision)


def _pick(n, target, mult):
    best = None
    for d in range(mult, min(n, target) + 1, mult):
        if n % d == 0:
            best = d
    assert best is not None, (n, target, mult)
    return best


def _params(sem):
    return pltpu.CompilerParams(dimension_semantics=sem, vmem_limit_bytes=VMEM_LIMIT)


def _log_sigmoid(x):
    return jnp.minimum(x, 0.0) - jnp.log1p(jnp.exp(-jnp.abs(x)))


def _sigmoid(x):
    return 1.0 / (1.0 + jnp.exp(-x))


def _masked_rms_norm(x, g, valid):
    ms = jnp.mean(x * x, axis=-1, keepdims=True)
    h = x * lax.rsqrt(ms + NORM_EPS) * g
    return jnp.where(valid > 0.0, h, 0.0)


def _norm_matmul_kernel(x_ref, g_ref, valid_ref, w_ref, o_ref, h_scr):
    @pl.when(pl.program_id(1) == 0)
    def _():
        h_scr[...] = _masked_rms_norm(x_ref[...], g_ref[...], valid_ref[...]).astype(BF16)

    o_ref[...] = _dot(h_scr[...], w_ref[...])


def norm_matmul(x2d, g, valid, w_bf16, tm, tn):
    m, d = x2d.shape
    n = w_bf16.shape[1]
    return pl.pallas_call(
        _norm_matmul_kernel,
        grid=(m // tm, n // tn),
        in_specs=[
            pl.BlockSpec((tm, d), lambda i, j: (i, 0)),
            pl.BlockSpec((1, d), lambda i, j: (0, 0)),
            pl.BlockSpec((tm, 1), lambda i, j: (i, 0)),
            pl.BlockSpec((d, tn), lambda i, j: (0, j)),
        ],
        out_specs=pl.BlockSpec((tm, tn), lambda i, j: (i, j)),
        out_shape=jax.ShapeDtypeStruct((m, n), F32),
        scratch_shapes=[pltpu.VMEM((tm, d), BF16)],
        compiler_params=_params(("parallel", "arbitrary")),
        name="norm_matmul",
    )(x2d, g, valid, w_bf16)


def _shortconv_kernel(b_ref, c_ref, h_ref, w_ref, o_ref):
    u = c_ref[0] * h_ref[0]
    p = u.shape[0]
    w = w_ref[...]
    y = pltpu.roll(u, 1, 0) * w[0:1] + u * w[1:2] + pltpu.roll(u, p - 1, 0) * w[2:3]
    o_ref[0] = b_ref[0] * y


def shortconv(z_sc, conv_w):
    bt, p, w3 = z_sc.shape
    width = w3 // 3
    nb = width // V7X_LANES
    blk = (1, p, V7X_LANES)
    return pl.pallas_call(
        _shortconv_kernel,
        grid=(bt, nb),
        in_specs=[
            pl.BlockSpec(blk, lambda b, c: (b, 0, c)),
            pl.BlockSpec(blk, lambda b, c: (b, 0, nb + c)),
            pl.BlockSpec(blk, lambda b, c: (b, 0, 2 * nb + c)),
            pl.BlockSpec((SC_KSIZE, V7X_LANES), lambda b, c: (0, c)),
        ],
        out_specs=pl.BlockSpec(blk, lambda b, c: (b, 0, c)),
        out_shape=jax.ShapeDtypeStruct((bt, p, width), F32),
        compiler_params=_params(("parallel", "parallel")),
        name="shortconv",
    )(z_sc, z_sc, z_sc, conv_w)


def _attn_kernel(q_ref, k_ref, v_ref, cq_ref, s1q_ref, s2q_ref, ck_ref, s1k_ref, s2k_ref,
                 bd_ref, qg_ref, kg_ref, lam_ref, sg_ref, bias_ref, o_ref, k_scr, v_scr, s_scr,
                 *, lam_init, tk):
    bd = bd_ref[...]

    def norm_rope(x, g, c, s1, s2):
        ms = _dot_split_lhs(x * x, bd)
        y = x * lax.rsqrt(ms + NORM_EPS) * g
        return (y * c + pltpu.roll(y, V7X_LANES - ROPE_DIM // 2, 1) * s1
                + pltpu.roll(y, ROPE_DIM // 2, 1) * s2)

    @pl.when(pl.program_id(2) == 0)
    def _():
        k = norm_rope(k_ref[0], kg_ref[...], ck_ref[...], s1k_ref[...], s2k_ref[...])
        k_scr[...] = k.astype(BF16)
        v_scr[...] = v_ref[0].astype(BF16)

    q = norm_rope(q_ref[0], qg_ref[...], cq_ref[...], s1q_ref[...], s2q_ref[...])
    q = q * (DA_QK_DIM ** -0.5 * LOG2_E)
    tq = q.shape[0]
    first = lax.broadcasted_iota(jnp.int32, q.shape, 1) < DA_QK_DIM
    lp = lam_ref[...]
    lam = (jnp.exp(jnp.sum(lp[0:1] * lp[1:2], axis=-1, keepdims=True))
           - jnp.exp(jnp.sum(lp[2:3] * lp[3:4], axis=-1, keepdims=True)) + lam_init)
    qs = jnp.concatenate([jnp.where(first, q, 0.0), jnp.where(first, 0.0, q)], axis=0).astype(BF16)
    nk = k_scr.shape[0] // tk
    lanes = range(0, tk, V7X_LANES)
    m128 = None
    for j in range(nk):
        cols = slice(j * tk, (j + 1) * tk)
        s_j = _dot_nt(qs, k_scr[cols, :])
        if j in (0, nk - 1):
            s_j = s_j + bias_ref[:, cols]
        s_scr[:, cols] = s_j
        for c in lanes:
            piece = s_j[:, c:c + V7X_LANES]
            m128 = piece if m128 is None else jnp.maximum(m128, piece)
    m = jnp.max(m128, axis=-1, keepdims=True)
    l128 = jnp.zeros_like(m128)
    o2 = jnp.zeros((2 * tq, DA_V_DIM), F32)
    for j in range(nk):
        cols = slice(j * tk, (j + 1) * tk)
        pr = jnp.exp2(s_scr[:, cols] - m)
        for c in lanes:
            l128 = l128 + pr[:, c:c + V7X_LANES]
        o2 = o2 + _dot(pr.astype(BF16), v_scr[cols, :])
    inv = 1.0 / jnp.sum(l128, axis=-1, keepdims=True)
    o = o2[:tq] * inv[:tq] - o2[tq:] * (lam * inv[tq:])
    ms = jnp.mean(o * o, axis=-1, keepdims=True)
    o_ref[0] = o * lax.rsqrt(ms + SUBLN_EPS) * sg_ref[...] * (1.0 - lam_init)


def diff_attention(z_da, tables, bd64, qn_g, kn_g, lam_p, subln_g, key_bias, lam_init, tq, tk):
    bt, p, _ = z_da.shape
    c_t, s1_t, s2_t = tables
    nh = DA_HEADS
    qspec = pl.BlockSpec((1, tq, V7X_LANES), lambda b, h, i: (b, i, h))
    kspec = pl.BlockSpec((1, p, V7X_LANES), lambda b, h, i: (b, 0, nh + h))
    vspec = pl.BlockSpec((1, p, V7X_LANES), lambda b, h, i: (b, 0, 2 * nh + h))
    tq_spec = pl.BlockSpec((tq, V7X_LANES), lambda b, h, i: (i, 0))
    tk_spec = pl.BlockSpec((p, V7X_LANES), lambda b, h, i: (0, 0))
    full = lambda shape: pl.BlockSpec(shape, lambda b, h, i: (0,) * len(shape))
    return pl.pallas_call(
        functools.partial(_attn_kernel, lam_init=lam_init, tk=tk),
        grid=(bt, nh, p // tq),
        in_specs=[qspec, kspec, vspec, tq_spec, tq_spec, tq_spec, tk_spec, tk_spec, tk_spec,
                  full((V7X_LANES, V7X_LANES)), full((1, V7X_LANES)), full((1, V7X_LANES)),
                  full((4, DA_QK_DIM)), full((1, V7X_LANES)), full((1, p))],
        out_specs=pl.BlockSpec((1, tq, V7X_LANES), lambda b, h, i: (b, i, h)),
        out_shape=jax.ShapeDtypeStruct((bt, p, nh * DA_V_DIM), F32),
        scratch_shapes=[pltpu.VMEM((p, V7X_LANES), BF16), pltpu.VMEM((p, V7X_LANES), BF16),
                        pltpu.VMEM((2 * tq, p), F32)],
        compiler_params=_params(("parallel", "parallel", "arbitrary")),
        name="diff_attention",
    )(z_da, z_da, z_da, c_t, s1_t, s2_t, c_t, s1_t, s2_t, bd64, qn_g, kn_g, lam_p, subln_g,
      key_bias)


def _gla_local(q, qt, qh, k, kh, v, b, bpiv, reverse):
    c, sub, half = HG_CHUNK, HG_SUB, HG_SUB // 2
    nsub = c // sub
    row = lax.broadcasted_iota(jnp.int32, (c, 1), 0)
    attn = jnp.zeros((c, c), F32)
    for i in range(nsub):
        if (reverse and i == nsub - 1) or (not reverse and i == 0):
            continue
        bi = bpiv[i * sub:i * sub + 1]
        kmask = (row >= (i + 1) * sub) if reverse else (row < i * sub)
        kt = jnp.where(kmask, k * jnp.exp(jnp.minimum(bi - b, 0.0)), 0.0)
        attn = jnp.where(row // sub == i, _dot_nt(qt, kt, P_HG), attn)
    t_idx = lax.broadcasted_iota(jnp.int32, (c, c), 0)
    s_idx = lax.broadcasted_iota(jnp.int32, (c, c), 1)
    attn = jnp.where(t_idx // sub == s_idx // sub, _dot_nt(qh, kh, P_HG), attn)

    b2 = b * LOG2_E
    tloc = lax.broadcasted_iota(jnp.int32, (half, c), 0)
    scol = lax.broadcasted_iota(jnp.int32, (half, c), 1)
    blocks = []
    for i in range(c // half):
        bt, qi = b2[i * half:(i + 1) * half], q[i * half:(i + 1) * half]
        blk = attn[i * half:(i + 1) * half]
        for j in range(half):
            s = i * half + j
            a = jnp.sum(qi * k[s:s + 1] * jnp.exp2(bt - b2[s:s + 1]), axis=-1, keepdims=True)
            keep = (scol == s) & ((tloc <= j) if reverse else (tloc >= j))
            blk = jnp.where(keep, a, blk)
        blocks.append(blk)
    return _dot(jnp.concatenate(blocks, axis=0), v, P_HG)


def _hgrn_kernel(qf_ref, ff_ref, vf_ref, mf_ref, qb_ref, fb_ref, vb_ref, mb_ref, lb_ref,
                 of_ref, ob_ref, st_scr, *, nc):
    @pl.when(pl.program_id(1) == 0)
    def _():
        st_scr[...] = jnp.zeros_like(st_scr)

    c, sub, half = HG_CHUNK, HG_SUB, HG_SUB // 2
    nh = st_scr.shape[1]
    t_idx = lax.broadcasted_iota(jnp.int32, (c, c), 0)
    s_idx = lax.broadcasted_iota(jnp.int32, (c, c), 1)
    second = (lax.broadcasted_iota(jnp.int32, (c, 1), 0) % sub) >= half
    dirs = ((qf_ref, ff_ref, vf_ref, mf_ref, of_ref), (qb_ref, fb_ref, vb_ref, mb_ref, ob_ref))
    units = []
    for d, (q_ref, f_ref, v_ref, m_ref, _) in enumerate(dirs):
        reverse = d == 1
        if reverse:
            mats = (s_idx >= t_idx, s_idx >= (t_idx // sub + 1) * sub,
                    s_idx >= (t_idx // half + 1) * half, s_idx >= (t_idx // half) * half)
        else:
            mats = (s_idx <= t_idx, s_idx < (t_idx // sub) * sub,
                    s_idx < (t_idx // half) * half, s_idx < (t_idx // half + 1) * half)
        cmat = jnp.concatenate([x.astype(F32) for x in mats], axis=0).astype(BF16)
        q_half = jnp.logical_not(second) if reverse else second
        lb = lb_ref[d:d + 1, :]
        la = jnp.log(jnp.maximum(lb, LB_FLOOR))
        l1 = jnp.log1p(-lb)
        for j in range(nc):
            rows = slice(j * c, (j + 1) * c)
            valid = m_ref[rows, :] > 0.0
            lc = l1 + _log_sigmoid(f_ref[0, rows, :])
            logf = jnp.maximum(la, lc) + jnp.log1p(jnp.exp(-jnp.abs(la - lc)))
            logf = jnp.where(valid, logf, 0.0)
            k = jnp.where(valid, 1.0 - jnp.exp(logf), 0.0)
            cum = _dot_split_rhs3(cmat, logf)
            b, bpiv, bhin, bhout = (cum[i * c:(i + 1) * c] for i in range(4))
            btot = b[0:1] if reverse else b[c - 1:c]
            q, v = q_ref[0, rows, :], v_ref[0, rows, :]
            qt = q * jnp.exp(b - bpiv)
            qh = jnp.where(q_half, q * jnp.exp(b - bhin), 0.0)
            kh = jnp.where(q_half, 0.0, k * jnp.exp(bhout - b))
            qe = (q * jnp.exp(b)).astype(BF16)
            k2 = k * jnp.exp(btot - b)
            dec = jnp.exp(btot)
            for h in range(nh):
                hs = slice(h * HG_HEAD_DIM, (h + 1) * HG_HEAD_DIM)
                units.append(dict(
                    d=d, j=j, h=h, rows=rows, hs=hs, qe=qe[:, hs], dec=dec[:, hs],
                    o=_gla_local(q[:, hs], qt[:, hs], qh[:, hs], k[:, hs], kh[:, hs], v[:, hs],
                                 b[:, hs], bpiv[:, hs], reverse),
                    kv=_dot_tn(v[:, hs], k2[:, hs], P_HG)))

    state = {(d, h): st_scr[d, h] for d in range(2) for h in range(nh)}
    for step in range(nc):
        for u in units:
            if u["j"] != (nc - 1 - step if u["d"] == 1 else step):
                continue
            key = (u["d"], u["h"])
            dirs[u["d"]][4][0, u["rows"], u["hs"]] = u["o"] + _dot_nt(u["qe"], state[key].astype(BF16))
            state[key] = state[key] * u["dec"] + u["kv"]
    for (d, h), s in state.items():
        st_scr[d, h] = s


def hgrn_scan(z_hg, valid_seq, lb, nc):
    bt, p, w5 = z_hg.shape
    width = w5 // 5
    nh = width // HG_HEAD_DIM
    rows = nc * HG_CHUNK
    n = p // rows
    blk = (1, rows, width)

    def spec(group, rev):
        if rev:
            return pl.BlockSpec(blk, lambda b, i: (b, n - 1 - i, group))
        return pl.BlockSpec(blk, lambda b, i: (b, i, group))

    mf = pl.BlockSpec((rows, 1), lambda b, i: (i, 0))
    mb = pl.BlockSpec((rows, 1), lambda b, i: (n - 1 - i, 0))
    out = jax.ShapeDtypeStruct((bt, p, width), F32)
    return pl.pallas_call(
        functools.partial(_hgrn_kernel, nc=nc),
        grid=(bt, n),
        in_specs=[spec(0, False), spec(1, False), spec(3, False), mf,
                  spec(0, True), spec(2, True), spec(3, True), mb,
                  pl.BlockSpec((2, width), lambda b, i: (0, 0))],
        out_specs=[pl.BlockSpec(blk, lambda b, i: (b, i, 0)),
                   pl.BlockSpec(blk, lambda b, i: (b, n - 1 - i, 0))],
        out_shape=[out, out],
        scratch_shapes=[pltpu.VMEM((2, nh, HG_HEAD_DIM, HG_HEAD_DIM), F32)],
        compiler_params=_params(("parallel", "arbitrary")),
        name="hgrn_scan",
    )(z_hg, z_hg, z_hg, valid_seq, z_hg, z_hg, z_hg, valid_seq, lb)


def _group_sum(x, bd):
    cols = [_dot_split_lhs(x[:, j:j + V7X_LANES], bd) for j in range(0, x.shape[1], V7X_LANES)]
    return jnp.concatenate(cols, axis=1)


def _rwkv_prep_kernel(z_ref, zp_ref, zn_ref, valid_ref, mu_ref, w0_ref, w2_ref, a0_ref, a2_ref,
                      g2_ref, kk_ref, ka_ref, rk_ref, bd_ref,
                      r_out, v_out, nkk_out, lwf_out, bef_out, kdf_out, lwb_out, beb_out, kdb_out,
                      gate_out, bonus_out):
    u = z_ref[0]
    tr = u.shape[0]
    width = r_out.shape[-1]
    row = lax.broadcasted_iota(jnp.int32, (tr, 1), 0)
    u_prev = jnp.where(row == 0, zp_ref[0][7:8], pltpu.roll(u, 1, 0))
    u_next = jnp.where(row == tr - 1, zn_ref[0][0:1], pltpu.roll(u, tr - 1, 0))
    xm = u + mu_ref[...] * (0.5 * (u_prev + u_next) - u)
    valid = valid_ref[...] > 0.0
    bd = bd_ref[...]
    r = xm[:, 0:width]
    k = xm[:, width:2 * width]
    v = jnp.where(valid, xm[:, 2 * width:3 * width], 0.0)
    lr = 3 * width
    wl = jnp.tanh(xm[:, lr:lr + V7X_LANES])
    al = xm[:, lr + V7X_LANES:lr + 2 * V7X_LANES]
    gl = xm[:, lr + 2 * V7X_LANES:lr + 3 * V7X_LANES]
    kk = k * kk_ref[...]
    kk = kk / jnp.maximum(jnp.sqrt(_group_sum(kk * kk, bd)), 1e-12)
    kk = jnp.where(valid, kk, 0.0)
    r_out[0] = r
    v_out[0] = v
    nkk_out[0] = -kk
    kd_sum = jnp.zeros_like(k)
    for d, (lw_out, be_out, kd_out) in enumerate(((lwf_out, bef_out, kdf_out),
                                                   (lwb_out, beb_out, kdb_out))):
        wlog = _log_sigmoid(w0_ref[d:d + 1, :] + _dot(wl, w2_ref[d], P_RW_LR)) - 0.5
        lw_out[0] = jnp.where(valid, -jnp.exp(wlog), 0.0)
        a = _sigmoid(a0_ref[d:d + 1, :] + _dot(al, a2_ref[d], P_RW_LR))
        kd = k * (1.0 + (a - 1.0) * ka_ref[...])
        kd_sum = kd_sum + kd
        kd_out[0] = jnp.where(valid, kd, 0.0)
        be_out[0] = kk * a
    gate_out[0] = _dot(_sigmoid(gl), g2_ref[...], P_RW_LR)
    bonus_out[0] = _group_sum(r * kd_sum * rk_ref[...], bd) * v


def rwkv_prep(z_rw, valid2d, mu, w0, w2pad, a0, a2pad, g2, k_k, k_a, r_k, bd_ones, tr):
    bt, p, cols = z_rw.shape
    width = w0.shape[1]
    nt = p // tr
    r8 = tr // 8
    last8 = p // 8 - 1
    full = lambda a: pl.BlockSpec(a.shape, lambda b, i: (0,) * a.ndim)
    out = jax.ShapeDtypeStruct((bt, p, width), F32)
    ospec = pl.BlockSpec((1, tr, width), lambda b, i: (b, i, 0))
    params = (mu, w0, w2pad, a0, a2pad, g2, k_k, k_a, r_k, bd_ones)
    return pl.pallas_call(
        _rwkv_prep_kernel,
        grid=(bt, nt),
        in_specs=[pl.BlockSpec((1, tr, cols), lambda b, i: (b, i, 0)),
                  pl.BlockSpec((1, 8, cols), lambda b, i: (b, jnp.maximum(i * r8 - 1, 0), 0)),
                  pl.BlockSpec((1, 8, cols), lambda b, i: (b, jnp.minimum((i + 1) * r8, last8), 0)),
                  pl.BlockSpec((tr, 1), lambda b, i: (i, 0))] + [full(a) for a in params],
        out_specs=[ospec] * 11,
        out_shape=[out] * 11,
        compiler_params=_params(("parallel", "parallel")),
        name="rwkv_prep",
    )(z_rw, z_rw, z_rw, valid2d, *params)


def _tri_inverse_all(l_list, eye):
    n = range(len(l_list))
    m = l_list[0].shape[0]
    l_hi_lo = [_split_bf16(l) for l in l_list]
    pw = [hl[0] for hl in l_hi_lo]
    tinv = [eye + l for l in l_list]
    levels = int(math.log2(RW_CHUNK))
    nxt = [_dot(pw[u], pw[u]).astype(BF16) for u in n]
    for j in range(1, levels):
        pw = nxt
        if j < levels - 1:
            both = [_dot(jnp.concatenate([pw[u], tinv[u].astype(BF16)], axis=0), pw[u]) for u in n]
            nxt = [both[u][:m].astype(BF16) for u in n]
            tinv = [tinv[u] + both[u][m:] for u in n]
        else:
            tinv = [tinv[u] + _dot(tinv[u].astype(BF16), pw[u]) for u in n]
    out = []
    for u in n:
        lh, ll = l_hi_lo[u]
        th, tl = _split_bf16(tinv[u])
        lt = _dot(jnp.concatenate([lh, ll], axis=0), th)
        resid = (eye - tinv[u]) + (lt[:m] + (lt[m:] + _dot(lh, tl)))
        out.append(tinv[u] + _dot(th, resid.astype(BF16)))
    return out


def _rwkv_chunk_kernel(rf_ref, vf_ref, af_ref, lwf_ref, bef_ref, kdf_ref,
                       rb_ref, vb_ref, ab_ref, lwb_ref, beb_ref, kdb_ref,
                       of_ref, ob_ref, h_scr, *, nc):
    @pl.when(pl.program_id(1) == 0)
    def _():
        h_scr[...] = jnp.zeros_like(h_scr)

    c = RW_CHUNK
    n2 = 2 * c
    npair = h_scr.shape[1]
    t_idx = lax.broadcasted_iota(jnp.int32, (c, c), 0)
    s_idx = lax.broadcasted_iota(jnp.int32, (c, c), 1)
    ti = lax.broadcasted_iota(jnp.int32, (n2, n2), 0)
    si = lax.broadcasted_iota(jnp.int32, (n2, n2), 1)
    same = (ti // c) == (si // c)
    tl, sl = ti % c, si % c
    eye = jnp.where(ti == si, 1.0, 0.0)
    first = lax.broadcasted_iota(jnp.int32, (c, V7X_LANES), 1) < RW_HEAD_DIM

    def stack(x):
        return jnp.concatenate([jnp.where(first, x, 0.0), jnp.where(first, 0.0, x)], axis=0)

    dirs = ((rf_ref, vf_ref, af_ref, lwf_ref, bef_ref, kdf_ref, of_ref),
            (rb_ref, vb_ref, ab_ref, lwb_ref, beb_ref, kdb_ref, ob_ref))
    units = []
    for d, (r_ref, v_ref, a_ref, lw_ref, be_ref, kd_ref, _) in enumerate(dirs):
        reverse = d == 1
        tri = ((s_idx >= t_idx) if reverse else (s_idx <= t_idx)).astype(F32).astype(BF16)
        strict = same & ((sl > tl) if reverse else (sl < tl))
        incl = same & ((sl >= tl) if reverse else (sl <= tl))
        for j in range(nc):
            rows = slice(j * c, (j + 1) * c)
            lw = lw_ref[0, rows, :]
            lg = _dot_split_rhs3(tri, lw)
            ltot = lg[0:1] if reverse else lg[c - 1:c]
            ginv = jnp.exp(-lg)
            gend = jnp.exp(ltot - lg)
            be, kd = be_ref[0, rows, :], kd_ref[0, rows, :]
            cols = (a_ref[0, rows, :] * jnp.exp(lg - lw), r_ref[0, rows, :] * jnp.exp(lg),
                    be * ginv, kd * ginv, be * gend, kd * gend, v_ref[0, rows, :])
            gtot = jnp.exp(ltot)
            for p in range(npair):
                ps = slice(p * V7X_LANES, (p + 1) * V7X_LANES)
                a_s, r_s, b_s, k_s, bh_s, kh_s, v_s = (stack(x[:, ps]).astype(BF16) for x in cols)
                units.append(dict(d=d, j=j, p=p, rows=rows, ps=ps, strict=strict, incl=incl,
                                  ar=jnp.concatenate([a_s, r_s], axis=0),
                                  bk=jnp.concatenate([b_s, k_s], axis=0),
                                  bh=bh_s, kh=kh_s, v=v_s, gtot=gtot[:, ps]))
    for u in units:
        sc = _dot_nt(u["ar"], u["bk"])
        u["l_ab"] = jnp.where(u["strict"], sc[:n2, :n2], 0.0)
        u["l_ak"] = jnp.where(u["strict"], sc[:n2, n2:], 0.0).astype(BF16)
        u["m_rb"] = jnp.where(u["incl"], sc[n2:, :n2], 0.0).astype(BF16)
        u["m_rk"] = jnp.where(u["incl"], sc[n2:, n2:], 0.0).astype(BF16)
    for u, tinv in zip(units, _tri_inverse_all([u["l_ab"] for u in units], eye)):
        u["tinv"] = tinv.astype(BF16)
    for u in units:
        both = _dot(jnp.concatenate([u["l_ak"], u["m_rk"]], axis=0), u["v"])
        u["lakv"], u["mv"] = both[:n2], both[n2:]
        u["kv"] = _dot_tn(u["v"], u["kh"])

    state = {(d, p): h_scr[d, p] for d in range(2) for p in range(npair)}
    for step in range(nc):
        live = [u for u in units if u["j"] == (nc - 1 - step if u["d"] == 1 else step)]
        for u in live:
            u["ah"] = _dot_nt(u["ar"], state[u["d"], u["p"]].astype(BF16))
        for u in live:
            u["u"] = _dot(u["tinv"], (u["ah"][:n2] + u["lakv"]).astype(BF16)).astype(BF16)
        for u in live:
            o_s = u["ah"][n2:] + u["mv"] + _dot(u["m_rb"], u["u"])
            dirs[u["d"]][6][0, u["rows"], u["ps"]] = o_s[:c] + o_s[c:]
            key = (u["d"], u["p"])
            state[key] = state[key] * u["gtot"] + u["kv"] + _dot_tn(u["u"], u["bh"])
    for (d, p), h in state.items():
        h_scr[d, p] = h


def rwkv_scan(r, v, nkk, lw_f, be_f, kd_f, lw_b, be_b, kd_b, nc):
    bt, p, width = r.shape
    rows = nc * RW_CHUNK
    n = p // rows
    blk = (1, rows, width)
    fwd = pl.BlockSpec(blk, lambda b, c: (b, c, 0))
    bwd = pl.BlockSpec(blk, lambda b, c: (b, n - 1 - c, 0))
    out = jax.ShapeDtypeStruct((bt, p, width), F32)
    return pl.pallas_call(
        functools.partial(_rwkv_chunk_kernel, nc=nc),
        grid=(bt, n),
        in_specs=[fwd] * 6 + [bwd] * 6,
        out_specs=[fwd, bwd],
        out_shape=[out, out],
        scratch_shapes=[pltpu.VMEM((2, width // V7X_LANES, V7X_LANES, V7X_LANES), F32)],
        compiler_params=_params(("parallel", "arbitrary")),
        name="rwkv_scan",
    )(r, v, nkk, lw_f, be_f, kd_f, r, v, nkk, lw_b, be_b, kd_b)


def _merge_kernel(x_ref, g_ref, valid_ref, hof_ref, hob_ref, hg_ref, ysc_ref, yda_ref,
                  rof_ref, rob_ref, bonus_ref, rgate_ref, onorm_ref, lnxg_ref, lnxb_ref, bd_ref,
                  wg_ref, bp_ref, wo_ref, o_ref):
    x = x_ref[...]
    d = x.shape[1]
    h = _masked_rms_norm(x, g_ref[...], valid_ref[...]).astype(BF16)

    o = hof_ref[...] + hob_ref[...]
    heads = []
    for c in range(0, o.shape[1], HG_HEAD_DIM):
        oh = o[:, c:c + HG_HEAD_DIM]
        heads.append(oh * lax.rsqrt(jnp.mean(oh * oh, axis=-1, keepdims=True) + NORM_EPS))
    hg = hg_ref[...]
    y_hg = jnp.concatenate(heads, axis=1) * onorm_ref[...] * (hg * _sigmoid(hg))

    o = rof_ref[...] + rob_ref[...]
    bd = bd_ref[...]
    inv = 1.0 / RW_HEAD_DIM
    cen = o - _group_sum(o, bd) * inv
    var = _group_sum(cen * cen, bd) * inv
    y_rw = ((cen * lax.rsqrt(var + RW_LNX_EPS) * lnxg_ref[...] + lnxb_ref[...] + bonus_ref[...])
            * rgate_ref[...])

    merged = jnp.zeros_like(x)
    for n, y in enumerate((y_hg, ysc_ref[...], yda_ref[...], y_rw)):
        gate = _sigmoid(_dot(h, wg_ref[:, n * d:(n + 1) * d]))
        merged = merged + gate * _dot(y.astype(BF16), bp_ref[n])
    o_ref[...] = x + _dot(merged.astype(BF16), wo_ref[...])


def merge(x2d, g, valid, hg_of, hg_ob, z_hg2d, y_sc, y_da, rw_of, rw_ob, bonus, rw_gate,
          onorm_g, lnx_g, lnx_b, bd_ones, w_gate, branch_proj, w_out, tm):
    m, d = x2d.shape
    bw = y_sc.shape[1]
    row = lambda width: pl.BlockSpec((tm, width), lambda i: (i, 0))
    full = lambda a: pl.BlockSpec(a.shape, lambda i: (0,) * a.ndim)
    hg_gate = pl.BlockSpec((tm, bw), lambda i: (i, 4))
    consts = (onorm_g, lnx_g, lnx_b, bd_ones, w_gate, branch_proj, w_out)
    return pl.pallas_call(
        _merge_kernel,
        grid=(m // tm,),
        in_specs=[row(d), full(g), row(1), row(bw), row(bw), hg_gate] + [row(bw)] * 6
                 + [full(a) for a in consts],
        out_specs=row(d),
        out_shape=jax.ShapeDtypeStruct((m, d), F32),
        compiler_params=_params(("parallel",)),
        name="merge",
    )(x2d, g, valid, hg_of, hg_ob, z_hg2d, y_sc, y_da, rw_of, rw_ob, bonus, rw_gate, *consts)


def _mlp_kernel(x_ref, g_ref, w1_ref, w2_ref, o_ref, h_scr, acc_scr):
    f = pl.program_id(1)

    @pl.when(f == 0)
    def _():
        x = x_ref[...]
        ms = jnp.mean(x * x, axis=-1, keepdims=True)
        h_scr[...] = (x * lax.rsqrt(ms + NORM_EPS) * g_ref[...]).astype(BF16)
        acc_scr[...] = jnp.zeros_like(acc_scr)

    a = jnp.maximum(_dot(h_scr[...], w1_ref[...]), 0.0)
    acc_scr[...] += _dot((a * a).astype(BF16), w2_ref[...])

    @pl.when(f == pl.num_programs(1) - 1)
    def _():
        o_ref[...] = x_ref[...] + acc_scr[...]


def mlp(x2d, g, w1, w2, tm, tf):
    m, d = x2d.shape
    dff = w1.shape[1]
    return pl.pallas_call(
        _mlp_kernel,
        grid=(m // tm, dff // tf),
        in_specs=[pl.BlockSpec((tm, d), lambda i, f: (i, 0)),
                  pl.BlockSpec((1, d), lambda i, f: (0, 0)),
                  pl.BlockSpec((d, tf), lambda i, f: (0, f)),
                  pl.BlockSpec((tf, d), lambda i, f: (f, 0))],
        out_specs=pl.BlockSpec((tm, d), lambda i, f: (i, 0)),
        out_shape=jax.ShapeDtypeStruct((m, d), F32),
        scratch_shapes=[pltpu.VMEM((tm, d), BF16), pltpu.VMEM((tm, d), F32)],
        compiler_params=_params(("parallel", "arbitrary")),
        name="mlp",
    )(x2d, g, w1, w2)


def _rope_tables(p):
    half = ROPE_DIM // 2
    pos = jnp.arange(p, dtype=F32) - FRONT_PAD
    inv = ROPE_THETA ** (-jnp.arange(half, dtype=F32) / half)
    ang = pos[:, None] * inv[None, :]
    cos, sin = jnp.cos(ang), jnp.sin(ang)
    ones = jnp.ones((p, DA_QK_DIM - ROPE_DIM), F32)
    zeros = jnp.zeros((p, DA_QK_DIM - ROPE_DIM), F32)
    zh = jnp.zeros((p, half), F32)
    c = jnp.concatenate([cos, cos, ones], axis=1)
    s1 = jnp.concatenate([-sin, zh, zeros], axis=1)
    s2 = jnp.concatenate([zh, sin, zeros], axis=1)
    return tuple(jnp.tile(t, (1, 2)) for t in (c, s1, s2))


def _block_diag_ones(n, blk):
    i = jnp.arange(n)
    return ((i[:, None] // blk) == (i[None, :] // blk)).astype(F32)


def kernel(x_prompt, x_sample, meta_tokens, norm_mix_g, w_in, hgrn_lb_logits, hgrn_onorm_g, conv_w,
           diff_qnorm_g, diff_knorm_g, diff_lambda, diff_subln_g, rwkv_mu, rwkv_w0, rwkv_w2, rwkv_a0,
           rwkv_a2, rwkv_g2, rwkv_k_k, rwkv_k_a, rwkv_r_k, rwkv_lnx_g, rwkv_lnx_b, w_gate,
           branch_proj, w_out, norm_mlp_g, mlp_w1, mlp_w2):
    assert x_prompt.shape[1:] == x_sample.shape[1:]
    x = jnp.concatenate([x_prompt, x_sample], axis=0)
    bt, seq, d = x.shape
    depth = w_in.shape[0]
    length = N_META + seq
    p = -(-(FRONT_PAD + length) // V7X_LANES) * V7X_LANES
    m = bt * p
    bw = branch_proj.shape[2]
    rw_cols = rwkv_mu.shape[1]
    sizes = (5 * bw, 3 * bw, 3 * bw, rw_cols)
    offs = [0]
    for s in sizes:
        offs.append(offs[-1] + s)

    meta = jnp.broadcast_to(meta_tokens.astype(x.dtype)[None], (bt, N_META, d))
    xp = jnp.concatenate([jnp.zeros((bt, FRONT_PAD, d), x.dtype), meta, x,
                          jnp.zeros((bt, p - FRONT_PAD - length, d), x.dtype)], axis=1)
    x2d = xp.reshape(m, d)

    rows = jnp.arange(p)
    valid_seq = ((rows >= FRONT_PAD) & (rows < FRONT_PAD + length)).astype(F32)[:, None]
    valid = jnp.tile(valid_seq, (bt, 1))
    key_bias = jnp.where(valid_seq[:, 0] > 0, 0.0, NEG_BIG).astype(F32)[None, :]
    tables = _rope_tables(p)
    bd64_mean = (_block_diag_ones(V7X_LANES, DA_QK_DIM) / DA_QK_DIM).astype(BF16)
    bd64_ones = _block_diag_ones(V7X_LANES, RW_HEAD_DIM).astype(BF16)

    sm = jax.nn.softmax(hgrn_lb_logits.astype(F32), axis=1)
    lb_all = jnp.cumsum(sm, axis=1) - sm[:, :1]

    tm_proj = _pick(m, 1536, 128)
    tm_merge = _pick(m, 256, 128)
    tm_mlp = _pick(m, 768, 128)
    tq = _pick(p, 384, 128)
    tk = _pick(p, 384, 128)
    assert FRONT_PAD <= tk and p - (FRONT_PAD + length) <= tk
    tr = _pick(p, 384, 8)

    half = V7X_LANES // 2
    for l in range(depth):
        g_mix = norm_mix_g[l][None, :]
        w_l = w_in[l].astype(BF16)
        z = []
        for gi, size in enumerate(sizes):
            tn = _pick(size, 640, V7X_LANES)
            z.append(norm_matmul(x2d, g_mix, valid, w_l[:, offs[gi]:offs[gi + 1]], tm_proj, tn)
                     .reshape(bt, p, size))
        z_hg, z_sc, z_da, z_rw = z

        of, ob = hgrn_scan(z_hg, valid_seq, lb_all[:, l],
                           _pick(p // HG_CHUNK, HG_CHUNKS_PER_STEP, 1))

        y_sc = shortconv(z_sc, conv_w[l])

        lam_init = 0.8 - 0.6 * math.exp(-0.3 * l)
        y_da = diff_attention(z_da, tables, bd64_mean,
                              jnp.tile(diff_qnorm_g[l], 2)[None, :], jnp.tile(diff_knorm_g[l], 2)[None, :],
                              diff_lambda[l], diff_subln_g[l][None, :], key_bias, lam_init, tq, tk)

        zpad = jnp.zeros((2, half, bw), F32)
        w2pad = jnp.stack([jnp.concatenate([rwkv_w2[l, 0], zpad[0]], 0),
                           jnp.concatenate([zpad[0], rwkv_w2[l, 1]], 0)])
        a2pad = jnp.stack([jnp.concatenate([rwkv_a2[l, 0], zpad[0]], 0),
                           jnp.concatenate([zpad[0], rwkv_a2[l, 1]], 0)])
        (r, v, nkk, lwf, bef, kdf, lwb, beb, kdb, gate, bonus) = rwkv_prep(
            z_rw, valid_seq, rwkv_mu[l][None, :], rwkv_w0[l], w2pad, rwkv_a0[l], a2pad, rwkv_g2[l],
            rwkv_k_k[l][None, :], rwkv_k_a[l][None, :], rwkv_r_k[l].reshape(1, bw), bd64_ones, tr)
        orf, orb = rwkv_scan(r, v, nkk, lwf, bef, kdf, lwb, beb, kdb,
                             _pick(p // RW_CHUNK, RW_CHUNKS_PER_STEP, 1))

        flat = lambda a: a.reshape(m, a.shape[-1])
        x2d = merge(x2d, g_mix, valid, flat(of), flat(ob), flat(z_hg), flat(y_sc), flat(y_da),
                    flat(orf), flat(orb), flat(bonus), flat(gate), hgrn_onorm_g[l][None, :],
                    rwkv_lnx_g[l][None, :], rwkv_lnx_b[l][None, :], bd64_ones,
                    w_gate[l].astype(BF16), branch_proj[l].astype(BF16), w_out[l].astype(BF16), tm_merge)
        x2d = mlp(x2d, norm_mlp_g[l][None, :], mlp_w1[l].astype(BF16), mlp_w2[l].astype(BF16),
                  tm_mlp, _pick(mlp_w1.shape[2], 512, V7X_LANES))

    y = x2d.reshape(bt, p, d)[:, FRONT_PAD + N_META:FRONT_PAD + length]
    nb = x_prompt.shape[0]
    return (y[:nb], y[nb:])
```

```python
import functools
import math

import jax
import jax.numpy as jnp
from jax import lax
from jax.experimental import pallas as pl
from jax.experimental.pallas import tpu as pltpu

F32 = jnp.float32
BF16 = jnp.bfloat16
HI = lax.Precision.HIGHEST

V7X_LANES = 128
V7X_VMEM_BYTES = 64 * 1024 * 1024
VMEM_LIMIT = V7X_VMEM_BYTES - 8 * 1024 * 1024

N_META = 16
NORM_EPS = 1e-6
N_BRANCH = 4
HG_HEAD_DIM = 128
HG_CHUNK = 64
HG_SUB = 16
LB_FLOOR = 1e-20
SC_KSIZE = 3
DA_HEADS = 4
DA_QK_DIM = 64
DA_V_DIM = 128
ROPE_THETA = 500000.0
ROPE_DIM = DA_QK_DIM // 4
SUBLN_EPS = 1e-5
RW_HEAD_DIM = 64
RW_CHUNK = 64
RW_LNX_EPS = 64e-5
FRONT_PAD = (-N_META) % HG_CHUNK
NEG_BIG = -1e30
LOG2_E = 1.4426950408889634

P_HG = "bf16"
P_RW_LR = "bf16"
SCAN_CHUNKS_PER_STEP = 2


def _split_bf16(a):
    hi = a.astype(BF16)
    return hi, (a - hi.astype(F32)).astype(BF16)


def _mm(a, b, dims, precision):
    dg = functools.partial(lax.dot_general, dimension_numbers=(dims, ((), ())),
                           preferred_element_type=F32)
    if precision == "bf16":
        return dg(a.astype(BF16), b.astype(BF16))
    if precision == "x3":
        ah, al = _split_bf16(a)
        bh, bl = _split_bf16(b)
        return dg(ah, bh) + (dg(al, bh) + dg(ah, bl))
    return dg(a, b, precision=precision)


def _dot(a, b, precision=None):
    return _mm(a, b, ((1,), (0,)), precision)


def _dot_split_rhs3(a_bf16, b):
    b1 = b.astype(BF16)
    r1 = b - b1.astype(F32)
    b2 = r1.astype(BF16)
    b3 = (r1 - b2.astype(F32)).astype(BF16)
    return _dot(a_bf16, b1) + (_dot(a_bf16, b2) + _dot(a_bf16, b3))


def _dot_split_lhs(a, b_bf16):
    hi, lo = _split_bf16(a)
    return _dot(hi, b_bf16) + _dot(lo, b_bf16)


def _dot_nt(a, b, precision=None):
    return _mm(a, b, ((1,), (1,)), precision)


def _dot_tn(a, b, precision=None):
    return _mm(a, b, ((0,), (0,)), precision)


def _pick(n, target, mult):
    best = None
    for d in range(mult, min(n, target) + 1, mult):
        if n % d == 0:
            best = d
    assert best is not None, (n, target, mult)
    return best


def _params(sem):
    return pltpu.CompilerParams(dimension_semantics=sem, vmem_limit_bytes=VMEM_LIMIT)


def _log_sigmoid(x):
    return jnp.minimum(x, 0.0) - jnp.log1p(jnp.exp(-jnp.abs(x)))


def _sigmoid(x):
    return 1.0 / (1.0 + jnp.exp(-x))


def _masked_rms_norm(x, g, valid):
    ms = jnp.mean(x * x, axis=-1, keepdims=True)
    h = x * lax.rsqrt(ms + NORM_EPS) * g
    return jnp.where(valid > 0.0, h, 0.0)


def _norm_matmul_kernel(x_ref, g_ref, valid_ref, w_ref, o_ref, h_scr):
    @pl.when(pl.program_id(1) == 0)
    def _():
        h_scr[...] = _masked_rms_norm(x_ref[...], g_ref[...], valid_ref[...]).astype(BF16)

    o_ref[...] = _dot(h_scr[...], w_ref[...])


def norm_matmul(x2d, g, valid, w_bf16, tm, tn):
    m, d = x2d.shape
    n = w_bf16.shape[1]
    return pl.pallas_call(
        _norm_matmul_kernel,
        grid=(m // tm, n // tn),
        in_specs=[
            pl.BlockSpec((tm, d), lambda i, j: (i, 0)),
            pl.BlockSpec((1, d), lambda i, j: (0, 0)),
            pl.BlockSpec((tm, 1), lambda i, j: (i, 0)),
            pl.BlockSpec((d, tn), lambda i, j: (0, j)),
        ],
        out_specs=pl.BlockSpec((tm, tn), lambda i, j: (i, j)),
        out_shape=jax.ShapeDtypeStruct((m, n), F32),
        scratch_shapes=[pltpu.VMEM((tm, d), BF16)],
        compiler_params=_params(("parallel", "arbitrary")),
        name="norm_matmul",
    )(x2d, g, valid, w_bf16)


def _shortconv_kernel(b_ref, c_ref, h_ref, w_ref, o_ref):
    u = c_ref[0] * h_ref[0]
    p = u.shape[0]
    w = w_ref[...]
    y = pltpu.roll(u, 1, 0) * w[0:1] + u * w[1:2] + pltpu.roll(u, p - 1, 0) * w[2:3]
    o_ref[0] = b_ref[0] * y


def shortconv(z_sc, conv_w):
    bt, p, w3 = z_sc.shape
    width = w3 // 3
    nb = width // V7X_LANES
    blk = (1, p, V7X_LANES)
    return pl.pallas_call(
        _shortconv_kernel,
        grid=(bt, nb),
        in_specs=[
            pl.BlockSpec(blk, lambda b, c: (b, 0, c)),
            pl.BlockSpec(blk, lambda b, c: (b, 0, nb + c)),
            pl.BlockSpec(blk, lambda b, c: (b, 0, 2 * nb + c)),
            pl.BlockSpec((SC_KSIZE, V7X_LANES), lambda b, c: (0, c)),
        ],
        out_specs=pl.BlockSpec(blk, lambda b, c: (b, 0, c)),
        out_shape=jax.ShapeDtypeStruct((bt, p, width), F32),
        compiler_params=_params(("parallel", "parallel")),
        name="shortconv",
    )(z_sc, z_sc, z_sc, conv_w)


def _attn_kernel(q_ref, k_ref, v_ref, cq_ref, s1q_ref, s2q_ref, ck_ref, s1k_ref, s2k_ref,
                 bd_ref, qg_ref, kg_ref, lam_ref, sg_ref, bias_ref, o_ref, k_scr, v_scr, s_scr,
                 *, lam_init, tk):
    bd = bd_ref[...]

    def norm_rope(x, g, c, s1, s2):
        ms = _dot_split_lhs(x * x, bd)
        y = x * lax.rsqrt(ms + NORM_EPS) * g
        return (y * c + pltpu.roll(y, V7X_LANES - ROPE_DIM // 2, 1) * s1
                + pltpu.roll(y, ROPE_DIM // 2, 1) * s2)

    @pl.when(pl.program_id(2) == 0)
    def _():
        k = norm_rope(k_ref[0], kg_ref[...], ck_ref[...], s1k_ref[...], s2k_ref[...])
        k_scr[...] = k.astype(BF16)
        v_scr[...] = v_ref[0].astype(BF16)

    q = norm_rope(q_ref[0], qg_ref[...], cq_ref[...], s1q_ref[...], s2q_ref[...])
    q = q * (DA_QK_DIM ** -0.5 * LOG2_E)
    tq = q.shape[0]
    first = lax.broadcasted_iota(jnp.int32, q.shape, 1) < DA_QK_DIM
    lp = lam_ref[...]
    lam = (jnp.exp(jnp.sum(lp[0:1] * lp[1:2], axis=-1, keepdims=True))
           - jnp.exp(jnp.sum(lp[2:3] * lp[3:4], axis=-1, keepdims=True)) + lam_init)
    qs = jnp.concatenate([jnp.where(first, q, 0.0), jnp.where(first, 0.0, q)], axis=0).astype(BF16)
    nk = k_scr.shape[0] // tk
    lanes = range(0, tk, V7X_LANES)
    m128 = None
    for j in range(nk):
        cols = slice(j * tk, (j + 1) * tk)
        s_j = _dot_nt(qs, k_scr[cols, :])
        if j in (0, nk - 1):
            s_j = s_j + bias_ref[:, cols]
        s_scr[:, cols] = s_j
        mt = functools.reduce(jnp.maximum, [s_j[:, c:c + V7X_LANES] for c in lanes])
        m128 = mt if m128 is None else jnp.maximum(m128, mt)
    m = jnp.max(m128, axis=-1, keepdims=True)
    l128 = jnp.zeros_like(m128)
    o2 = jnp.zeros((2 * tq, DA_V_DIM), F32)
    for j in range(nk):
        cols = slice(j * tk, (j + 1) * tk)
        pr = jnp.exp2(s_scr[:, cols] - m)
        l128 = l128 + functools.reduce(jnp.add, [pr[:, c:c + V7X_LANES] for c in lanes])
        o2 = o2 + _dot(pr.astype(BF16), v_scr[cols, :])
    inv = 1.0 / jnp.sum(l128, axis=-1, keepdims=True)
    o = o2[:tq] * inv[:tq] - o2[tq:] * (lam * inv[tq:])
    ms = jnp.mean(o * o, axis=-1, keepdims=True)
    o_ref[0] = o * lax.rsqrt(ms + SUBLN_EPS) * sg_ref[...] * (1.0 - lam_init)


def diff_attention(z_da, tables, bd64, qn_g, kn_g, lam_p, subln_g, key_bias, lam_init, tq, tk):
    bt, p, _ = z_da.shape
    c_t, s1_t, s2_t = tables
    nh = DA_HEADS
    qspec = pl.BlockSpec((1, tq, V7X_LANES), lambda b, h, i: (b, i, h))
    kspec = pl.BlockSpec((1, p, V7X_LANES), lambda b, h, i: (b, 0, nh + h))
    vspec = pl.BlockSpec((1, p, V7X_LANES), lambda b, h, i: (b, 0, 2 * nh + h))
    tq_spec = pl.BlockSpec((tq, V7X_LANES), lambda b, h, i: (i, 0))
    tk_spec = pl.BlockSpec((p, V7X_LANES), lambda b, h, i: (0, 0))
    full = lambda shape: pl.BlockSpec(shape, lambda b, h, i: (0,) * len(shape))
    return pl.pallas_call(
        functools.partial(_attn_kernel, lam_init=lam_init, tk=tk),
        grid=(bt, nh, p // tq),
        in_specs=[qspec, kspec, vspec, tq_spec, tq_spec, tq_spec, tk_spec, tk_spec, tk_spec,
                  full((V7X_LANES, V7X_LANES)), full((1, V7X_LANES)), full((1, V7X_LANES)),
                  full((4, DA_QK_DIM)), full((1, V7X_LANES)), full((1, p))],
        out_specs=pl.BlockSpec((1, tq, V7X_LANES), lambda b, h, i: (b, i, h)),
        out_shape=jax.ShapeDtypeStruct((bt, p, nh * DA_V_DIM), F32),
        scratch_shapes=[pltpu.VMEM((p, V7X_LANES), BF16), pltpu.VMEM((p, V7X_LANES), BF16),
                        pltpu.VMEM((2 * tq, p), F32)],
        compiler_params=_params(("parallel", "parallel", "arbitrary")),
        name="diff_attention",
    )(z_da, z_da, z_da, c_t, s1_t, s2_t, c_t, s1_t, s2_t, bd64, qn_g, kn_g, lam_p, subln_g,
      key_bias)


def _gla_local(q, qt, qh, k, kh, v, b, bpiv, reverse):
    c, sub, half = HG_CHUNK, HG_SUB, HG_SUB // 2
    nsub = c // sub
    row = lax.broadcasted_iota(jnp.int32, (c, 1), 0)
    attn = jnp.zeros((c, c), F32)
    for i in range(nsub):
        if (reverse and i == nsub - 1) or (not reverse and i == 0):
            continue
        bi = bpiv[i * sub:i * sub + 1]
        kmask = (row >= (i + 1) * sub) if reverse else (row < i * sub)
        kt = jnp.where(kmask, k * jnp.exp(jnp.minimum(bi - b, 0.0)), 0.0)
        attn = jnp.where(row // sub == i, _dot_nt(qt, kt, P_HG), attn)
    t_idx = lax.broadcasted_iota(jnp.int32, (c, c), 0)
    s_idx = lax.broadcasted_iota(jnp.int32, (c, c), 1)
    attn = jnp.where(t_idx // sub == s_idx // sub, _dot_nt(qh, kh, P_HG), attn)

    b2 = b * LOG2_E
    tloc = lax.broadcasted_iota(jnp.int32, (half, c), 0)
    scol = lax.broadcasted_iota(jnp.int32, (half, c), 1)
    blocks = []
    for i in range(c // half):
        bt, qi = b2[i * half:(i + 1) * half], q[i * half:(i + 1) * half]
        blk = attn[i * half:(i + 1) * half]
        for j in range(half):
            s = i * half + j
            a = jnp.sum(qi * k[s:s + 1] * jnp.exp2(bt - b2[s:s + 1]), axis=-1, keepdims=True)
            keep = (scol == s) & ((tloc <= j) if reverse else (tloc >= j))
            blk = jnp.where(keep, a, blk)
        blocks.append(blk)
    return _dot(jnp.concatenate(blocks, axis=0), v, P_HG)


def _hgrn_body(qf_ref, ff_ref, vf_ref, mf_ref, qb_ref, fb_ref, vb_ref, mb_ref, lb_ref,
               of_ref, ob_ref, st_scr, *, nc):
    c, sub, half = HG_CHUNK, HG_SUB, HG_SUB // 2
    nh = st_scr.shape[1]
    t_idx = lax.broadcasted_iota(jnp.int32, (c, c), 0)
    s_idx = lax.broadcasted_iota(jnp.int32, (c, c), 1)
    second = (lax.broadcasted_iota(jnp.int32, (c, 1), 0) % sub) >= half
    dirs = ((qf_ref, ff_ref, vf_ref, mf_ref, of_ref), (qb_ref, fb_ref, vb_ref, mb_ref, ob_ref))
    units = []
    for d, (q_ref, f_ref, v_ref, m_ref, _) in enumerate(dirs):
        reverse = d == 1
        if reverse:
            mats = (s_idx >= t_idx, s_idx >= (t_idx // sub + 1) * sub,
                    s_idx >= (t_idx // half + 1) * half, s_idx >= (t_idx // half) * half)
        else:
            mats = (s_idx <= t_idx, s_idx < (t_idx // sub) * sub,
                    s_idx < (t_idx // half) * half, s_idx < (t_idx // half + 1) * half)
        cmat = jnp.concatenate([x.astype(F32) for x in mats], axis=0).astype(BF16)
        q_half = jnp.logical_not(second) if reverse else second
        lb = lb_ref[d:d + 1, :]
        la = jnp.log(jnp.maximum(lb, LB_FLOOR))
        l1 = jnp.log1p(-lb)
        for j in range(nc):
            rows = slice(j * c, (j + 1) * c)
            valid = m_ref[rows, :] > 0.0
            lc = l1 + _log_sigmoid(f_ref[0, rows, :])
            logf = jnp.maximum(la, lc) + jnp.log1p(jnp.exp(-jnp.abs(la - lc)))
            logf = jnp.where(valid, logf, 0.0)
            k = jnp.where(valid, 1.0 - jnp.exp(logf), 0.0)
            cum = _dot_split_rhs3(cmat, logf)
            b, bpiv, bhin, bhout = (cum[i * c:(i + 1) * c] for i in range(4))
            btot = b[0:1] if reverse else b[c - 1:c]
            q, v = q_ref[0, rows, :], v_ref[0, rows, :]
            qt = q * jnp.exp(b - bpiv)
            qh = jnp.where(q_half, q * jnp.exp(b - bhin), 0.0)
            kh = jnp.where(q_half, 0.0, k * jnp.exp(bhout - b))
            qe = (q * jnp.exp(b)).astype(BF16)
            k2 = k * jnp.exp(btot - b)
            dec = jnp.exp(btot)
            for h in range(nh):
                hs = slice(h * HG_HEAD_DIM, (h + 1) * HG_HEAD_DIM)
                units.append(dict(
                    d=d, j=j, h=h, rows=rows, hs=hs, qe=qe[:, hs], dec=dec[:, hs],
                    o=_gla_local(q[:, hs], qt[:, hs], qh[:, hs], k[:, hs], kh[:, hs], v[:, hs],
                                 b[:, hs], bpiv[:, hs], reverse),
                    kv=_dot_tn(v[:, hs], k2[:, hs], P_HG)))
                yield

    state = {(d, h): st_scr[d, h] for d in range(2) for h in range(nh)}
    for step in range(nc):
        for u in units:
            if u["j"] != (nc - 1 - step if u["d"] == 1 else step):
                continue
            key = (u["d"], u["h"])
            dirs[u["d"]][4][0, u["rows"], u["hs"]] = u["o"] + _dot_nt(u["qe"], state[key].astype(BF16))
            state[key] = state[key] * u["dec"] + u["kv"]
        yield
    for (d, h), s in state.items():
        st_scr[d, h] = s


def _group_sum(x, bd):
    cols = [_dot_split_lhs(x[:, j:j + V7X_LANES], bd) for j in range(0, x.shape[1], V7X_LANES)]
    return jnp.concatenate(cols, axis=1)


def _rwkv_prep_kernel(z_ref, zp_ref, zn_ref, valid_ref, mu_ref, w0_ref, w2_ref, a0_ref, a2_ref,
                      g2_ref, kk_ref, ka_ref, rk_ref, bd_ref,
                      r_out, v_out, nkk_out, lwf_out, bef_out, kdf_out, lwb_out, beb_out, kdb_out,
                      gate_out, bonus_out):
    u = z_ref[0]
    tr = u.shape[0]
    width = r_out.shape[-1]
    row = lax.broadcasted_iota(jnp.int32, (tr, 1), 0)
    u_prev = jnp.where(row == 0, zp_ref[0][7:8], pltpu.roll(u, 1, 0))
    u_next = jnp.where(row == tr - 1, zn_ref[0][0:1], pltpu.roll(u, tr - 1, 0))
    xm = u + mu_ref[...] * (0.5 * (u_prev + u_next) - u)
    valid = valid_ref[...] > 0.0
    bd = bd_ref[...]
    r = xm[:, 0:width]
    k = xm[:, width:2 * width]
    v = jnp.where(valid, xm[:, 2 * width:3 * width], 0.0)
    lr = 3 * width
    wl = jnp.tanh(xm[:, lr:lr + V7X_LANES])
    al = xm[:, lr + V7X_LANES:lr + 2 * V7X_LANES]
    gl = xm[:, lr + 2 * V7X_LANES:lr + 3 * V7X_LANES]
    kk = k * kk_ref[...]
    kk = kk / jnp.maximum(jnp.sqrt(_group_sum(kk * kk, bd)), 1e-12)
    kk = jnp.where(valid, kk, 0.0)
    r_out[0] = r
    v_out[0] = v
    nkk_out[0] = -kk
    kd_sum = jnp.zeros_like(k)
    for d, (lw_out, be_out, kd_out) in enumerate(((lwf_out, bef_out, kdf_out),
                                                   (lwb_out, beb_out, kdb_out))):
        wlog = _log_sigmoid(w0_ref[d:d + 1, :] + _dot(wl, w2_ref[d], P_RW_LR)) - 0.5
        lw_out[0] = jnp.where(valid, -jnp.exp(wlog), 0.0)
        a = _sigmoid(a0_ref[d:d + 1, :] + _dot(al, a2_ref[d], P_RW_LR))
        kd = k * (1.0 + (a - 1.0) * ka_ref[...])
        kd_sum = kd_sum + kd
        kd_out[0] = jnp.where(valid, kd, 0.0)
        be_out[0] = kk * a
    gate_out[0] = _dot(_sigmoid(gl), g2_ref[...], P_RW_LR)
    bonus_out[0] = _group_sum(r * kd_sum * rk_ref[...], bd) * v


def rwkv_prep(z_rw, valid2d, mu, w0, w2pad, a0, a2pad, g2, k_k, k_a, r_k, bd_ones, tr):
    bt, p, cols = z_rw.shape
    width = w0.shape[1]
    nt = p // tr
    r8 = tr // 8
    last8 = p // 8 - 1
    full = lambda a: pl.BlockSpec(a.shape, lambda b, i: (0,) * a.ndim)
    out = jax.ShapeDtypeStruct((bt, p, width), F32)
    ospec = pl.BlockSpec((1, tr, width), lambda b, i: (b, i, 0))
    params = (mu, w0, w2pad, a0, a2pad, g2, k_k, k_a, r_k, bd_ones)
    return pl.pallas_call(
        _rwkv_prep_kernel,
        grid=(bt, nt),
        in_specs=[pl.BlockSpec((1, tr, cols), lambda b, i: (b, i, 0)),
                  pl.BlockSpec((1, 8, cols), lambda b, i: (b, jnp.maximum(i * r8 - 1, 0), 0)),
                  pl.BlockSpec((1, 8, cols), lambda b, i: (b, jnp.minimum((i + 1) * r8, last8), 0)),
                  pl.BlockSpec((tr, 1), lambda b, i: (i, 0))] + [full(a) for a in params],
        out_specs=[ospec] * 11,
        out_shape=[out] * 11,
        compiler_params=_params(("parallel", "parallel")),
        name="rwkv_prep",
    )(z_rw, z_rw, z_rw, valid2d, *params)


def _tri_inverse_all(l_list, eye):
    n = range(len(l_list))
    m = l_list[0].shape[0]
    l_hi_lo = [_split_bf16(l) for l in l_list]
    pw = [hl[0] for hl in l_hi_lo]
    tinv = [eye + l for l in l_list]
    levels = int(math.log2(RW_CHUNK))
    nxt = [_dot(pw[u], pw[u]).astype(BF16) for u in n]
    yield
    for j in range(1, levels):
        pw = nxt
        if j < levels - 1:
            both = [_dot(jnp.concatenate([pw[u], tinv[u].astype(BF16)], axis=0), pw[u]) for u in n]
            nxt = [both[u][:m].astype(BF16) for u in n]
            tinv = [tinv[u] + both[u][m:] for u in n]
        else:
            tinv = [tinv[u] + _dot(tinv[u].astype(BF16), pw[u]) for u in n]
        yield
    t_hi_lo = [_split_bf16(t) for t in tinv]
    lt = [_dot(jnp.concatenate(l_hi_lo[u], axis=0), t_hi_lo[u][0]) for u in n]
    yield
    resid = [(eye - tinv[u]) + (lt[u][:m] + (lt[u][m:] + _dot(l_hi_lo[u][0], t_hi_lo[u][1])))
             for u in n]
    yield
    return [tinv[u] + _dot(t_hi_lo[u][0], resid[u].astype(BF16)) for u in n]


def _rwkv_body(rf_ref, vf_ref, af_ref, lwf_ref, bef_ref, kdf_ref,
               rb_ref, vb_ref, ab_ref, lwb_ref, beb_ref, kdb_ref,
               of_ref, ob_ref, h_scr, *, nc):
    c = RW_CHUNK
    n2 = 2 * c
    npair = h_scr.shape[1]
    t_idx = lax.broadcasted_iota(jnp.int32, (c, c), 0)
    s_idx = lax.broadcasted_iota(jnp.int32, (c, c), 1)
    ti = lax.broadcasted_iota(jnp.int32, (n2, n2), 0)
    si = lax.broadcasted_iota(jnp.int32, (n2, n2), 1)
    same = (ti // c) == (si // c)
    tl, sl = ti % c, si % c
    eye = jnp.where(ti == si, 1.0, 0.0)
    first = lax.broadcasted_iota(jnp.int32, (c, V7X_LANES), 1) < RW_HEAD_DIM

    def stack(x):
        return jnp.concatenate([jnp.where(first, x, 0.0), jnp.where(first, 0.0, x)], axis=0)

    dirs = ((rf_ref, vf_ref, af_ref, lwf_ref, bef_ref, kdf_ref, of_ref),
            (rb_ref, vb_ref, ab_ref, lwb_ref, beb_ref, kdb_ref, ob_ref))
    masks = []
    for reverse in (False, True):
        masks.append((((s_idx >= t_idx) if reverse else (s_idx <= t_idx)).astype(F32).astype(BF16),
                      same & ((sl > tl) if reverse else (sl < tl)),
                      same & ((sl >= tl) if reverse else (sl <= tl))))

    units = []
    for step in range(nc):
        for d, (r_ref, v_ref, a_ref, lw_ref, be_ref, kd_ref, _) in enumerate(dirs):
            reverse = d == 1
            tri, strict, incl = masks[d]
            j = nc - 1 - step if reverse else step
            rows = slice(j * c, (j + 1) * c)
            lw = lw_ref[0, rows, :]
            lg = _dot_split_rhs3(tri, lw)
            ltot = lg[0:1] if reverse else lg[c - 1:c]
            ginv = jnp.exp(-lg)
            gend = jnp.exp(ltot - lg)
            be, kd = be_ref[0, rows, :], kd_ref[0, rows, :]
            cols = (a_ref[0, rows, :] * jnp.exp(lg - lw), r_ref[0, rows, :] * jnp.exp(lg),
                    be * ginv, kd * ginv, be * gend, kd * gend, v_ref[0, rows, :])
            gtot = jnp.exp(ltot)
            for p in range(npair):
                ps = slice(p * V7X_LANES, (p + 1) * V7X_LANES)
                a_s, r_s, b_s, k_s, bh_s, kh_s, v_s = (stack(x[:, ps]).astype(BF16) for x in cols)
                units.append(dict(step=step, d=d, p=p, rows=rows, ps=ps, strict=strict, incl=incl,
                                  ar=jnp.concatenate([a_s, r_s], axis=0),
                                  bk=jnp.concatenate([b_s, k_s], axis=0),
                                  bh=bh_s, kh=kh_s, v=v_s, gtot=gtot[:, ps]))
            yield
    for u in units:
        sc = _dot_nt(u["ar"], u["bk"])
        u["l_ab"] = jnp.where(u["strict"], sc[:n2, :n2], 0.0)
        u["l_ak"] = jnp.where(u["strict"], sc[:n2, n2:], 0.0).astype(BF16)
        u["m_rb"] = jnp.where(u["incl"], sc[n2:, :n2], 0.0).astype(BF16)
        u["m_rk"] = jnp.where(u["incl"], sc[n2:, n2:], 0.0).astype(BF16)
    yield
    tinvs = yield from _tri_inverse_all([u["l_ab"] for u in units], eye)
    for u, tinv in zip(units, tinvs):
        u["tinv"] = tinv.astype(BF16)
    yield
    for u in units:
        both = _dot(jnp.concatenate([u["l_ak"], u["m_rk"]], axis=0), u["v"])
        u["lakv"], u["mv"] = both[:n2], both[n2:]
        u["kv"] = _dot_tn(u["v"], u["kh"])
    yield

    state = {(d, p): h_scr[d, p] for d in range(2) for p in range(npair)}
    for step in range(nc):
        live = [u for u in units if u["step"] == step]
        for u in live:
            u["ah"] = _dot_nt(u["ar"], state[u["d"], u["p"]].astype(BF16))
        yield
        for u in live:
            u["u"] = _dot(u["tinv"], (u["ah"][:n2] + u["lakv"]).astype(BF16)).astype(BF16)
        yield
        for u in live:
            o_s = u["ah"][n2:] + u["mv"] + _dot(u["m_rb"], u["u"])
            dirs[u["d"]][6][0, u["rows"], u["ps"]] = o_s[:c] + o_s[c:]
            key = (u["d"], u["p"])
            state[key] = state[key] * u["gtot"] + u["kv"] + _dot_tn(u["u"], u["bh"])
        yield
    for (d, p), h in state.items():
        h_scr[d, p] = h


N_HGRN_IN = 9
N_RWKV_IN = 12


def _scans_kernel(*refs, nc):
    hg_in = refs[:N_HGRN_IN]
    rw_in = refs[N_HGRN_IN:N_HGRN_IN + N_RWKV_IN]
    hg_of, hg_ob, rw_of, rw_ob, st_scr, h_scr = refs[N_HGRN_IN + N_RWKV_IN:]

    @pl.when(pl.program_id(1) == 0)
    def _():
        st_scr[...] = jnp.zeros_like(st_scr)
        h_scr[...] = jnp.zeros_like(h_scr)

    streams = [_rwkv_body(*rw_in, rw_of, rw_ob, h_scr, nc=nc),
               _hgrn_body(*hg_in, hg_of, hg_ob, st_scr, nc=nc)]
    while streams:
        for g in list(streams):
            if next(g, StopIteration) is StopIteration:
                streams.remove(g)


def scans(z_hg, valid_seq, lb, r, v, nkk, lw_f, be_f, kd_f, lw_b, be_b, kd_b, nc):
    assert HG_CHUNK == RW_CHUNK
    bt, p, width = r.shape
    nh = width // HG_HEAD_DIM
    rows = nc * HG_CHUNK
    n = p // rows
    blk = (1, rows, width)

    def spec(group, rev):
        if rev:
            return pl.BlockSpec(blk, lambda b, i: (b, n - 1 - i, group))
        return pl.BlockSpec(blk, lambda b, i: (b, i, group))

    fwd, bwd = spec(0, False), spec(0, True)
    mf = pl.BlockSpec((rows, 1), lambda b, i: (i, 0))
    mb = pl.BlockSpec((rows, 1), lambda b, i: (n - 1 - i, 0))
    out = jax.ShapeDtypeStruct((bt, p, width), F32)
    hg_specs = [spec(0, False), spec(1, False), spec(3, False), mf,
                spec(0, True), spec(2, True), spec(3, True), mb,
                pl.BlockSpec((2, width), lambda b, i: (0, 0))]
    assert len(hg_specs) == N_HGRN_IN
    return pl.pallas_call(
        functools.partial(_scans_kernel, nc=nc),
        grid=(bt, n),
        in_specs=hg_specs + [fwd] * (N_RWKV_IN // 2) + [bwd] * (N_RWKV_IN // 2),
        out_specs=[fwd, bwd, fwd, bwd],
        out_shape=[out] * 4,
        scratch_shapes=[pltpu.VMEM((2, nh, HG_HEAD_DIM, HG_HEAD_DIM), F32),
                        pltpu.VMEM((2, width // V7X_LANES, V7X_LANES, V7X_LANES), F32)],
        compiler_params=_params(("parallel", "arbitrary")),
        name="scans",
    )(z_hg, z_hg, z_hg, valid_seq, z_hg, z_hg, z_hg, valid_seq, lb,
      r, v, nkk, lw_f, be_f, kd_f, r, v, nkk, lw_b, be_b, kd_b)


def _merge_kernel(x_ref, g_ref, valid_ref, hof_ref, hob_ref, hg_ref, ysc_ref, yda_ref,
                  rof_ref, rob_ref, bonus_ref, rgate_ref, onorm_ref, lnxg_ref, lnxb_ref, bd_ref,
                  wg_ref, bp_ref, wo_ref, o_ref):
    x = x_ref[...]
    d = x.shape[1]
    h = _masked_rms_norm(x, g_ref[...], valid_ref[...]).astype(BF16)

    o = hof_ref[...] + hob_ref[...]
    heads = []
    for c in range(0, o.shape[1], HG_HEAD_DIM):
        oh = o[:, c:c + HG_HEAD_DIM]
        heads.append(oh * lax.rsqrt(jnp.mean(oh * oh, axis=-1, keepdims=True) + NORM_EPS))
    hg = hg_ref[...]
    y_hg = jnp.concatenate(heads, axis=1) * onorm_ref[...] * (hg * _sigmoid(hg))

    o = rof_ref[...] + rob_ref[...]
    bd = bd_ref[...]
    inv = 1.0 / RW_HEAD_DIM
    cen = o - _group_sum(o, bd) * inv
    var = _group_sum(cen * cen, bd) * inv
    y_rw = ((cen * lax.rsqrt(var + RW_LNX_EPS) * lnxg_ref[...] + lnxb_ref[...] + bonus_ref[...])
            * rgate_ref[...])

    merged = jnp.zeros_like(x)
    for n, y in enumerate((y_hg, ysc_ref[...], yda_ref[...], y_rw)):
        gate = _sigmoid(_dot(h, wg_ref[:, n * d:(n + 1) * d]))
        merged = merged + gate * _dot(y.astype(BF16), bp_ref[n])
    o_ref[...] = x + _dot(merged.astype(BF16), wo_ref[...])


def merge(x2d, g, valid, hg_of, hg_ob, z_hg2d, y_sc, y_da, rw_of, rw_ob, bonus, rw_gate,
          onorm_g, lnx_g, lnx_b, bd_ones, w_gate, branch_proj, w_out, tm):
    m, d = x2d.shape
    bw = y_sc.shape[1]
    row = lambda width: pl.BlockSpec((tm, width), lambda i: (i, 0))
    full = lambda a: pl.BlockSpec(a.shape, lambda i: (0,) * a.ndim)
    hg_gate = pl.BlockSpec((tm, bw), lambda i: (i, 4))
    consts = (onorm_g, lnx_g, lnx_b, bd_ones, w_gate, branch_proj, w_out)
    return pl.pallas_call(
        _merge_kernel,
        grid=(m // tm,),
        in_specs=[row(d), full(g), row(1), row(bw), row(bw), hg_gate] + [row(bw)] * 6
                 + [full(a) for a in consts],
        out_specs=row(d),
        out_shape=jax.ShapeDtypeStruct((m, d), F32),
        compiler_params=_params(("parallel",)),
        name="merge",
    )(x2d, g, valid, hg_of, hg_ob, z_hg2d, y_sc, y_da, rw_of, rw_ob, bonus, rw_gate, *consts)


def _mlp_kernel(x_ref, g_ref, w1_ref, w2_ref, o_ref, h_scr, acc_scr):
    f = pl.program_id(1)

    @pl.when(f == 0)
    def _():
        x = x_ref[...]
        ms = jnp.mean(x * x, axis=-1, keepdims=True)
        h_scr[...] = (x * lax.rsqrt(ms + NORM_EPS) * g_ref[...]).astype(BF16)
        acc_scr[...] = jnp.zeros_like(acc_scr)

    a = jnp.maximum(_dot(h_scr[...], w1_ref[...]), 0.0)
    acc_scr[...] += _dot((a * a).astype(BF16), w2_ref[...])

    @pl.when(f == pl.num_programs(1) - 1)
    def _():
        o_ref[...] = x_ref[...] + acc_scr[...]


def mlp(x2d, g, w1, w2, tm, tf):
    m, d = x2d.shape
    dff = w1.shape[1]
    return pl.pallas_call(
        _mlp_kernel,
        grid=(m // tm, dff // tf),
        in_specs=[pl.BlockSpec((tm, d), lambda i, f: (i, 0)),
                  pl.BlockSpec((1, d), lambda i, f: (0, 0)),
                  pl.BlockSpec((d, tf), lambda i, f: (0, f)),
                  pl.BlockSpec((tf, d), lambda i, f: (f, 0))],
        out_specs=pl.BlockSpec((tm, d), lambda i, f: (i, 0)),
        out_shape=jax.ShapeDtypeStruct((m, d), F32),
        scratch_shapes=[pltpu.VMEM((tm, d), BF16), pltpu.VMEM((tm, d), F32)],
        compiler_params=_params(("parallel", "arbitrary")),
        name="mlp",
    )(x2d, g, w1, w2)


def _rope_tables(p):
    half = ROPE_DIM // 2
    pos = jnp.arange(p, dtype=F32) - FRONT_PAD
    inv = ROPE_THETA ** (-jnp.arange(half, dtype=F32) / half)
    ang = pos[:, None] * inv[None, :]
    cos, sin = jnp.cos(ang), jnp.sin(ang)
    ones = jnp.ones((p, DA_QK_DIM - ROPE_DIM), F32)
    zeros = jnp.zeros((p, DA_QK_DIM - ROPE_DIM), F32)
    zh = jnp.zeros((p, half), F32)
    c = jnp.concatenate([cos, cos, ones], axis=1)
    s1 = jnp.concatenate([-sin, zh, zeros], axis=1)
    s2 = jnp.concatenate([zh, sin, zeros], axis=1)
    return tuple(jnp.tile(t, (1, 2)) for t in (c, s1, s2))


def _block_diag_ones(n, blk):
    i = jnp.arange(n)
    return ((i[:, None] // blk) == (i[None, :] // blk)).astype(F32)


def kernel(x_prompt, x_sample, meta_tokens, norm_mix_g, w_in, hgrn_lb_logits, hgrn_onorm_g, conv_w,
           diff_qnorm_g, diff_knorm_g, diff_lambda, diff_subln_g, rwkv_mu, rwkv_w0, rwkv_w2, rwkv_a0,
           rwkv_a2, rwkv_g2, rwkv_k_k, rwkv_k_a, rwkv_r_k, rwkv_lnx_g, rwkv_lnx_b, w_gate,
           branch_proj, w_out, norm_mlp_g, mlp_w1, mlp_w2):
    assert x_prompt.shape[1:] == x_sample.shape[1:]
    x = jnp.concatenate([x_prompt, x_sample], axis=0)
    bt, seq, d = x.shape
    depth = w_in.shape[0]
    length = N_META + seq
    p = -(-(FRONT_PAD + length) // V7X_LANES) * V7X_LANES
    m = bt * p
    bw = branch_proj.shape[2]
    rw_cols = rwkv_mu.shape[1]
    sizes = (5 * bw, 3 * bw, 3 * bw, rw_cols)
    offs = [0]
    for s in sizes:
        offs.append(offs[-1] + s)

    meta = jnp.broadcast_to(meta_tokens.astype(x.dtype)[None], (bt, N_META, d))
    xp = jnp.concatenate([jnp.zeros((bt, FRONT_PAD, d), x.dtype), meta, x,
                          jnp.zeros((bt, p - FRONT_PAD - length, d), x.dtype)], axis=1)
    x2d = xp.reshape(m, d)

    rows = jnp.arange(p)
    valid_seq = ((rows >= FRONT_PAD) & (rows < FRONT_PAD + length)).astype(F32)[:, None]
    valid = jnp.tile(valid_seq, (bt, 1))
    key_bias = jnp.where(valid_seq[:, 0] > 0, 0.0, NEG_BIG).astype(F32)[None, :]
    tables = _rope_tables(p)
    bd64_mean = (_block_diag_ones(V7X_LANES, DA_QK_DIM) / DA_QK_DIM).astype(BF16)
    bd64_ones = _block_diag_ones(V7X_LANES, RW_HEAD_DIM).astype(BF16)

    sm = jax.nn.softmax(hgrn_lb_logits.astype(F32), axis=1)
    lb_all = jnp.cumsum(sm, axis=1) - sm[:, :1]

    tm_proj = _pick(m, 1536, 128)
    tm_merge = _pick(m, 256, 128)
    tm_mlp = _pick(m, 768, 128)
    tq = _pick(p, 384, 128)
    tk = _pick(p, 384, 128)
    assert FRONT_PAD <= tk and p - (FRONT_PAD + length) <= tk
    tr = _pick(p, 384, 8)

    half = V7X_LANES // 2
    for l in range(depth):
        g_mix = norm_mix_g[l][None, :]
        w_l = w_in[l].astype(BF16)
        z = []
        for gi, size in enumerate(sizes):
            tn = _pick(size, 640, V7X_LANES)
            z.append(norm_matmul(x2d, g_mix, valid, w_l[:, offs[gi]:offs[gi + 1]], tm_proj, tn)
                     .reshape(bt, p, size))
        z_hg, z_sc, z_da, z_rw = z

        y_sc = shortconv(z_sc, conv_w[l])

        lam_init = 0.8 - 0.6 * math.exp(-0.3 * l)
        y_da = diff_attention(z_da, tables, bd64_mean,
                              jnp.tile(diff_qnorm_g[l], 2)[None, :], jnp.tile(diff_knorm_g[l], 2)[None, :],
                              diff_lambda[l], diff_subln_g[l][None, :], key_bias, lam_init, tq, tk)

        zpad = jnp.zeros((2, half, bw), F32)
        w2pad = jnp.stack([jnp.concatenate([rwkv_w2[l, 0], zpad[0]], 0),
                           jnp.concatenate([zpad[0], rwkv_w2[l, 1]], 0)])
        a2pad = jnp.stack([jnp.concatenate([rwkv_a2[l, 0], zpad[0]], 0),
                           jnp.concatenate([zpad[0], rwkv_a2[l, 1]], 0)])
        (r, v, nkk, lwf, bef, kdf, lwb, beb, kdb, gate, bonus) = rwkv_prep(
            z_rw, valid_seq, rwkv_mu[l][None, :], rwkv_w0[l], w2pad, rwkv_a0[l], a2pad, rwkv_g2[l],
            rwkv_k_k[l][None, :], rwkv_k_a[l][None, :], rwkv_r_k[l].reshape(1, bw), bd64_ones, tr)
        of, ob, orf, orb = scans(z_hg, valid_seq, lb_all[:, l], r, v, nkk, lwf, bef, kdf, lwb, beb, kdb,
                                 _pick(p // RW_CHUNK, SCAN_CHUNKS_PER_STEP, 1))

        flat = lambda a: a.reshape(m, a.shape[-1])
        x2d = merge(x2d, g_mix, valid, flat(of), flat(ob), flat(z_hg), flat(y_sc), flat(y_da),
                    flat(orf), flat(orb), flat(bonus), flat(gate), hgrn_onorm_g[l][None, :],
                    rwkv_lnx_g[l][None, :], rwkv_lnx_b[l][None, :], bd64_ones,
                    w_gate[l].astype(BF16), branch_proj[l].astype(BF16), w_out[l].astype(BF16), tm_merge)
        x2d = mlp(x2d, norm_mlp_g[l][None, :], mlp_w1[l].astype(BF16), mlp_w2[l].astype(BF16),
                  tm_mlp, _pick(mlp_w1.shape[2], 512, V7X_LANES))

    y = x2d.reshape(bt, p, d)[:, FRONT_PAD + N_META:FRONT_PAD + length]
    nb = x_prompt.shape[0]
    return (y[:nb], y[nb:])
```

```python
import functools
import math

import jax
import jax.numpy as jnp
from jax import lax
from jax.experimental import pallas as pl
from jax.experimental.pallas import tpu as pltpu

F32 = jnp.float32
BF16 = jnp.bfloat16
HI = lax.Precision.HIGHEST

V7X_LANES = 128
V7X_VMEM_BYTES = 64 * 1024 * 1024
VMEM_LIMIT = V7X_VMEM_BYTES - 8 * 1024 * 1024

N_META = 16
NORM_EPS = 1e-6
N_BRANCH = 4
HG_HEAD_DIM = 128
HG_CHUNK = 64
HG_SUB = 16
LB_FLOOR = 1e-20
SC_KSIZE = 3
DA_HEADS = 4
DA_QK_DIM = 64
DA_V_DIM = 128
ROPE_THETA = 500000.0
ROPE_DIM = DA_QK_DIM // 4
SUBLN_EPS = 1e-5
RW_HEAD_DIM = 64
RW_CHUNK = 64
RW_LNX_EPS = 64e-5
FRONT_PAD = (-N_META) % HG_CHUNK
NEG_BIG = -1e30
LOG2_E = 1.4426950408889634

P_HG = "bf16"
P_RW_LR = "bf16"
SCAN_CHUNKS_PER_STEP = 2


def _split_bf16(a):
    hi = a.astype(BF16)
    return hi, (a - hi.astype(F32)).astype(BF16)


def _mm(a, b, dims, precision):
    dg = functools.partial(lax.dot_general, dimension_numbers=(dims, ((), ())),
                           preferred_element_type=F32)
    if precision == "bf16":
        return dg(a.astype(BF16), b.astype(BF16))
    if precision == "x3":
        ah, al = _split_bf16(a)
        bh, bl = _split_bf16(b)
        return dg(ah, bh) + (dg(al, bh) + dg(ah, bl))
    return dg(a, b, precision=precision)


def _dot(a, b, precision=None):
    return _mm(a, b, ((1,), (0,)), precision)


def _dot_split_rhs3(a_bf16, b):
    b1 = b.astype(BF16)
    r1 = b - b1.astype(F32)
    b2 = r1.astype(BF16)
    b3 = (r1 - b2.astype(F32)).astype(BF16)
    return _dot(a_bf16, b1) + (_dot(a_bf16, b2) + _dot(a_bf16, b3))


def _dot_split_lhs(a, b_bf16):
    hi, lo = _split_bf16(a)
    return _dot(hi, b_bf16) + _dot(lo, b_bf16)


def _dot_nt(a, b, precision=None):
    return _mm(a, b, ((1,), (1,)), precision)


def _dot_tn(a, b, precision=None):
    return _mm(a, b, ((0,), (0,)), precision)


def _pick(n, target, mult):
    best = None
    for d in range(mult, min(n, target) + 1, mult):
        if n % d == 0:
            best = d
    assert best is not None, (n, target, mult)
    return best


def _params(sem):
    return pltpu.CompilerParams(dimension_semantics=sem, vmem_limit_bytes=VMEM_LIMIT)


def _log_sigmoid(x):
    return jnp.minimum(x, 0.0) - jnp.log1p(jnp.exp(-jnp.abs(x)))


def _sigmoid(x):
    return 1.0 / (1.0 + jnp.exp(-x))


def _masked_rms_norm(x, g, valid):
    ms = jnp.mean(x * x, axis=-1, keepdims=True)
    h = x * lax.rsqrt(ms + NORM_EPS) * g
    return jnp.where(valid > 0.0, h, 0.0)


def _norm_matmul_kernel(x_ref, g_ref, valid_ref, w_ref, o_ref, h_scr):
    @pl.when(pl.program_id(1) == 0)
    def _():
        h_scr[...] = _masked_rms_norm(x_ref[...], g_ref[...], valid_ref[...]).astype(BF16)

    o_ref[...] = _dot(h_scr[...], w_ref[...])


def norm_matmul(x2d, g, valid, w_bf16, tm, tn):
    m, d = x2d.shape
    n = w_bf16.shape[1]
    return pl.pallas_call(
        _norm_matmul_kernel,
        grid=(m // tm, n // tn),
        in_specs=[
            pl.BlockSpec((tm, d), lambda i, j: (i, 0)),
            pl.BlockSpec((1, d), lambda i, j: (0, 0)),
            pl.BlockSpec((tm, 1), lambda i, j: (i, 0)),
            pl.BlockSpec((d, tn), lambda i, j: (0, j)),
        ],
        out_specs=pl.BlockSpec((tm, tn), lambda i, j: (i, j)),
        out_shape=jax.ShapeDtypeStruct((m, n), F32),
        scratch_shapes=[pltpu.VMEM((tm, d), BF16)],
        compiler_params=_params(("parallel", "arbitrary")),
        name="norm_matmul",
    )(x2d, g, valid, w_bf16)


def _shortconv_kernel(b_ref, c_ref, h_ref, w_ref, o_ref):
    u = c_ref[0] * h_ref[0]
    p = u.shape[0]
    w = w_ref[...]
    y = pltpu.roll(u, 1, 0) * w[0:1] + u * w[1:2] + pltpu.roll(u, p - 1, 0) * w[2:3]
    o_ref[0] = b_ref[0] * y


def shortconv(z, col0, conv_w):
    bt, p, _ = z.shape
    width = conv_w.shape[1]
    nb = width // V7X_LANES
    c0 = col0 // V7X_LANES
    blk = (1, p, V7X_LANES)
    z_sc = z
    return pl.pallas_call(
        _shortconv_kernel,
        grid=(bt, nb),
        in_specs=[
            pl.BlockSpec(blk, lambda b, c: (b, 0, c0 + c)),
            pl.BlockSpec(blk, lambda b, c: (b, 0, c0 + nb + c)),
            pl.BlockSpec(blk, lambda b, c: (b, 0, c0 + 2 * nb + c)),
            pl.BlockSpec((SC_KSIZE, V7X_LANES), lambda b, c: (0, c)),
        ],
        out_specs=pl.BlockSpec(blk, lambda b, c: (b, 0, c)),
        out_shape=jax.ShapeDtypeStruct((bt, p, width), F32),
        compiler_params=_params(("parallel", "parallel")),
        name="shortconv",
    )(z_sc, z_sc, z_sc, conv_w)


def _attn_kernel(q_ref, k_ref, v_ref, cq_ref, s1q_ref, s2q_ref, ck_ref, s1k_ref, s2k_ref,
                 bd_ref, qg_ref, kg_ref, lam_ref, sg_ref, bias_ref, o_ref, k_scr, v_scr, s_scr,
                 *, lam_init, tk):
    bd = bd_ref[...]

    def norm_rope(x, g, c, s1, s2):
        ms = _dot_split_lhs(x * x, bd)
        y = x * lax.rsqrt(ms + NORM_EPS) * g
        return (y * c + pltpu.roll(y, V7X_LANES - ROPE_DIM // 2, 1) * s1
                + pltpu.roll(y, ROPE_DIM // 2, 1) * s2)

    @pl.when(pl.program_id(2) == 0)
    def _():
        k = norm_rope(k_ref[0], kg_ref[...], ck_ref[...], s1k_ref[...], s2k_ref[...])
        k_scr[...] = k.astype(BF16)
        v_scr[...] = v_ref[0].astype(BF16)

    q = norm_rope(q_ref[0], qg_ref[...], cq_ref[...], s1q_ref[...], s2q_ref[...])
    q = q * (DA_QK_DIM ** -0.5 * LOG2_E)
    tq = q.shape[0]
    first = lax.broadcasted_iota(jnp.int32, q.shape, 1) < DA_QK_DIM
    lp = lam_ref[...]
    lam = (jnp.exp(jnp.sum(lp[0:1] * lp[1:2], axis=-1, keepdims=True))
           - jnp.exp(jnp.sum(lp[2:3] * lp[3:4], axis=-1, keepdims=True)) + lam_init)
    qs = jnp.concatenate([jnp.where(first, q, 0.0), jnp.where(first, 0.0, q)], axis=0).astype(BF16)
    nk = k_scr.shape[0] // tk
    lanes = range(0, tk, V7X_LANES)
    m128 = None
    for j in range(nk):
        cols = slice(j * tk, (j + 1) * tk)
        s_j = _dot_nt(qs, k_scr[cols, :])
        if j in (0, nk - 1):
            s_j = s_j + bias_ref[:, cols]
        s_scr[:, cols] = s_j
        mt = functools.reduce(jnp.maximum, [s_j[:, c:c + V7X_LANES] for c in lanes])
        m128 = mt if m128 is None else jnp.maximum(m128, mt)
    m = jnp.max(m128, axis=-1, keepdims=True)
    l128 = jnp.zeros_like(m128)
    o2 = jnp.zeros((2 * tq, DA_V_DIM), F32)
    for j in range(nk):
        cols = slice(j * tk, (j + 1) * tk)
        pr = jnp.exp2(s_scr[:, cols] - m)
        l128 = l128 + functools.reduce(jnp.add, [pr[:, c:c + V7X_LANES] for c in lanes])
        o2 = o2 + _dot(pr.astype(BF16), v_scr[cols, :])
    inv = 1.0 / jnp.sum(l128, axis=-1, keepdims=True)
    o = o2[:tq] * inv[:tq] - o2[tq:] * (lam * inv[tq:])
    ms = jnp.mean(o * o, axis=-1, keepdims=True)
    o_ref[0] = o * lax.rsqrt(ms + SUBLN_EPS) * sg_ref[...] * (1.0 - lam_init)


def diff_attention(z, col0, tables, bd64, qn_g, kn_g, lam_p, subln_g, key_bias, lam_init, tq, tk):
    bt, p, _ = z.shape
    z_da = z
    c_t, s1_t, s2_t = tables
    nh = DA_HEADS
    c0 = col0 // V7X_LANES
    qspec = pl.BlockSpec((1, tq, V7X_LANES), lambda b, h, i: (b, i, c0 + h))
    kspec = pl.BlockSpec((1, p, V7X_LANES), lambda b, h, i: (b, 0, c0 + nh + h))
    vspec = pl.BlockSpec((1, p, V7X_LANES), lambda b, h, i: (b, 0, c0 + 2 * nh + h))
    tq_spec = pl.BlockSpec((tq, V7X_LANES), lambda b, h, i: (i, 0))
    tk_spec = pl.BlockSpec((p, V7X_LANES), lambda b, h, i: (0, 0))
    full = lambda shape: pl.BlockSpec(shape, lambda b, h, i: (0,) * len(shape))
    return pl.pallas_call(
        functools.partial(_attn_kernel, lam_init=lam_init, tk=tk),
        grid=(bt, nh, p // tq),
        in_specs=[qspec, kspec, vspec, tq_spec, tq_spec, tq_spec, tk_spec, tk_spec, tk_spec,
                  full((V7X_LANES, V7X_LANES)), full((1, V7X_LANES)), full((1, V7X_LANES)),
                  full((4, DA_QK_DIM)), full((1, V7X_LANES)), full((1, p))],
        out_specs=pl.BlockSpec((1, tq, V7X_LANES), lambda b, h, i: (b, i, h)),
        out_shape=jax.ShapeDtypeStruct((bt, p, nh * DA_V_DIM), F32),
        scratch_shapes=[pltpu.VMEM((p, V7X_LANES), BF16), pltpu.VMEM((p, V7X_LANES), BF16),
                        pltpu.VMEM((2 * tq, p), F32)],
        compiler_params=_params(("parallel", "parallel", "arbitrary")),
        name="diff_attention",
    )(z_da, z_da, z_da, c_t, s1_t, s2_t, c_t, s1_t, s2_t, bd64, qn_g, kn_g, lam_p, subln_g,
      key_bias)


def _gla_local(q, qt, qh, k, kh, v, b, bpiv, reverse):
    c, sub, half = HG_CHUNK, HG_SUB, HG_SUB // 2
    nsub = c // sub
    row = lax.broadcasted_iota(jnp.int32, (c, 1), 0)
    attn = jnp.zeros((c, c), F32)
    for i in range(nsub):
        if (reverse and i == nsub - 1) or (not reverse and i == 0):
            continue
        bi = bpiv[i * sub:i * sub + 1]
        kmask = (row >= (i + 1) * sub) if reverse else (row < i * sub)
        kt = jnp.where(kmask, k * jnp.exp(jnp.minimum(bi - b, 0.0)), 0.0)
        attn = jnp.where(row // sub == i, _dot_nt(qt, kt, P_HG), attn)
    t_idx = lax.broadcasted_iota(jnp.int32, (c, c), 0)
    s_idx = lax.broadcasted_iota(jnp.int32, (c, c), 1)
    attn = jnp.where(t_idx // sub == s_idx // sub, _dot_nt(qh, kh, P_HG), attn)

    b2 = b * LOG2_E
    tloc = lax.broadcasted_iota(jnp.int32, (half, c), 0)
    scol = lax.broadcasted_iota(jnp.int32, (half, c), 1)
    blocks = []
    for i in range(c // half):
        bt, qi = b2[i * half:(i + 1) * half], q[i * half:(i + 1) * half]
        blk = attn[i * half:(i + 1) * half]
        for j in range(half):
            s = i * half + j
            a = jnp.sum(qi * k[s:s + 1] * jnp.exp2(bt - b2[s:s + 1]), axis=-1, keepdims=True)
            keep = (scol == s) & ((tloc <= j) if reverse else (tloc >= j))
            blk = jnp.where(keep, a, blk)
        blocks.append(blk)
    return _dot(jnp.concatenate(blocks, axis=0), v, P_HG)


def _hgrn_body(qf_ref, ff_ref, vf_ref, mf_ref, qb_ref, fb_ref, vb_ref, mb_ref, lb_ref,
               of_ref, ob_ref, st_scr, *, nc):
    c, sub, half = HG_CHUNK, HG_SUB, HG_SUB // 2
    nh = st_scr.shape[1]
    t_idx = lax.broadcasted_iota(jnp.int32, (c, c), 0)
    s_idx = lax.broadcasted_iota(jnp.int32, (c, c), 1)
    second = (lax.broadcasted_iota(jnp.int32, (c, 1), 0) % sub) >= half
    dirs = ((qf_ref, ff_ref, vf_ref, mf_ref, of_ref), (qb_ref, fb_ref, vb_ref, mb_ref, ob_ref))
    units = []
    for d, (q_ref, f_ref, v_ref, m_ref, _) in enumerate(dirs):
        reverse = d == 1
        if reverse:
            mats = (s_idx >= t_idx, s_idx >= (t_idx // sub + 1) * sub,
                    s_idx >= (t_idx // half + 1) * half, s_idx >= (t_idx // half) * half)
        else:
            mats = (s_idx <= t_idx, s_idx < (t_idx // sub) * sub,
                    s_idx < (t_idx // half) * half, s_idx < (t_idx // half + 1) * half)
        cmat = jnp.concatenate([x.astype(F32) for x in mats], axis=0).astype(BF16)
        q_half = jnp.logical_not(second) if reverse else second
        lb = lb_ref[d:d + 1, :]
        la = jnp.log(jnp.maximum(lb, LB_FLOOR))
        l1 = jnp.log1p(-lb)
        for j in range(nc):
            rows = slice(j * c, (j + 1) * c)
            valid = m_ref[rows, :] > 0.0
            lc = l1 + _log_sigmoid(f_ref[0, rows, :])
            logf = jnp.maximum(la, lc) + jnp.log1p(jnp.exp(-jnp.abs(la - lc)))
            logf = jnp.where(valid, logf, 0.0)
            k = jnp.where(valid, 1.0 - jnp.exp(logf), 0.0)
            cum = _dot_split_rhs3(cmat, logf)
            b, bpiv, bhin, bhout = (cum[i * c:(i + 1) * c] for i in range(4))
            btot = b[0:1] if reverse else b[c - 1:c]
            q, v = q_ref[0, rows, :], v_ref[0, rows, :]
            qt = q * jnp.exp(b - bpiv)
            qh = jnp.where(q_half, q * jnp.exp(b - bhin), 0.0)
            kh = jnp.where(q_half, 0.0, k * jnp.exp(bhout - b))
            qe = (q * jnp.exp(b)).astype(BF16)
            k2 = k * jnp.exp(btot - b)
            dec = jnp.exp(btot)
            for h in range(nh):
                hs = slice(h * HG_HEAD_DIM, (h + 1) * HG_HEAD_DIM)
                units.append(dict(
                    d=d, j=j, h=h, rows=rows, hs=hs, qe=qe[:, hs], dec=dec[:, hs],
                    o=_gla_local(q[:, hs], qt[:, hs], qh[:, hs], k[:, hs], kh[:, hs], v[:, hs],
                                 b[:, hs], bpiv[:, hs], reverse),
                    kv=_dot_tn(v[:, hs], k2[:, hs], P_HG)))
                yield

    state = {(d, h): st_scr[d, h] for d in range(2) for h in range(nh)}
    for step in range(nc):
        for u in units:
            if u["j"] != (nc - 1 - step if u["d"] == 1 else step):
                continue
            key = (u["d"], u["h"])
            dirs[u["d"]][4][0, u["rows"], u["hs"]] = u["o"] + _dot_nt(u["qe"], state[key].astype(BF16))
            state[key] = state[key] * u["dec"] + u["kv"]
        yield
    for (d, h), s in state.items():
        st_scr[d, h] = s


def _group_sum(x, bd):
    cols = [_dot_split_lhs(x[:, j:j + V7X_LANES], bd) for j in range(0, x.shape[1], V7X_LANES)]
    return jnp.concatenate(cols, axis=1)


def _rwkv_prep_kernel(z_ref, zp_ref, zn_ref, valid_ref, mu_ref, w0_ref, w2_ref, a0_ref, a2_ref,
                      g2_ref, kk_ref, ka_ref, rk_ref, bd_ref,
                      r_out, v_out, nkk_out, lwf_out, bef_out, kdf_out, lwb_out, beb_out, kdb_out,
                      gate_out, bonus_out):
    u = z_ref[0]
    tr = u.shape[0]
    width = r_out.shape[-1]
    row = lax.broadcasted_iota(jnp.int32, (tr, 1), 0)
    u_prev = jnp.where(row == 0, zp_ref[0][7:8], pltpu.roll(u, 1, 0))
    u_next = jnp.where(row == tr - 1, zn_ref[0][0:1], pltpu.roll(u, tr - 1, 0))
    xm = u + mu_ref[...] * (0.5 * (u_prev + u_next) - u)
    valid = valid_ref[...] > 0.0
    bd = bd_ref[...]
    r = xm[:, 0:width]
    k = xm[:, width:2 * width]
    v = jnp.where(valid, xm[:, 2 * width:3 * width], 0.0)
    lr = 3 * width
    wl = jnp.tanh(xm[:, lr:lr + V7X_LANES])
    al = xm[:, lr + V7X_LANES:lr + 2 * V7X_LANES]
    gl = xm[:, lr + 2 * V7X_LANES:lr + 3 * V7X_LANES]
    kk = k * kk_ref[...]
    kk = kk / jnp.maximum(jnp.sqrt(_group_sum(kk * kk, bd)), 1e-12)
    kk = jnp.where(valid, kk, 0.0)
    r_out[0] = r.astype(BF16)
    v_out[0] = v.astype(BF16)
    nkk_out[0] = (-kk).astype(BF16)
    kd_sum = jnp.zeros_like(k)
    for d, (lw_out, be_out, kd_out) in enumerate(((lwf_out, bef_out, kdf_out),
                                                   (lwb_out, beb_out, kdb_out))):
        wlog = _log_sigmoid(w0_ref[d:d + 1, :] + _dot(wl, w2_ref[d], P_RW_LR)) - 0.5
        lw_out[0] = jnp.where(valid, -jnp.exp(wlog), 0.0)
        a = _sigmoid(a0_ref[d:d + 1, :] + _dot(al, a2_ref[d], P_RW_LR))
        kd = k * (1.0 + (a - 1.0) * ka_ref[...])
        kd_sum = kd_sum + kd
        kd_out[0] = jnp.where(valid, kd, 0.0).astype(BF16)
        be_out[0] = (kk * a).astype(BF16)
    gate_out[0] = _dot(_sigmoid(gl), g2_ref[...], P_RW_LR).astype(BF16)
    bonus_out[0] = (_group_sum(r * kd_sum * rk_ref[...], bd) * v).astype(BF16)


def rwkv_prep(z_rw, valid2d, mu, w0, w2pad, a0, a2pad, g2, k_k, k_a, r_k, bd_ones, tr):
    bt, p, _ = z_rw.shape
    cols = mu.shape[1]
    width = w0.shape[1]
    nt = p // tr
    r8 = tr // 8
    last8 = p // 8 - 1
    full = lambda a: pl.BlockSpec(a.shape, lambda b, i: (0,) * a.ndim)
    outs = [jax.ShapeDtypeStruct((bt, p, width), F32 if i in (3, 6) else BF16) for i in range(11)]
    ospec = pl.BlockSpec((1, tr, width), lambda b, i: (b, i, 0))
    params = (mu, w0, w2pad, a0, a2pad, g2, k_k, k_a, r_k, bd_ones)
    return pl.pallas_call(
        _rwkv_prep_kernel,
        grid=(bt, nt),
        in_specs=[pl.BlockSpec((1, tr, cols), lambda b, i: (b, i, 0)),
                  pl.BlockSpec((1, 8, cols), lambda b, i: (b, jnp.maximum(i * r8 - 1, 0), 0)),
                  pl.BlockSpec((1, 8, cols), lambda b, i: (b, jnp.minimum((i + 1) * r8, last8), 0)),
                  pl.BlockSpec((tr, 1), lambda b, i: (i, 0))] + [full(a) for a in params],
        out_specs=[ospec] * 11,
        out_shape=outs,
        compiler_params=_params(("parallel", "parallel")),
        name="rwkv_prep",
    )(z_rw, z_rw, z_rw, valid2d, *params)


def _tri_inverse_all(l_list, eye):
    n = range(len(l_list))
    m = l_list[0].shape[0]
    l_hi_lo = [_split_bf16(l) for l in l_list]
    pw = [hl[0] for hl in l_hi_lo]
    tinv = [eye + l for l in l_list]
    levels = int(math.log2(RW_CHUNK))
    nxt = [_dot(pw[u], pw[u]).astype(BF16) for u in n]
    yield
    for j in range(1, levels):
        pw = nxt
        if j < levels - 1:
            both = [_dot(jnp.concatenate([pw[u], tinv[u].astype(BF16)], axis=0), pw[u]) for u in n]
            nxt = [both[u][:m].astype(BF16) for u in n]
            tinv = [tinv[u] + both[u][m:] for u in n]
        else:
            tinv = [tinv[u] + _dot(tinv[u].astype(BF16), pw[u]) for u in n]
        yield
    t_hi_lo = [_split_bf16(t) for t in tinv]
    lt = [_dot(jnp.concatenate(l_hi_lo[u], axis=0), t_hi_lo[u][0]) for u in n]
    yield
    resid = [(eye - tinv[u]) + (lt[u][:m] + (lt[u][m:] + _dot(l_hi_lo[u][0], t_hi_lo[u][1])))
             for u in n]
    yield
    return [tinv[u] + _dot(t_hi_lo[u][0], resid[u].astype(BF16)) for u in n]


def _rwkv_body(rf_ref, vf_ref, af_ref, lwf_ref, bef_ref, kdf_ref,
               rb_ref, vb_ref, ab_ref, lwb_ref, beb_ref, kdb_ref,
               of_ref, ob_ref, h_scr, *, nc):
    c = RW_CHUNK
    n2 = 2 * c
    npair = h_scr.shape[1]
    t_idx = lax.broadcasted_iota(jnp.int32, (c, c), 0)
    s_idx = lax.broadcasted_iota(jnp.int32, (c, c), 1)
    ti = lax.broadcasted_iota(jnp.int32, (n2, n2), 0)
    si = lax.broadcasted_iota(jnp.int32, (n2, n2), 1)
    same = (ti // c) == (si // c)
    tl, sl = ti % c, si % c
    eye = jnp.where(ti == si, 1.0, 0.0)
    first = lax.broadcasted_iota(jnp.int32, (c, V7X_LANES), 1) < RW_HEAD_DIM

    def stack(x):
        return jnp.concatenate([jnp.where(first, x, 0.0), jnp.where(first, 0.0, x)], axis=0)

    dirs = ((rf_ref, vf_ref, af_ref, lwf_ref, bef_ref, kdf_ref, of_ref),
            (rb_ref, vb_ref, ab_ref, lwb_ref, beb_ref, kdb_ref, ob_ref))
    masks = []
    for reverse in (False, True):
        masks.append((((s_idx >= t_idx) if reverse else (s_idx <= t_idx)).astype(F32).astype(BF16),
                      same & ((sl > tl) if reverse else (sl < tl)),
                      same & ((sl >= tl) if reverse else (sl <= tl))))

    units = []
    for step in range(nc):
        for d, (r_ref, v_ref, a_ref, lw_ref, be_ref, kd_ref, _) in enumerate(dirs):
            reverse = d == 1
            tri, strict, incl = masks[d]
            j = nc - 1 - step if reverse else step
            rows = slice(j * c, (j + 1) * c)
            lw = lw_ref[0, rows, :]
            lg = _dot_split_rhs3(tri, lw)
            ltot = lg[0:1] if reverse else lg[c - 1:c]
            ginv = jnp.exp(-lg)
            gend = jnp.exp(ltot - lg)
            be, kd = be_ref[0, rows, :], kd_ref[0, rows, :]
            cols = (a_ref[0, rows, :] * jnp.exp(lg - lw), r_ref[0, rows, :] * jnp.exp(lg),
                    be * ginv, kd * ginv, be * gend, kd * gend, v_ref[0, rows, :])
            gtot = jnp.exp(ltot)
            for p in range(npair):
                ps = slice(p * V7X_LANES, (p + 1) * V7X_LANES)
                a_s, r_s, b_s, k_s, bh_s, kh_s, v_s = (stack(x[:, ps]).astype(BF16) for x in cols)
                units.append(dict(step=step, d=d, p=p, rows=rows, ps=ps, strict=strict, incl=incl,
                                  ar=jnp.concatenate([a_s, r_s], axis=0),
                                  bk=jnp.concatenate([b_s, k_s], axis=0),
                                  bh=bh_s, kh=kh_s, v=v_s, gtot=gtot[:, ps]))
            yield
    for u in units:
        sc = _dot_nt(u["ar"], u["bk"])
        u["l_ab"] = jnp.where(u["strict"], sc[:n2, :n2], 0.0)
        u["l_ak"] = jnp.where(u["strict"], sc[:n2, n2:], 0.0).astype(BF16)
        u["m_rb"] = jnp.where(u["incl"], sc[n2:, :n2], 0.0).astype(BF16)
        u["m_rk"] = jnp.where(u["incl"], sc[n2:, n2:], 0.0).astype(BF16)
    yield
    tinvs = yield from _tri_inverse_all([u["l_ab"] for u in units], eye)
    for u, tinv in zip(units, tinvs):
        u["tinv"] = tinv.astype(BF16)
    yield
    for u in units:
        both = _dot(jnp.concatenate([u["l_ak"], u["m_rk"]], axis=0), u["v"])
        u["lakv"], u["mv"] = both[:n2], both[n2:]
        u["kv"] = _dot_tn(u["v"], u["kh"])
    yield

    state = {(d, p): h_scr[d, p] for d in range(2) for p in range(npair)}
    for step in range(nc):
        live = [u for u in units if u["step"] == step]
        for u in live:
            u["ah"] = _dot_nt(u["ar"], state[u["d"], u["p"]].astype(BF16))
        yield
        for u in live:
            u["u"] = _dot(u["tinv"], (u["ah"][:n2] + u["lakv"]).astype(BF16)).astype(BF16)
        yield
        for u in live:
            o_s = u["ah"][n2:] + u["mv"] + _dot(u["m_rb"], u["u"])
            dirs[u["d"]][6][0, u["rows"], u["ps"]] = o_s[:c] + o_s[c:]
            key = (u["d"], u["p"])
            state[key] = state[key] * u["gtot"] + u["kv"] + _dot_tn(u["u"], u["bh"])
        yield
    for (d, p), h in state.items():
        h_scr[d, p] = h


N_HGRN_IN = 9
N_RWKV_IN = 12


def _scans_kernel(*refs, nc):
    hg_in = refs[:N_HGRN_IN]
    rw_in = refs[N_HGRN_IN:N_HGRN_IN + N_RWKV_IN]
    hg_of, hg_ob, rw_of, rw_ob, st_scr, h_scr = refs[N_HGRN_IN + N_RWKV_IN:]

    @pl.when(pl.program_id(1) == 0)
    def _():
        st_scr[...] = jnp.zeros_like(st_scr)
        h_scr[...] = jnp.zeros_like(h_scr)

    streams = [_rwkv_body(*rw_in, rw_of, rw_ob, h_scr, nc=nc),
               _hgrn_body(*hg_in, hg_of, hg_ob, st_scr, nc=nc)]
    while streams:
        for g in list(streams):
            if next(g, StopIteration) is StopIteration:
                streams.remove(g)


def scans(z_hg, hg_col0, valid_seq, lb, r, v, nkk, lw_f, be_f, kd_f, lw_b, be_b, kd_b, nc):
    assert HG_CHUNK == RW_CHUNK
    bt, p, width = r.shape
    nh = width // HG_HEAD_DIM
    rows = nc * HG_CHUNK
    n = p // rows
    blk = (1, rows, width)

    def spec(group, rev):
        if rev:
            return pl.BlockSpec(blk, lambda b, i: (b, n - 1 - i, group))
        return pl.BlockSpec(blk, lambda b, i: (b, i, group))

    fwd, bwd = spec(0, False), spec(0, True)
    mf = pl.BlockSpec((rows, 1), lambda b, i: (i, 0))
    mb = pl.BlockSpec((rows, 1), lambda b, i: (n - 1 - i, 0))
    out = jax.ShapeDtypeStruct((bt, p, width), F32)
    g0 = hg_col0 // width
    hg_specs = [spec(g0, False), spec(g0 + 1, False), spec(g0 + 3, False), mf,
                spec(g0, True), spec(g0 + 2, True), spec(g0 + 3, True), mb,
                pl.BlockSpec((2, width), lambda b, i: (0, 0))]
    assert len(hg_specs) == N_HGRN_IN
    return pl.pallas_call(
        functools.partial(_scans_kernel, nc=nc),
        grid=(bt, n),
        in_specs=hg_specs + [fwd] * (N_RWKV_IN // 2) + [bwd] * (N_RWKV_IN // 2),
        out_specs=[fwd, bwd, fwd, bwd],
        out_shape=[out] * 4,
        scratch_shapes=[pltpu.VMEM((2, nh, HG_HEAD_DIM, HG_HEAD_DIM), F32),
                        pltpu.VMEM((2, width // V7X_LANES, V7X_LANES, V7X_LANES), F32)],
        compiler_params=_params(("parallel", "arbitrary")),
        name="scans",
    )(z_hg, z_hg, z_hg, valid_seq, z_hg, z_hg, z_hg, valid_seq, lb,
      r, v, nkk, lw_f, be_f, kd_f, r, v, nkk, lw_b, be_b, kd_b)


def _merge_kernel(x_ref, g_ref, valid_ref, hof_ref, hob_ref, hg_ref, ysc_ref, yda_ref,
                  rof_ref, rob_ref, bonus_ref, rgate_ref, onorm_ref, lnxg_ref, lnxb_ref, bd_ref,
                  wg_ref, bp_ref, wo_ref, o_ref):
    x = x_ref[...]
    d = x.shape[1]
    h = _masked_rms_norm(x, g_ref[...], valid_ref[...]).astype(BF16)

    o = hof_ref[...] + hob_ref[...]
    heads = []
    for c in range(0, o.shape[1], HG_HEAD_DIM):
        oh = o[:, c:c + HG_HEAD_DIM]
        heads.append(oh * lax.rsqrt(jnp.mean(oh * oh, axis=-1, keepdims=True) + NORM_EPS))
    hg = hg_ref[...]
    y_hg = jnp.concatenate(heads, axis=1) * onorm_ref[...] * (hg * _sigmoid(hg))

    o = rof_ref[...] + rob_ref[...]
    bd = bd_ref[...]
    inv = 1.0 / RW_HEAD_DIM
    cen = o - _group_sum(o, bd) * inv
    var = _group_sum(cen * cen, bd) * inv
    y_rw = ((cen * lax.rsqrt(var + RW_LNX_EPS) * lnxg_ref[...] + lnxb_ref[...] + bonus_ref[...])
            * rgate_ref[...])

    merged = jnp.zeros_like(x)
    for n, y in enumerate((y_hg, ysc_ref[...], yda_ref[...], y_rw)):
        gate = _sigmoid(_dot(h, wg_ref[:, n * d:(n + 1) * d]))
        merged = merged + gate * _dot(y.astype(BF16), bp_ref[n])
    o_ref[...] = x + _dot(merged.astype(BF16), wo_ref[...])


def merge(x2d, g, valid, hg_of, hg_ob, z_hg2d, hg_col0, y_sc, y_da, rw_of, rw_ob, bonus, rw_gate,
          onorm_g, lnx_g, lnx_b, bd_ones, w_gate, branch_proj, w_out, tm):
    m, d = x2d.shape
    bw = y_sc.shape[1]
    row = lambda width: pl.BlockSpec((tm, width), lambda i: (i, 0))
    full = lambda a: pl.BlockSpec(a.shape, lambda i: (0,) * a.ndim)
    gate_blk = hg_col0 // bw + 4
    hg_gate = pl.BlockSpec((tm, bw), lambda i: (i, gate_blk))
    consts = (onorm_g, lnx_g, lnx_b, bd_ones, w_gate, branch_proj, w_out)
    return pl.pallas_call(
        _merge_kernel,
        grid=(m // tm,),
        in_specs=[row(d), full(g), row(1), row(bw), row(bw), hg_gate] + [row(bw)] * 6
                 + [full(a) for a in consts],
        out_specs=row(d),
        out_shape=jax.ShapeDtypeStruct((m, d), F32),
        compiler_params=_params(("parallel",)),
        name="merge",
    )(x2d, g, valid, hg_of, hg_ob, z_hg2d, y_sc, y_da, rw_of, rw_ob, bonus, rw_gate, *consts)


def _mlp_kernel(x_ref, g_ref, w1_ref, w2_ref, o_ref, h_scr, acc_scr):
    f = pl.program_id(1)

    @pl.when(f == 0)
    def _():
        x = x_ref[...]
        ms = jnp.mean(x * x, axis=-1, keepdims=True)
        h_scr[...] = (x * lax.rsqrt(ms + NORM_EPS) * g_ref[...]).astype(BF16)
        acc_scr[...] = jnp.zeros_like(acc_scr)

    a = jnp.maximum(_dot(h_scr[...], w1_ref[...]), 0.0)
    acc_scr[...] += _dot((a * a).astype(BF16), w2_ref[...])

    @pl.when(f == pl.num_programs(1) - 1)
    def _():
        o_ref[...] = x_ref[...] + acc_scr[...]


def mlp(x2d, g, w1, w2, tm, tf):
    m, d = x2d.shape
    dff = w1.shape[1]
    return pl.pallas_call(
        _mlp_kernel,
        grid=(m // tm, dff // tf),
        in_specs=[pl.BlockSpec((tm, d), lambda i, f: (i, 0)),
                  pl.BlockSpec((1, d), lambda i, f: (0, 0)),
                  pl.BlockSpec((d, tf), lambda i, f: (0, f)),
                  pl.BlockSpec((tf, d), lambda i, f: (f, 0))],
        out_specs=pl.BlockSpec((tm, d), lambda i, f: (i, 0)),
        out_shape=jax.ShapeDtypeStruct((m, d), F32),
        scratch_shapes=[pltpu.VMEM((tm, d), BF16), pltpu.VMEM((tm, d), F32)],
        compiler_params=_params(("parallel", "arbitrary")),
        name="mlp",
    )(x2d, g, w1, w2)


def _rope_tables(p):
    half = ROPE_DIM // 2
    pos = jnp.arange(p, dtype=F32) - FRONT_PAD
    inv = ROPE_THETA ** (-jnp.arange(half, dtype=F32) / half)
    ang = pos[:, None] * inv[None, :]
    cos, sin = jnp.cos(ang), jnp.sin(ang)
    ones = jnp.ones((p, DA_QK_DIM - ROPE_DIM), F32)
    zeros = jnp.zeros((p, DA_QK_DIM - ROPE_DIM), F32)
    zh = jnp.zeros((p, half), F32)
    c = jnp.concatenate([cos, cos, ones], axis=1)
    s1 = jnp.concatenate([-sin, zh, zeros], axis=1)
    s2 = jnp.concatenate([zh, sin, zeros], axis=1)
    return tuple(jnp.tile(t, (1, 2)) for t in (c, s1, s2))


def _block_diag_ones(n, blk):
    i = jnp.arange(n)
    return ((i[:, None] // blk) == (i[None, :] // blk)).astype(F32)


def kernel(x_prompt, x_sample, meta_tokens, norm_mix_g, w_in, hgrn_lb_logits, hgrn_onorm_g, conv_w,
           diff_qnorm_g, diff_knorm_g, diff_lambda, diff_subln_g, rwkv_mu, rwkv_w0, rwkv_w2, rwkv_a0,
           rwkv_a2, rwkv_g2, rwkv_k_k, rwkv_k_a, rwkv_r_k, rwkv_lnx_g, rwkv_lnx_b, w_gate,
           branch_proj, w_out, norm_mlp_g, mlp_w1, mlp_w2):
    assert x_prompt.shape[1:] == x_sample.shape[1:]
    x = jnp.concatenate([x_prompt, x_sample], axis=0)
    bt, seq, d = x.shape
    depth = w_in.shape[0]
    length = N_META + seq
    p = -(-(FRONT_PAD + length) // V7X_LANES) * V7X_LANES
    m = bt * p
    bw = branch_proj.shape[2]
    rw_cols = rwkv_mu.shape[1]
    sizes = (5 * bw, 3 * bw, 3 * bw, rw_cols)
    offs = [0]
    for s in sizes:
        offs.append(offs[-1] + s)
    rw_pad = -(-rw_cols // bw) * bw
    c_hg, c_sc, c_da = rw_pad, rw_pad + sizes[0], rw_pad + sizes[0] + sizes[1]
    zw = c_da + sizes[2]

    meta = jnp.broadcast_to(meta_tokens.astype(x.dtype)[None], (bt, N_META, d))
    xp = jnp.concatenate([jnp.zeros((bt, FRONT_PAD, d), x.dtype), meta, x,
                          jnp.zeros((bt, p - FRONT_PAD - length, d), x.dtype)], axis=1)
    x2d = xp.reshape(m, d)

    rows = jnp.arange(p)
    valid_seq = ((rows >= FRONT_PAD) & (rows < FRONT_PAD + length)).astype(F32)[:, None]
    valid = jnp.tile(valid_seq, (bt, 1))
    key_bias = jnp.where(valid_seq[:, 0] > 0, 0.0, NEG_BIG).astype(F32)[None, :]
    tables = _rope_tables(p)
    bd64_mean = (_block_diag_ones(V7X_LANES, DA_QK_DIM) / DA_QK_DIM).astype(BF16)
    bd64_ones = _block_diag_ones(V7X_LANES, RW_HEAD_DIM).astype(BF16)

    sm = jax.nn.softmax(hgrn_lb_logits.astype(F32), axis=1)
    lb_all = jnp.cumsum(sm, axis=1) - sm[:, :1]

    tm_proj = _pick(m, 1536, 128)
    tm_merge = _pick(m, 256, 128)
    tm_mlp = _pick(m, 768, 128)
    tq = _pick(p, 384, 128)
    tk = _pick(p, 384, 128)
    assert FRONT_PAD <= tk and p - (FRONT_PAD + length) <= tk
    tr = _pick(p, 384, 8)

    half = V7X_LANES // 2
    for l in range(depth):
        g_mix = norm_mix_g[l][None, :]
        w_l = jnp.concatenate([w_in[l][:, offs[3]:], jnp.zeros((d, rw_pad - rw_cols), w_in.dtype),
                               w_in[l][:, :offs[3]]], axis=1).astype(BF16)
        z = norm_matmul(x2d, g_mix, valid, w_l, tm_proj, _pick(zw, 768, V7X_LANES)).reshape(bt, p, zw)

        y_sc = shortconv(z, c_sc, conv_w[l])

        lam_init = 0.8 - 0.6 * math.exp(-0.3 * l)
        y_da = diff_attention(z, c_da, tables, bd64_mean,
                              jnp.tile(diff_qnorm_g[l], 2)[None, :], jnp.tile(diff_knorm_g[l], 2)[None, :],
                              diff_lambda[l], diff_subln_g[l][None, :], key_bias, lam_init, tq, tk)

        zpad = jnp.zeros((2, half, bw), F32)
        w2pad = jnp.stack([jnp.concatenate([rwkv_w2[l, 0], zpad[0]], 0),
                           jnp.concatenate([zpad[0], rwkv_w2[l, 1]], 0)])
        a2pad = jnp.stack([jnp.concatenate([rwkv_a2[l, 0], zpad[0]], 0),
                           jnp.concatenate([zpad[0], rwkv_a2[l, 1]], 0)])
        (r, v, nkk, lwf, bef, kdf, lwb, beb, kdb, gate, bonus) = rwkv_prep(
            z, valid_seq, rwkv_mu[l][None, :], rwkv_w0[l], w2pad, rwkv_a0[l], a2pad, rwkv_g2[l],
            rwkv_k_k[l][None, :], rwkv_k_a[l][None, :], rwkv_r_k[l].reshape(1, bw), bd64_ones, tr)
        of, ob, orf, orb = scans(z, c_hg, valid_seq, lb_all[:, l], r, v, nkk, lwf, bef, kdf, lwb, beb,
                                 kdb, _pick(p // RW_CHUNK, SCAN_CHUNKS_PER_STEP, 1))

        flat = lambda a: a.reshape(m, a.shape[-1])
        x2d = merge(x2d, g_mix, valid, flat(of), flat(ob), flat(z), c_hg, flat(y_sc), flat(y_da),
                    flat(orf), flat(orb), flat(bonus), flat(gate), hgrn_onorm_g[l][None, :],
                    rwkv_lnx_g[l][None, :], rwkv_lnx_b[l][None, :], bd64_ones,
                    w_gate[l].astype(BF16), branch_proj[l].astype(BF16), w_out[l].astype(BF16), tm_merge)
        x2d = mlp(x2d, norm_mlp_g[l][None, :], mlp_w1[l].astype(BF16), mlp_w2[l].astype(BF16),
                  tm_mlp, _pick(mlp_w1.shape[2], 512, V7X_LANES))

    y = x2d.reshape(bt, p, d)[:, FRONT_PAD + N_META:FRONT_PAD + length]
    nb = x_prompt.shape[0]
    return (y[:nb], y[nb:])
```

```python
import functools
import math

import jax
import jax.numpy as jnp
from jax import lax
from jax.experimental import pallas as pl
from jax.experimental.pallas import tpu as pltpu

F32 = jnp.float32
BF16 = jnp.bfloat16
HI = lax.Precision.HIGHEST

V7X_LANES = 128
V7X_VMEM_BYTES = 64 * 1024 * 1024
VMEM_LIMIT = V7X_VMEM_BYTES - 8 * 1024 * 1024

N_META = 16
NORM_EPS = 1e-6
N_BRANCH = 4
HG_HEAD_DIM = 128
HG_CHUNK = 64
HG_SUB = 16
LB_FLOOR = 1e-20
SC_KSIZE = 3
DA_HEADS = 4
DA_QK_DIM = 64
DA_V_DIM = 128
ROPE_THETA = 500000.0
ROPE_DIM = DA_QK_DIM // 4
SUBLN_EPS = 1e-5
RW_HEAD_DIM = 64
RW_CHUNK = 64
RW_LNX_EPS = 64e-5
FRONT_PAD = (-N_META) % HG_CHUNK
NEG_BIG = -1e30
LOG2_E = 1.4426950408889634

P_HG = "bf16"
P_RW_LR = "bf16"
SCAN_CHUNKS_PER_STEP = 3


def _split_bf16(a):
    hi = a.astype(BF16)
    return hi, (a - hi.astype(F32)).astype(BF16)


def _mm(a, b, dims, precision):
    dg = functools.partial(lax.dot_general, dimension_numbers=(dims, ((), ())),
                           preferred_element_type=F32)
    if precision == "bf16":
        return dg(a.astype(BF16), b.astype(BF16))
    if precision == "x3":
        ah, al = _split_bf16(a)
        bh, bl = _split_bf16(b)
        return dg(ah, bh) + (dg(al, bh) + dg(ah, bl))
    return dg(a, b, precision=precision)


def _dot(a, b, precision=None):
    return _mm(a, b, ((1,), (0,)), precision)


def _dot_split_rhs3(a_bf16, b):
    b1 = b.astype(BF16)
    r1 = b - b1.astype(F32)
    b2 = r1.astype(BF16)
    b3 = (r1 - b2.astype(F32)).astype(BF16)
    return _dot(a_bf16, b1) + (_dot(a_bf16, b2) + _dot(a_bf16, b3))


def _dot_split_lhs(a, b_bf16):
    hi, lo = _split_bf16(a)
    return _dot(hi, b_bf16) + _dot(lo, b_bf16)


def _dot_nt(a, b, precision=None):
    return _mm(a, b, ((1,), (1,)), precision)


def _dot_tn(a, b, precision=None):
    return _mm(a, b, ((0,), (0,)), precision)


def _pick(n, target, mult):
    best = None
    for d in range(mult, min(n, target) + 1, mult):
        if n % d == 0:
            best = d
    assert best is not None, (n, target, mult)
    return best


def _params(sem):
    return pltpu.CompilerParams(dimension_semantics=sem, vmem_limit_bytes=VMEM_LIMIT)


def _log_sigmoid(x):
    return jnp.minimum(x, 0.0) - jnp.log1p(jnp.exp(-jnp.abs(x)))


def _sigmoid(x):
    return 1.0 / (1.0 + jnp.exp(-x))


def _masked_rms_norm(x, g, valid):
    ms = jnp.mean(x * x, axis=-1, keepdims=True)
    h = x * lax.rsqrt(ms + NORM_EPS) * g
    return jnp.where(valid > 0.0, h, 0.0)


def _norm_matmul_kernel(x_ref, g_ref, valid_ref, w_ref, o_ref, h_scr):
    @pl.when(pl.program_id(1) == 0)
    def _():
        h_scr[...] = _masked_rms_norm(x_ref[...], g_ref[...], valid_ref[...]).astype(BF16)

    o_ref[...] = _dot(h_scr[...], w_ref[...])


def norm_matmul(x2d, g, valid, w_bf16, tm, tn):
    m, d = x2d.shape
    n = w_bf16.shape[1]
    return pl.pallas_call(
        _norm_matmul_kernel,
        grid=(m // tm, n // tn),
        in_specs=[
            pl.BlockSpec((tm, d), lambda i, j: (i, 0)),
            pl.BlockSpec((1, d), lambda i, j: (0, 0)),
            pl.BlockSpec((tm, 1), lambda i, j: (i, 0)),
            pl.BlockSpec((d, tn), lambda i, j: (0, j)),
        ],
        out_specs=pl.BlockSpec((tm, tn), lambda i, j: (i, j)),
        out_shape=jax.ShapeDtypeStruct((m, n), F32),
        scratch_shapes=[pltpu.VMEM((tm, d), BF16)],
        compiler_params=_params(("parallel", "arbitrary")),
        name="norm_matmul",
    )(x2d, g, valid, w_bf16)


def _shortconv_kernel(b_ref, c_ref, h_ref, w_ref, o_ref):
    u = c_ref[0] * h_ref[0]
    p = u.shape[0]
    w = w_ref[...]
    y = pltpu.roll(u, 1, 0) * w[0:1] + u * w[1:2] + pltpu.roll(u, p - 1, 0) * w[2:3]
    o_ref[0] = b_ref[0] * y


def shortconv(z, col0, conv_w):
    bt, p, _ = z.shape
    width = conv_w.shape[1]
    nb = width // V7X_LANES
    c0 = col0 // V7X_LANES
    blk = (1, p, V7X_LANES)
    z_sc = z
    return pl.pallas_call(
        _shortconv_kernel,
        grid=(bt, nb),
        in_specs=[
            pl.BlockSpec(blk, lambda b, c: (b, 0, c0 + c)),
            pl.BlockSpec(blk, lambda b, c: (b, 0, c0 + nb + c)),
            pl.BlockSpec(blk, lambda b, c: (b, 0, c0 + 2 * nb + c)),
            pl.BlockSpec((SC_KSIZE, V7X_LANES), lambda b, c: (0, c)),
        ],
        out_specs=pl.BlockSpec(blk, lambda b, c: (b, 0, c)),
        out_shape=jax.ShapeDtypeStruct((bt, p, width), F32),
        compiler_params=_params(("parallel", "parallel")),
        name="shortconv",
    )(z_sc, z_sc, z_sc, conv_w)


def _attn_kernel(q_ref, k_ref, v_ref, cq_ref, s1q_ref, s2q_ref, ck_ref, s1k_ref, s2k_ref,
                 bd_ref, qg_ref, kg_ref, lam_ref, sg_ref, bias_ref, o_ref, k_scr, v_scr, s_scr,
                 *, lam_init, tk):
    bd = bd_ref[...]

    def norm_rope(x, g, c, s1, s2):
        ms = _dot_split_lhs(x * x, bd)
        y = x * lax.rsqrt(ms + NORM_EPS) * g
        return (y * c + pltpu.roll(y, V7X_LANES - ROPE_DIM // 2, 1) * s1
                + pltpu.roll(y, ROPE_DIM // 2, 1) * s2)

    @pl.when(pl.program_id(2) == 0)
    def _():
        k = norm_rope(k_ref[0], kg_ref[...], ck_ref[...], s1k_ref[...], s2k_ref[...])
        k_scr[...] = k.astype(BF16)
        v_scr[...] = v_ref[0].astype(BF16)

    q = norm_rope(q_ref[0], qg_ref[...], cq_ref[...], s1q_ref[...], s2q_ref[...])
    q = q * (DA_QK_DIM ** -0.5 * LOG2_E)
    tq = q.shape[0]
    first = lax.broadcasted_iota(jnp.int32, q.shape, 1) < DA_QK_DIM
    lp = lam_ref[...]
    lam = (jnp.exp(jnp.sum(lp[0:1] * lp[1:2], axis=-1, keepdims=True))
           - jnp.exp(jnp.sum(lp[2:3] * lp[3:4], axis=-1, keepdims=True)) + lam_init)
    qs = jnp.concatenate([jnp.where(first, q, 0.0), jnp.where(first, 0.0, q)], axis=0).astype(BF16)
    nk = k_scr.shape[0] // tk
    lanes = range(0, tk, V7X_LANES)
    m128 = None
    for j in range(nk):
        cols = slice(j * tk, (j + 1) * tk)
        s_j = _dot_nt(qs, k_scr[cols, :])
        if j in (0, nk - 1):
            s_j = s_j + bias_ref[:, cols]
        s_scr[:, cols] = s_j
        mt = functools.reduce(jnp.maximum, [s_j[:, c:c + V7X_LANES] for c in lanes])
        m128 = mt if m128 is None else jnp.maximum(m128, mt)
    m = jnp.max(m128, axis=-1, keepdims=True)
    l128 = jnp.zeros_like(m128)
    o2 = jnp.zeros((2 * tq, DA_V_DIM), F32)
    for j in range(nk):
        cols = slice(j * tk, (j + 1) * tk)
        pr = jnp.exp2(s_scr[:, cols] - m)
        l128 = l128 + functools.reduce(jnp.add, [pr[:, c:c + V7X_LANES] for c in lanes])
        o2 = o2 + _dot(pr.astype(BF16), v_scr[cols, :])
    inv = 1.0 / jnp.sum(l128, axis=-1, keepdims=True)
    o = o2[:tq] * inv[:tq] - o2[tq:] * (lam * inv[tq:])
    ms = jnp.mean(o * o, axis=-1, keepdims=True)
    o_ref[0] = o * lax.rsqrt(ms + SUBLN_EPS) * sg_ref[...] * (1.0 - lam_init)


def _alternate(streams):
    streams = list(streams)
    while streams:
        for g in list(streams):
            if next(g, StopIteration) is StopIteration:
                streams.remove(g)


def diff_attention(z, col0, tables, bd64, qn_g, kn_g, lam_p, subln_g, key_bias, lam_init, tq, tk):
    bt, p, _ = z.shape
    z_da = z
    c_t, s1_t, s2_t = tables
    nh = DA_HEADS
    c0 = col0 // V7X_LANES
    qspec = pl.BlockSpec((1, tq, V7X_LANES), lambda b, h, i: (b, i, c0 + h))
    kspec = pl.BlockSpec((1, p, V7X_LANES), lambda b, h, i: (b, 0, c0 + nh + h))
    vspec = pl.BlockSpec((1, p, V7X_LANES), lambda b, h, i: (b, 0, c0 + 2 * nh + h))
    tq_spec = pl.BlockSpec((tq, V7X_LANES), lambda b, h, i: (i, 0))
    tk_spec = pl.BlockSpec((p, V7X_LANES), lambda b, h, i: (0, 0))
    full = lambda shape: pl.BlockSpec(shape, lambda b, h, i: (0,) * len(shape))
    return pl.pallas_call(
        functools.partial(_attn_kernel, lam_init=lam_init, tk=tk),
        grid=(bt, nh, p // tq),
        in_specs=[qspec, kspec, vspec, tq_spec, tq_spec, tq_spec, tk_spec, tk_spec, tk_spec,
                  full((V7X_LANES, V7X_LANES)), full((1, V7X_LANES)), full((1, V7X_LANES)),
                  full((4, DA_QK_DIM)), full((1, V7X_LANES)), full((1, p))],
        out_specs=pl.BlockSpec((1, tq, V7X_LANES), lambda b, h, i: (b, i, h)),
        out_shape=jax.ShapeDtypeStruct((bt, p, nh * DA_V_DIM), F32),
        scratch_shapes=[pltpu.VMEM((p, V7X_LANES), BF16), pltpu.VMEM((p, V7X_LANES), BF16),
                        pltpu.VMEM((2 * tq, p), F32)],
        compiler_params=_params(("parallel", "parallel", "arbitrary")),
        name="diff_attention",
    )(z_da, z_da, z_da, c_t, s1_t, s2_t, c_t, s1_t, s2_t, bd64, qn_g, kn_g, lam_p, subln_g,
      key_bias)


def _gla_local(q, qt, qh, k, kh, v, b, bpiv, reverse):
    c, sub, half = HG_CHUNK, HG_SUB, HG_SUB // 2
    nsub = c // sub
    row = lax.broadcasted_iota(jnp.int32, (c, 1), 0)
    attn = jnp.zeros((c, c), F32)
    for i in range(nsub):
        if (reverse and i == nsub - 1) or (not reverse and i == 0):
            continue
        bi = bpiv[i * sub:i * sub + 1]
        kmask = (row >= (i + 1) * sub) if reverse else (row < i * sub)
        kt = jnp.where(kmask, k * jnp.exp(jnp.minimum(bi - b, 0.0)), 0.0)
        attn = jnp.where(row // sub == i, _dot_nt(qt, kt, P_HG), attn)
    t_idx = lax.broadcasted_iota(jnp.int32, (c, c), 0)
    s_idx = lax.broadcasted_iota(jnp.int32, (c, c), 1)
    attn = jnp.where(t_idx // sub == s_idx // sub, _dot_nt(qh, kh, P_HG), attn)

    b2 = b * LOG2_E
    tloc = lax.broadcasted_iota(jnp.int32, (half, c), 0)
    scol = lax.broadcasted_iota(jnp.int32, (half, c), 1)
    blocks = []
    for i in range(c // half):
        bt, qi = b2[i * half:(i + 1) * half], q[i * half:(i + 1) * half]
        blk = attn[i * half:(i + 1) * half]
        for j in range(half):
            s = i * half + j
            a = jnp.sum(qi * k[s:s + 1] * jnp.exp2(bt - b2[s:s + 1]), axis=-1, keepdims=True)
            keep = (scol == s) & ((tloc <= j) if reverse else (tloc >= j))
            blk = jnp.where(keep, a, blk)
        blocks.append(blk)
    return _dot(jnp.concatenate(blocks, axis=0), v, P_HG)


def _hgrn_body(qf_ref, ff_ref, vf_ref, mf_ref, qb_ref, fb_ref, vb_ref, mb_ref, lb_ref,
               of_ref, ob_ref, st_scr, *, nc):
    c, sub, half = HG_CHUNK, HG_SUB, HG_SUB // 2
    nh = st_scr.shape[1]
    t_idx = lax.broadcasted_iota(jnp.int32, (c, c), 0)
    s_idx = lax.broadcasted_iota(jnp.int32, (c, c), 1)
    second = (lax.broadcasted_iota(jnp.int32, (c, 1), 0) % sub) >= half
    dirs = ((qf_ref, ff_ref, vf_ref, mf_ref, of_ref), (qb_ref, fb_ref, vb_ref, mb_ref, ob_ref))
    units = []
    for d, (q_ref, f_ref, v_ref, m_ref, _) in enumerate(dirs):
        reverse = d == 1
        if reverse:
            mats = (s_idx >= t_idx, s_idx >= (t_idx // sub + 1) * sub,
                    s_idx >= (t_idx // half + 1) * half, s_idx >= (t_idx // half) * half)
        else:
            mats = (s_idx <= t_idx, s_idx < (t_idx // sub) * sub,
                    s_idx < (t_idx // half) * half, s_idx < (t_idx // half + 1) * half)
        cmat = jnp.concatenate([x.astype(F32) for x in mats], axis=0).astype(BF16)
        q_half = jnp.logical_not(second) if reverse else second
        lb = lb_ref[d:d + 1, :]
        la = jnp.log(jnp.maximum(lb, LB_FLOOR))
        l1 = jnp.log1p(-lb)
        for j in range(nc):
            rows = slice(j * c, (j + 1) * c)
            valid = m_ref[rows, :] > 0.0
            lc = l1 + _log_sigmoid(f_ref[0, rows, :])
            logf = jnp.maximum(la, lc) + jnp.log1p(jnp.exp(-jnp.abs(la - lc)))
            logf = jnp.where(valid, logf, 0.0)
            k = jnp.where(valid, 1.0 - jnp.exp(logf), 0.0)
            cum = _dot_split_rhs3(cmat, logf)
            b, bpiv, bhin, bhout = (cum[i * c:(i + 1) * c] for i in range(4))
            btot = b[0:1] if reverse else b[c - 1:c]
            q, v = q_ref[0, rows, :], v_ref[0, rows, :]
            qt = q * jnp.exp(b - bpiv)
            qh = jnp.where(q_half, q * jnp.exp(b - bhin), 0.0)
            kh = jnp.where(q_half, 0.0, k * jnp.exp(bhout - b))
            qe = (q * jnp.exp(b)).astype(BF16)
            k2 = k * jnp.exp(btot - b)
            dec = jnp.exp(btot)
            for h in range(nh):
                hs = slice(h * HG_HEAD_DIM, (h + 1) * HG_HEAD_DIM)
                units.append(dict(
                    d=d, j=j, h=h, rows=rows, hs=hs, qe=qe[:, hs], dec=dec[:, hs],
                    o=_gla_local(q[:, hs], qt[:, hs], qh[:, hs], k[:, hs], kh[:, hs], v[:, hs],
                                 b[:, hs], bpiv[:, hs], reverse),
                    kv=_dot_tn(v[:, hs], k2[:, hs], P_HG)))
                yield

    state = {(d, h): st_scr[d, h] for d in range(2) for h in range(nh)}
    for step in range(nc):
        for u in units:
            if u["j"] != (nc - 1 - step if u["d"] == 1 else step):
                continue
            key = (u["d"], u["h"])
            dirs[u["d"]][4][0, u["rows"], u["hs"]] = u["o"] + _dot_nt(u["qe"], state[key].astype(BF16))
            state[key] = state[key] * u["dec"] + u["kv"]
        yield
    for (d, h), s in state.items():
        st_scr[d, h] = s


def _group_sum(x, bd):
    cols = [_dot_split_lhs(x[:, j:j + V7X_LANES], bd) for j in range(0, x.shape[1], V7X_LANES)]
    return jnp.concatenate(cols, axis=1)


def _rwkv_prep_kernel(z_ref, zp_ref, zn_ref, valid_ref, mu_ref, w0_ref, w2_ref, a0_ref, a2_ref,
                      g2_ref, kk_ref, ka_ref, rk_ref, bd_ref,
                      r_out, v_out, nkk_out, lwf_out, bef_out, kdf_out, lwb_out, beb_out, kdb_out,
                      gate_out, bonus_out):
    u = z_ref[0]
    tr = u.shape[0]
    width = r_out.shape[-1]
    row = lax.broadcasted_iota(jnp.int32, (8, 1), 0)
    down, up = pltpu.roll(u, 1, 0), pltpu.roll(u, tr - 1, 0)
    u_prev = jnp.concatenate([jnp.where(row == 0, zp_ref[0][7:8], down[:8]), down[8:]], axis=0)
    u_next = jnp.concatenate([up[:tr - 8], jnp.where(row == 7, zn_ref[0][0:1], up[tr - 8:])], axis=0)
    mu = mu_ref[...]
    xm = u * (1.0 - mu) + (u_prev + u_next) * (0.5 * mu)
    valid = valid_ref[...] > 0.0
    bd = bd_ref[...]
    r = xm[:, 0:width]
    k = xm[:, width:2 * width]
    v = jnp.where(valid, xm[:, 2 * width:3 * width], 0.0)
    lr = 3 * width
    wl = jnp.tanh(xm[:, lr:lr + V7X_LANES])
    al = xm[:, lr + V7X_LANES:lr + 2 * V7X_LANES]
    gl = xm[:, lr + 2 * V7X_LANES:lr + 3 * V7X_LANES]
    kk = k * kk_ref[...]
    kk = kk / jnp.maximum(jnp.sqrt(_group_sum(kk * kk, bd)), 1e-12)
    kk = jnp.where(valid, kk, 0.0)
    r_out[0] = r.astype(BF16)
    v_out[0] = v.astype(BF16)
    nkk_out[0] = (-kk).astype(BF16)
    kd_sum = jnp.zeros_like(k)
    for d, (lw_out, be_out, kd_out) in enumerate(((lwf_out, bef_out, kdf_out),
                                                   (lwb_out, beb_out, kdb_out))):
        wlog = _log_sigmoid(w0_ref[d:d + 1, :] + _dot(wl, w2_ref[d], P_RW_LR)) - 0.5
        lw_out[0] = jnp.where(valid, -jnp.exp(wlog), 0.0)
        a = _sigmoid(a0_ref[d:d + 1, :] + _dot(al, a2_ref[d], P_RW_LR))
        kd = k * (1.0 + (a - 1.0) * ka_ref[...])
        kd_sum = kd_sum + kd
        kd_out[0] = jnp.where(valid, kd, 0.0).astype(BF16)
        be_out[0] = (kk * a).astype(BF16)
    gate_out[0] = _dot(_sigmoid(gl), g2_ref[...], P_RW_LR).astype(BF16)
    bonus_out[0] = (_group_sum(r * kd_sum * rk_ref[...], bd) * v).astype(BF16)


def rwkv_prep(z_rw, valid2d, mu, w0, w2pad, a0, a2pad, g2, k_k, k_a, r_k, bd_ones, tr):
    bt, p, _ = z_rw.shape
    cols = mu.shape[1]
    width = w0.shape[1]
    nt = p // tr
    r8 = tr // 8
    last8 = p // 8 - 1
    full = lambda a: pl.BlockSpec(a.shape, lambda b, i: (0,) * a.ndim)
    outs = [jax.ShapeDtypeStruct((bt, p, width), F32 if i in (3, 6) else BF16) for i in range(11)]
    ospec = pl.BlockSpec((1, tr, width), lambda b, i: (b, i, 0))
    params = (mu, w0, w2pad, a0, a2pad, g2, k_k, k_a, r_k, bd_ones)
    return pl.pallas_call(
        _rwkv_prep_kernel,
        grid=(bt, nt),
        in_specs=[pl.BlockSpec((1, tr, cols), lambda b, i: (b, i, 0)),
                  pl.BlockSpec((1, 8, cols), lambda b, i: (b, jnp.maximum(i * r8 - 1, 0), 0)),
                  pl.BlockSpec((1, 8, cols), lambda b, i: (b, jnp.minimum((i + 1) * r8, last8), 0)),
                  pl.BlockSpec((tr, 1), lambda b, i: (i, 0))] + [full(a) for a in params],
        out_specs=[ospec] * 11,
        out_shape=outs,
        compiler_params=_params(("parallel", "parallel")),
        name="rwkv_prep",
    )(z_rw, z_rw, z_rw, valid2d, *params)


def _tri_inverse_all(l_list, eye):
    n = range(len(l_list))
    m = l_list[0].shape[0]
    l_hi_lo = [_split_bf16(l) for l in l_list]
    pw = [hl[0] for hl in l_hi_lo]
    tinv = [eye + l for l in l_list]
    levels = int(math.log2(RW_CHUNK))
    nxt = [_dot(pw[u], pw[u]).astype(BF16) for u in n]
    yield
    for j in range(1, levels):
        pw = nxt
        if j < levels - 1:
            both = [_dot(jnp.concatenate([pw[u], tinv[u].astype(BF16)], axis=0), pw[u]) for u in n]
            nxt = [both[u][:m].astype(BF16) for u in n]
            tinv = [tinv[u] + both[u][m:] for u in n]
        else:
            tinv = [tinv[u] + _dot(tinv[u].astype(BF16), pw[u]) for u in n]
        yield
    t_hi_lo = [_split_bf16(t) for t in tinv]
    lt = [_dot(jnp.concatenate(l_hi_lo[u], axis=0), t_hi_lo[u][0]) for u in n]
    yield
    resid = [(eye - tinv[u]) + (lt[u][:m] + (lt[u][m:] + _dot(l_hi_lo[u][0], t_hi_lo[u][1])))
             for u in n]
    yield
    return [tinv[u] + _dot(t_hi_lo[u][0], resid[u].astype(BF16)) for u in n]


def _rwkv_body(rf_ref, vf_ref, af_ref, lwf_ref, bef_ref, kdf_ref,
               rb_ref, vb_ref, ab_ref, lwb_ref, beb_ref, kdb_ref,
               of_ref, ob_ref, h_scr, *, nc):
    c = RW_CHUNK
    n2 = 2 * c
    npair = h_scr.shape[1]
    t_idx = lax.broadcasted_iota(jnp.int32, (c, c), 0)
    s_idx = lax.broadcasted_iota(jnp.int32, (c, c), 1)
    ti = lax.broadcasted_iota(jnp.int32, (n2, n2), 0)
    si = lax.broadcasted_iota(jnp.int32, (n2, n2), 1)
    same = (ti // c) == (si // c)
    tl, sl = ti % c, si % c
    eye = jnp.where(ti == si, 1.0, 0.0)
    first = lax.broadcasted_iota(jnp.int32, (c, V7X_LANES), 1) < RW_HEAD_DIM

    def stack(x):
        return jnp.concatenate([jnp.where(first, x, 0.0), jnp.where(first, 0.0, x)], axis=0)

    dirs = ((rf_ref, vf_ref, af_ref, lwf_ref, bef_ref, kdf_ref, of_ref),
            (rb_ref, vb_ref, ab_ref, lwb_ref, beb_ref, kdb_ref, ob_ref))
    masks = []
    for reverse in (False, True):
        masks.append((((s_idx >= t_idx) if reverse else (s_idx <= t_idx)).astype(F32).astype(BF16),
                      same & ((sl > tl) if reverse else (sl < tl)),
                      same & ((sl >= tl) if reverse else (sl <= tl))))

    units = []
    for step in range(nc):
        for d, (r_ref, v_ref, a_ref, lw_ref, be_ref, kd_ref, _) in enumerate(dirs):
            reverse = d == 1
            tri, strict, incl = masks[d]
            j = nc - 1 - step if reverse else step
            rows = slice(j * c, (j + 1) * c)
            lw = lw_ref[0, rows, :]
            lg = _dot_split_rhs3(tri, lw)
            ltot = lg[0:1] if reverse else lg[c - 1:c]
            ginv = jnp.exp(-lg)
            gend = jnp.exp(ltot - lg)
            be, kd = be_ref[0, rows, :], kd_ref[0, rows, :]
            cols = (a_ref[0, rows, :] * jnp.exp(lg - lw), r_ref[0, rows, :] * jnp.exp(lg),
                    be * ginv, kd * ginv, be * gend, kd * gend, v_ref[0, rows, :])
            gtot = jnp.exp(ltot)
            for p in range(npair):
                ps = slice(p * V7X_LANES, (p + 1) * V7X_LANES)
                a_s, r_s, b_s, k_s, bh_s, kh_s, v_s = (stack(x[:, ps]).astype(BF16) for x in cols)
                units.append(dict(step=step, d=d, p=p, rows=rows, ps=ps, strict=strict, incl=incl,
                                  ar=jnp.concatenate([a_s, r_s], axis=0),
                                  bk=jnp.concatenate([b_s, k_s], axis=0),
                                  bh=bh_s, kh=kh_s, v=v_s, gtot=gtot[:, ps]))
            yield
    for u in units:
        sc = _dot_nt(u["ar"], u["bk"])
        u["l_ab"] = jnp.where(u["strict"], sc[:n2, :n2], 0.0)
        u["l_ak"] = jnp.where(u["strict"], sc[:n2, n2:], 0.0).astype(BF16)
        u["m_rb"] = jnp.where(u["incl"], sc[n2:, :n2], 0.0).astype(BF16)
        u["m_rk"] = jnp.where(u["incl"], sc[n2:, n2:], 0.0).astype(BF16)
    yield
    tinvs = yield from _tri_inverse_all([u["l_ab"] for u in units], eye)
    for u, tinv in zip(units, tinvs):
        u["tinv"] = tinv.astype(BF16)
    yield
    for u in units:
        both = _dot(jnp.concatenate([u["l_ak"], u["m_rk"]], axis=0), u["v"])
        u["lakv"], u["mv"] = both[:n2], both[n2:]
        u["kv"] = _dot_tn(u["v"], u["kh"])
    yield

    state = {(d, p): h_scr[d, p] for d in range(2) for p in range(npair)}
    for step in range(nc):
        live = [u for u in units if u["step"] == step]
        for u in live:
            u["ah"] = _dot_nt(u["ar"], state[u["d"], u["p"]].astype(BF16))
        yield
        for u in live:
            u["u"] = _dot(u["tinv"], (u["ah"][:n2] + u["lakv"]).astype(BF16)).astype(BF16)
        yield
        for u in live:
            o_s = u["ah"][n2:] + u["mv"] + _dot(u["m_rb"], u["u"])
            dirs[u["d"]][6][0, u["rows"], u["ps"]] = o_s[:c] + o_s[c:]
            key = (u["d"], u["p"])
            state[key] = state[key] * u["gtot"] + u["kv"] + _dot_tn(u["u"], u["bh"])
        yield
    for (d, p), h in state.items():
        h_scr[d, p] = h


N_HGRN_IN = 9
N_RWKV_IN = 12


def _scans_kernel(*refs, nc):
    hg_in = refs[:N_HGRN_IN]
    rw_in = refs[N_HGRN_IN:N_HGRN_IN + N_RWKV_IN]
    hg_of, hg_ob, rw_of, rw_ob, st_scr, h_scr = refs[N_HGRN_IN + N_RWKV_IN:]

    @pl.when(pl.program_id(1) == 0)
    def _():
        st_scr[...] = jnp.zeros_like(st_scr)
        h_scr[...] = jnp.zeros_like(h_scr)

    _alternate([_rwkv_body(*rw_in, rw_of, rw_ob, h_scr, nc=nc),
                _hgrn_body(*hg_in, hg_of, hg_ob, st_scr, nc=nc)])


def scans(z_hg, hg_col0, valid_seq, lb, r, v, nkk, lw_f, be_f, kd_f, lw_b, be_b, kd_b, nc):
    assert HG_CHUNK == RW_CHUNK
    bt, p, width = r.shape
    nh = width // HG_HEAD_DIM
    rows = nc * HG_CHUNK
    n = p // rows
    blk = (1, rows, width)

    def spec(group, rev):
        if rev:
            return pl.BlockSpec(blk, lambda b, i: (b, n - 1 - i, group))
        return pl.BlockSpec(blk, lambda b, i: (b, i, group))

    fwd, bwd = spec(0, False), spec(0, True)
    mf = pl.BlockSpec((rows, 1), lambda b, i: (i, 0))
    mb = pl.BlockSpec((rows, 1), lambda b, i: (n - 1 - i, 0))
    out = jax.ShapeDtypeStruct((bt, p, width), F32)
    g0 = hg_col0 // width
    hg_specs = [spec(g0, False), spec(g0 + 1, False), spec(g0 + 3, False), mf,
                spec(g0, True), spec(g0 + 2, True), spec(g0 + 3, True), mb,
                pl.BlockSpec((2, width), lambda b, i: (0, 0))]
    assert len(hg_specs) == N_HGRN_IN
    return pl.pallas_call(
        functools.partial(_scans_kernel, nc=nc),
        grid=(bt, n),
        in_specs=hg_specs + [fwd] * (N_RWKV_IN // 2) + [bwd] * (N_RWKV_IN // 2),
        out_specs=[fwd, bwd, fwd, bwd],
        out_shape=[out] * 4,
        scratch_shapes=[pltpu.VMEM((2, nh, HG_HEAD_DIM, HG_HEAD_DIM), F32),
                        pltpu.VMEM((2, width // V7X_LANES, V7X_LANES, V7X_LANES), F32)],
        compiler_params=_params(("parallel", "arbitrary")),
        name="scans",
    )(z_hg, z_hg, z_hg, valid_seq, z_hg, z_hg, z_hg, valid_seq, lb,
      r, v, nkk, lw_f, be_f, kd_f, r, v, nkk, lw_b, be_b, kd_b)


def _merge_kernel(x_ref, g_ref, valid_ref, hof_ref, hob_ref, hg_ref, ysc_ref, yda_ref,
                  rof_ref, rob_ref, bonus_ref, rgate_ref, onorm_ref, lnxg_ref, lnxb_ref, bd_ref,
                  wg_ref, bp_ref, wo_ref, o_ref):
    x = x_ref[...]
    d = x.shape[1]
    h = _masked_rms_norm(x, g_ref[...], valid_ref[...]).astype(BF16)

    o = hof_ref[...] + hob_ref[...]
    heads = []
    for c in range(0, o.shape[1], HG_HEAD_DIM):
        oh = o[:, c:c + HG_HEAD_DIM]
        heads.append(oh * lax.rsqrt(jnp.mean(oh * oh, axis=-1, keepdims=True) + NORM_EPS))
    hg = hg_ref[...]
    y_hg = jnp.concatenate(heads, axis=1) * onorm_ref[...] * (hg * _sigmoid(hg))

    o = rof_ref[...] + rob_ref[...]
    bd = bd_ref[...]
    inv = 1.0 / RW_HEAD_DIM
    cen = o - _group_sum(o, bd) * inv
    var = _group_sum(cen * cen, bd) * inv
    y_rw = ((cen * lax.rsqrt(var + RW_LNX_EPS) * lnxg_ref[...] + lnxb_ref[...] + bonus_ref[...])
            * rgate_ref[...])

    merged = jnp.zeros_like(x)
    for n, y in enumerate((y_hg, ysc_ref[...], yda_ref[...], y_rw)):
        gate = _sigmoid(_dot(h, wg_ref[:, n * d:(n + 1) * d]))
        merged = merged + gate * _dot(y.astype(BF16), bp_ref[n])
    o_ref[...] = x + _dot(merged.astype(BF16), wo_ref[...])


def merge(x2d, g, valid, hg_of, hg_ob, z_hg2d, hg_col0, y_sc, y_da, rw_of, rw_ob, bonus, rw_gate,
          onorm_g, lnx_g, lnx_b, bd_ones, w_gate, branch_proj, w_out, tm):
    m, d = x2d.shape
    bw = y_sc.shape[1]
    row = lambda width: pl.BlockSpec((tm, width), lambda i: (i, 0))
    full = lambda a: pl.BlockSpec(a.shape, lambda i: (0,) * a.ndim)
    gate_blk = hg_col0 // bw + 4
    hg_gate = pl.BlockSpec((tm, bw), lambda i: (i, gate_blk))
    consts = (onorm_g, lnx_g, lnx_b, bd_ones, w_gate, branch_proj, w_out)
    return pl.pallas_call(
        _merge_kernel,
        grid=(m // tm,),
        in_specs=[row(d), full(g), row(1), row(bw), row(bw), hg_gate] + [row(bw)] * 6
                 + [full(a) for a in consts],
        out_specs=row(d),
        out_shape=jax.ShapeDtypeStruct((m, d), F32),
        compiler_params=_params(("parallel",)),
        name="merge",
    )(x2d, g, valid, hg_of, hg_ob, z_hg2d, y_sc, y_da, rw_of, rw_ob, bonus, rw_gate, *consts)


def _mlp_kernel(x_ref, g_ref, w1_ref, w2_ref, o_ref, h_scr, acc_scr):
    f = pl.program_id(1)

    @pl.when(f == 0)
    def _():
        x = x_ref[...]
        ms = jnp.mean(x * x, axis=-1, keepdims=True)
        h_scr[...] = (x * lax.rsqrt(ms + NORM_EPS) * g_ref[...]).astype(BF16)
        acc_scr[...] = jnp.zeros_like(acc_scr)

    a = jnp.maximum(_dot(h_scr[...], w1_ref[...]), 0.0)
    acc_scr[...] += _dot((a * a).astype(BF16), w2_ref[...])

    @pl.when(f == pl.num_programs(1) - 1)
    def _():
        o_ref[...] = x_ref[...] + acc_scr[...]


def mlp(x2d, g, w1, w2, tm, tf):
    m, d = x2d.shape
    dff = w1.shape[1]
    return pl.pallas_call(
        _mlp_kernel,
        grid=(m // tm, dff // tf),
        in_specs=[pl.BlockSpec((tm, d), lambda i, f: (i, 0)),
                  pl.BlockSpec((1, d), lambda i, f: (0, 0)),
                  pl.BlockSpec((d, tf), lambda i, f: (0, f)),
                  pl.BlockSpec((tf, d), lambda i, f: (f, 0))],
        out_specs=pl.BlockSpec((tm, d), lambda i, f: (i, 0)),
        out_shape=jax.ShapeDtypeStruct((m, d), F32),
        scratch_shapes=[pltpu.VMEM((tm, d), BF16), pltpu.VMEM((tm, d), F32)],
        compiler_params=_params(("parallel", "arbitrary")),
        name="mlp",
    )(x2d, g, w1, w2)


def _rope_tables(p):
    half = ROPE_DIM // 2
    pos = jnp.arange(p, dtype=F32) - FRONT_PAD
    inv = ROPE_THETA ** (-jnp.arange(half, dtype=F32) / half)
    ang = pos[:, None] * inv[None, :]
    cos, sin = jnp.cos(ang), jnp.sin(ang)
    ones = jnp.ones((p, DA_QK_DIM - ROPE_DIM), F32)
    zeros = jnp.zeros((p, DA_QK_DIM - ROPE_DIM), F32)
    zh = jnp.zeros((p, half), F32)
    c = jnp.concatenate([cos, cos, ones], axis=1)
    s1 = jnp.concatenate([-sin, zh, zeros], axis=1)
    s2 = jnp.concatenate([zh, sin, zeros], axis=1)
    return tuple(jnp.tile(t, (1, 2)) for t in (c, s1, s2))


def _block_diag_ones(n, blk):
    i = jnp.arange(n)
    return ((i[:, None] // blk) == (i[None, :] // blk)).astype(F32)


def kernel(x_prompt, x_sample, meta_tokens, norm_mix_g, w_in, hgrn_lb_logits, hgrn_onorm_g, conv_w,
           diff_qnorm_g, diff_knorm_g, diff_lambda, diff_subln_g, rwkv_mu, rwkv_w0, rwkv_w2, rwkv_a0,
           rwkv_a2, rwkv_g2, rwkv_k_k, rwkv_k_a, rwkv_r_k, rwkv_lnx_g, rwkv_lnx_b, w_gate,
           branch_proj, w_out, norm_mlp_g, mlp_w1, mlp_w2):
    assert x_prompt.shape[1:] == x_sample.shape[1:]
    x = jnp.concatenate([x_prompt, x_sample], axis=0)
    bt, seq, d = x.shape
    depth = w_in.shape[0]
    length = N_META + seq
    p = -(-(FRONT_PAD + length) // V7X_LANES) * V7X_LANES
    m = bt * p
    bw = branch_proj.shape[2]
    rw_cols = rwkv_mu.shape[1]
    sizes = (5 * bw, 3 * bw, 3 * bw, rw_cols)
    offs = [0]
    for s in sizes:
        offs.append(offs[-1] + s)
    rw_pad = -(-rw_cols // bw) * bw
    c_hg, c_sc, c_da = rw_pad, rw_pad + sizes[0], rw_pad + sizes[0] + sizes[1]
    zw = c_da + sizes[2]

    meta = jnp.broadcast_to(meta_tokens.astype(x.dtype)[None], (bt, N_META, d))
    xp = jnp.concatenate([jnp.zeros((bt, FRONT_PAD, d), x.dtype), meta, x,
                          jnp.zeros((bt, p - FRONT_PAD - length, d), x.dtype)], axis=1)
    x2d = xp.reshape(m, d)

    rows = jnp.arange(p)
    valid_seq = ((rows >= FRONT_PAD) & (rows < FRONT_PAD + length)).astype(F32)[:, None]
    valid = jnp.tile(valid_seq, (bt, 1))
    key_bias = jnp.where(valid_seq[:, 0] > 0, 0.0, NEG_BIG).astype(F32)[None, :]
    tables = _rope_tables(p)
    bd64_mean = (_block_diag_ones(V7X_LANES, DA_QK_DIM) / DA_QK_DIM).astype(BF16)
    bd64_ones = _block_diag_ones(V7X_LANES, RW_HEAD_DIM).astype(BF16)

    sm = jax.nn.softmax(hgrn_lb_logits.astype(F32), axis=1)
    lb_all = jnp.cumsum(sm, axis=1) - sm[:, :1]

    tm_proj = _pick(m, 1536, 128)
    tm_merge = _pick(m, 256, 128)
    tm_mlp = _pick(m, 768, 128)
    tq = _pick(p, 384, 128)
    tk = _pick(p, 384, 128)
    assert FRONT_PAD <= tk and p - (FRONT_PAD + length) <= tk
    tr = _pick(p, 384, 8)

    half = V7X_LANES // 2
    for l in range(depth):
        g_mix = norm_mix_g[l][None, :]
        w_l = jnp.concatenate([w_in[l][:, offs[3]:], jnp.zeros((d, rw_pad - rw_cols), w_in.dtype),
                               w_in[l][:, :offs[3]]], axis=1).astype(BF16)
        z = norm_matmul(x2d, g_mix, valid, w_l, tm_proj, _pick(zw, 768, V7X_LANES)).reshape(bt, p, zw)

        y_sc = shortconv(z, c_sc, conv_w[l])

        lam_init = 0.8 - 0.6 * math.exp(-0.3 * l)
        zpad = jnp.zeros((2, half, bw), F32)
        w2pad = jnp.stack([jnp.concatenate([rwkv_w2[l, 0], zpad[0]], 0),
                           jnp.concatenate([zpad[0], rwkv_w2[l, 1]], 0)])
        a2pad = jnp.stack([jnp.concatenate([rwkv_a2[l, 0], zpad[0]], 0),
                           jnp.concatenate([zpad[0], rwkv_a2[l, 1]], 0)])
        y_da = diff_attention(z, c_da, tables, bd64_mean,
                              jnp.tile(diff_qnorm_g[l], 2)[None, :], jnp.tile(diff_knorm_g[l], 2)[None, :],
                              diff_lambda[l], diff_subln_g[l][None, :], key_bias, lam_init, tq, tk)
        (r, v, nkk, lwf, bef, kdf, lwb, beb, kdb, gate, bonus) = rwkv_prep(
            z, valid_seq, rwkv_mu[l][None, :], rwkv_w0[l], w2pad, rwkv_a0[l], a2pad, rwkv_g2[l],
            rwkv_k_k[l][None, :], rwkv_k_a[l][None, :], rwkv_r_k[l].reshape(1, bw), bd64_ones, tr)
        of, ob, orf, orb = scans(z, c_hg, valid_seq, lb_all[:, l], r, v, nkk, lwf, bef, kdf, lwb, beb,
                                 kdb, _pick(p // RW_CHUNK, SCAN_CHUNKS_PER_STEP, 1))

        flat = lambda a: a.reshape(m, a.shape[-1])
        x2d = merge(x2d, g_mix, valid, flat(of), flat(ob), flat(z), c_hg, flat(y_sc), flat(y_da),
                    flat(orf), flat(orb), flat(bonus), flat(gate), hgrn_onorm_g[l][None, :],
                    rwkv_lnx_g[l][None, :], rwkv_lnx_b[l][None, :], bd64_ones,
                    w_gate[l].astype(BF16), branch_proj[l].astype(BF16), w_out[l].astype(BF16), tm_merge)
        x2d = mlp(x2d, norm_mlp_g[l][None, :], mlp_w1[l].astype(BF16), mlp_w2[l].astype(BF16),
                  tm_mlp, _pick(mlp_w1.shape[2], 1024, V7X_LANES))

    y = x2d.reshape(bt, p, d)[:, FRONT_PAD + N_META:FRONT_PAD + length]
    nb = x_prompt.shape[0]
    return (y[:nb], y[nb:])
```

```python
import functools
import math

import jax
import jax.numpy as jnp
from jax import lax
from jax.experimental import pallas as pl
from jax.experimental.pallas import tpu as pltpu

F32 = jnp.float32
BF16 = jnp.bfloat16
HI = lax.Precision.HIGHEST

V7X_LANES = 128
V7X_VMEM_BYTES = 64 * 1024 * 1024
VMEM_LIMIT = V7X_VMEM_BYTES - 8 * 1024 * 1024

N_META = 16
NORM_EPS = 1e-6
N_BRANCH = 4
HG_HEAD_DIM = 128
HG_CHUNK = 64
HG_SUB = 16
LB_FLOOR = 1e-20
SC_KSIZE = 3
DA_HEADS = 4
DA_QK_DIM = 64
DA_V_DIM = 128
ROPE_THETA = 500000.0
ROPE_DIM = DA_QK_DIM // 4
SUBLN_EPS = 1e-5
RW_HEAD_DIM = 64
RW_CHUNK = 64
RW_LNX_EPS = 64e-5
FRONT_PAD = (-N_META) % HG_CHUNK
NEG_BIG = -1e30
LOG2_E = 1.4426950408889634

P_HG = "bf16"
P_RW_LR = "bf16"
SCAN_CHUNKS_PER_STEP = 3


def _split_bf16(a):
    hi = a.astype(BF16)
    return hi, (a - hi.astype(F32)).astype(BF16)


def _mm(a, b, dims, precision):
    dg = functools.partial(lax.dot_general, dimension_numbers=(dims, ((), ())),
                           preferred_element_type=F32)
    if precision == "bf16":
        return dg(a.astype(BF16), b.astype(BF16))
    if precision == "x3":
        ah, al = _split_bf16(a)
        bh, bl = _split_bf16(b)
        return dg(ah, bh) + (dg(al, bh) + dg(ah, bl))
    return dg(a, b, precision=precision)


def _dot(a, b, precision=None):
    return _mm(a, b, ((1,), (0,)), precision)


def _dot_split_rhs3(a_bf16, b):
    b1 = b.astype(BF16)
    r1 = b - b1.astype(F32)
    b2 = r1.astype(BF16)
    b3 = (r1 - b2.astype(F32)).astype(BF16)
    return _dot(a_bf16, b1) + (_dot(a_bf16, b2) + _dot(a_bf16, b3))


def _dot_split_lhs(a, b_bf16):
    hi, lo = _split_bf16(a)
    return _dot(hi, b_bf16) + _dot(lo, b_bf16)


def _dot_nt(a, b, precision=None):
    return _mm(a, b, ((1,), (1,)), precision)


def _dot_tn(a, b, precision=None):
    return _mm(a, b, ((0,), (0,)), precision)


def _pick(n, target, mult):
    best = None
    for d in range(mult, min(n, target) + 1, mult):
        if n % d == 0:
            best = d
    assert best is not None, (n, target, mult)
    return best


def _params(sem):
    return pltpu.CompilerParams(dimension_semantics=sem, vmem_limit_bytes=VMEM_LIMIT)


def _log_sigmoid(x):
    return jnp.minimum(x, 0.0) - jnp.log1p(jnp.exp(-jnp.abs(x)))


def _sigmoid(x):
    return 1.0 / (1.0 + jnp.exp(-x))


def _masked_rms_norm(x, g, valid):
    ms = jnp.mean(x * x, axis=-1, keepdims=True)
    h = x * lax.rsqrt(ms + NORM_EPS) * g
    return jnp.where(valid > 0.0, h, 0.0)


def _norm_matmul_kernel(x_ref, g_ref, valid_ref, w_ref, o_ref, h_scr):
    @pl.when(pl.program_id(1) == 0)
    def _():
        h_scr[...] = _masked_rms_norm(x_ref[...], g_ref[...], valid_ref[...]).astype(BF16)

    o_ref[...] = _dot(h_scr[...], w_ref[...])


def norm_matmul(x2d, g, valid, w_bf16, tm, tn):
    m, d = x2d.shape
    n = w_bf16.shape[1]
    return pl.pallas_call(
        _norm_matmul_kernel,
        grid=(m // tm, n // tn),
        in_specs=[
            pl.BlockSpec((tm, d), lambda i, j: (i, 0)),
            pl.BlockSpec((1, d), lambda i, j: (0, 0)),
            pl.BlockSpec((tm, 1), lambda i, j: (i, 0)),
            pl.BlockSpec((d, tn), lambda i, j: (0, j)),
        ],
        out_specs=pl.BlockSpec((tm, tn), lambda i, j: (i, j)),
        out_shape=jax.ShapeDtypeStruct((m, n), F32),
        scratch_shapes=[pltpu.VMEM((tm, d), BF16)],
        compiler_params=_params(("parallel", "arbitrary")),
        name="norm_matmul",
    )(x2d, g, valid, w_bf16)


def _shortconv_kernel(b_ref, c_ref, h_ref, w_ref, o_ref):
    u = c_ref[0] * h_ref[0]
    p = u.shape[0]
    w = w_ref[...]
    y = pltpu.roll(u, 1, 0) * w[0:1] + u * w[1:2] + pltpu.roll(u, p - 1, 0) * w[2:3]
    o_ref[0] = b_ref[0] * y


def shortconv(z, col0, conv_w):
    bt, p, _ = z.shape
    width = conv_w.shape[1]
    nb = width // V7X_LANES
    c0 = col0 // V7X_LANES
    blk = (1, p, V7X_LANES)
    z_sc = z
    return pl.pallas_call(
        _shortconv_kernel,
        grid=(bt, nb),
        in_specs=[
            pl.BlockSpec(blk, lambda b, c: (b, 0, c0 + c)),
            pl.BlockSpec(blk, lambda b, c: (b, 0, c0 + nb + c)),
            pl.BlockSpec(blk, lambda b, c: (b, 0, c0 + 2 * nb + c)),
            pl.BlockSpec((SC_KSIZE, V7X_LANES), lambda b, c: (0, c)),
        ],
        out_specs=pl.BlockSpec(blk, lambda b, c: (b, 0, c)),
        out_shape=jax.ShapeDtypeStruct((bt, p, width), F32),
        compiler_params=_params(("parallel", "parallel")),
        name="shortconv",
    )(z_sc, z_sc, z_sc, conv_w)


def _attn_kernel(q_ref, k_ref, v_ref, cq_ref, s1q_ref, s2q_ref, ck_ref, s1k_ref, s2k_ref,
                 bd_ref, qg_ref, kg_ref, lam_ref, sg_ref, bias_ref, o_ref, k_scr, v_scr, s_scr,
                 *, lam_init, tk):
    bd = bd_ref[...]

    def norm_rope(x, g, c, s1, s2):
        ms = _dot_split_lhs(x * x, bd)
        y = x * lax.rsqrt(ms + NORM_EPS) * g
        return (y * c + pltpu.roll(y, V7X_LANES - ROPE_DIM // 2, 1) * s1
                + pltpu.roll(y, ROPE_DIM // 2, 1) * s2)

    @pl.when(pl.program_id(2) == 0)
    def _():
        k = norm_rope(k_ref[0], kg_ref[...], ck_ref[...], s1k_ref[...], s2k_ref[...])
        k_scr[...] = k.astype(BF16)
        v_scr[...] = v_ref[0].T.astype(BF16)

    q = norm_rope(q_ref[0], qg_ref[...], cq_ref[...], s1q_ref[...], s2q_ref[...])
    q = q * (DA_QK_DIM ** -0.5 * LOG2_E)
    tq = q.shape[0]
    first = lax.broadcasted_iota(jnp.int32, q.shape, 1) < DA_QK_DIM
    lp = lam_ref[...]
    lam = (jnp.exp(jnp.sum(lp[0:1] * lp[1:2], axis=-1, keepdims=True))
           - jnp.exp(jnp.sum(lp[2:3] * lp[3:4], axis=-1, keepdims=True)) + lam_init)
    qst = jnp.concatenate([jnp.where(first, q, 0.0), jnp.where(first, 0.0, q)], axis=0).T.astype(BF16)
    nk = k_scr.shape[0] // tk
    slabs = range(0, tk, 8)
    mrow = None
    for j in range(nk):
        rows = slice(j * tk, (j + 1) * tk)
        s_j = _dot(k_scr[rows, :], qst)
        if j in (0, nk - 1):
            s_j = s_j + bias_ref[rows, :]
        s_scr[rows, :] = s_j
        mt = functools.reduce(jnp.maximum, [s_j[r:r + 8] for r in slabs])
        mrow = mt if mrow is None else jnp.maximum(mrow, mt)
    m = jnp.max(mrow, axis=0, keepdims=True)
    lrow = jnp.zeros_like(mrow)
    o2 = jnp.zeros((DA_V_DIM, 2 * tq), F32)
    for j in range(nk):
        rows = slice(j * tk, (j + 1) * tk)
        pr = jnp.exp2(s_scr[rows, :] - m)
        lrow = lrow + functools.reduce(jnp.add, [pr[r:r + 8] for r in slabs])
        o2 = o2 + _dot(v_scr[:, rows], pr.astype(BF16))
    o2 = o2 * (1.0 / jnp.sum(lrow, axis=0, keepdims=True))
    o = o2[:, :tq] - lam * o2[:, tq:]
    ms = jnp.mean(o * o, axis=0, keepdims=True)
    o_ref[0] = (o * lax.rsqrt(ms + SUBLN_EPS) * sg_ref[...] * (1.0 - lam_init)).T


def _alternate(streams):
    streams = list(streams)
    while streams:
        for g in list(streams):
            if next(g, StopIteration) is StopIteration:
                streams.remove(g)


def diff_attention(z, col0, tables, bd64, qn_g, kn_g, lam_p, subln_g, key_bias, lam_init, tq, tk):
    bt, p, _ = z.shape
    z_da = z
    c_t, s1_t, s2_t = tables
    nh = DA_HEADS
    c0 = col0 // V7X_LANES
    qspec = pl.BlockSpec((1, tq, V7X_LANES), lambda b, h, i: (b, i, c0 + h))
    kspec = pl.BlockSpec((1, p, V7X_LANES), lambda b, h, i: (b, 0, c0 + nh + h))
    vspec = pl.BlockSpec((1, p, V7X_LANES), lambda b, h, i: (b, 0, c0 + 2 * nh + h))
    tq_spec = pl.BlockSpec((tq, V7X_LANES), lambda b, h, i: (i, 0))
    tk_spec = pl.BlockSpec((p, V7X_LANES), lambda b, h, i: (0, 0))
    full = lambda shape: pl.BlockSpec(shape, lambda b, h, i: (0,) * len(shape))
    return pl.pallas_call(
        functools.partial(_attn_kernel, lam_init=lam_init, tk=tk),
        grid=(bt, nh, p // tq),
        in_specs=[qspec, kspec, vspec, tq_spec, tq_spec, tq_spec, tk_spec, tk_spec, tk_spec,
                  full((V7X_LANES, V7X_LANES)), full((1, V7X_LANES)), full((1, V7X_LANES)),
                  full((4, DA_QK_DIM)), full((DA_V_DIM, 1)), full((p, 1))],
        out_specs=pl.BlockSpec((1, tq, V7X_LANES), lambda b, h, i: (b, i, h)),
        out_shape=jax.ShapeDtypeStruct((bt, p, nh * DA_V_DIM), F32),
        scratch_shapes=[pltpu.VMEM((p, V7X_LANES), BF16), pltpu.VMEM((DA_V_DIM, p), BF16),
                        pltpu.VMEM((p, 2 * tq), F32)],
        compiler_params=_params(("parallel", "parallel", "arbitrary")),
        name="diff_attention",
    )(z_da, z_da, z_da, c_t, s1_t, s2_t, c_t, s1_t, s2_t, bd64, qn_g, kn_g, lam_p, subln_g,
      key_bias)


def _gla_local(q, qt, qh, k, kh, v, b, bpiv, reverse):
    c, sub, half = HG_CHUNK, HG_SUB, HG_SUB // 2
    nsub = c // sub
    row = lax.broadcasted_iota(jnp.int32, (c, 1), 0)
    attn = jnp.zeros((c, c), F32)
    for i in range(nsub):
        if (reverse and i == nsub - 1) or (not reverse and i == 0):
            continue
        bi = bpiv[i * sub:i * sub + 1]
        kmask = (row >= (i + 1) * sub) if reverse else (row < i * sub)
        kt = jnp.where(kmask, k * jnp.exp(jnp.minimum(bi - b, 0.0)), 0.0)
        attn = jnp.where(row // sub == i, _dot_nt(qt, kt, P_HG), attn)
    t_idx = lax.broadcasted_iota(jnp.int32, (c, c), 0)
    s_idx = lax.broadcasted_iota(jnp.int32, (c, c), 1)
    attn = jnp.where(t_idx // sub == s_idx // sub, _dot_nt(qh, kh, P_HG), attn)

    b2 = b * LOG2_E
    tloc = lax.broadcasted_iota(jnp.int32, (half, c), 0)
    scol = lax.broadcasted_iota(jnp.int32, (half, c), 1)
    blocks = []
    for i in range(c // half):
        bt, qi = b2[i * half:(i + 1) * half], q[i * half:(i + 1) * half]
        blk = attn[i * half:(i + 1) * half]
        for j in range(half):
            s = i * half + j
            a = jnp.sum(qi * k[s:s + 1] * jnp.exp2(bt - b2[s:s + 1]), axis=-1, keepdims=True)
            keep = (scol == s) & ((tloc <= j) if reverse else (tloc >= j))
            blk = jnp.where(keep, a, blk)
        blocks.append(blk)
    return _dot(jnp.concatenate(blocks, axis=0), v, P_HG)


def _hgrn_body(qf_ref, ff_ref, vf_ref, mf_ref, qb_ref, fb_ref, vb_ref, mb_ref, lb_ref,
               of_ref, ob_ref, st_scr, *, nc):
    c, sub, half = HG_CHUNK, HG_SUB, HG_SUB // 2
    nh = st_scr.shape[1]
    t_idx = lax.broadcasted_iota(jnp.int32, (c, c), 0)
    s_idx = lax.broadcasted_iota(jnp.int32, (c, c), 1)
    second = (lax.broadcasted_iota(jnp.int32, (c, 1), 0) % sub) >= half
    dirs = ((qf_ref, ff_ref, vf_ref, mf_ref, of_ref), (qb_ref, fb_ref, vb_ref, mb_ref, ob_ref))
    units = []
    for d, (q_ref, f_ref, v_ref, m_ref, _) in enumerate(dirs):
        reverse = d == 1
        if reverse:
            mats = (s_idx >= t_idx, s_idx >= (t_idx // sub + 1) * sub,
                    s_idx >= (t_idx // half + 1) * half, s_idx >= (t_idx // half) * half)
        else:
            mats = (s_idx <= t_idx, s_idx < (t_idx // sub) * sub,
                    s_idx < (t_idx // half) * half, s_idx < (t_idx // half + 1) * half)
        cmat = jnp.concatenate([x.astype(F32) for x in mats], axis=0).astype(BF16)
        q_half = jnp.logical_not(second) if reverse else second
        lb = lb_ref[d:d + 1, :]
        la = jnp.log(jnp.maximum(lb, LB_FLOOR))
        l1 = jnp.log1p(-lb)
        for j in range(nc):
            rows = slice(j * c, (j + 1) * c)
            valid = m_ref[rows, :] > 0.0
            lc = l1 + _log_sigmoid(f_ref[0, rows, :])
            logf = jnp.maximum(la, lc) + jnp.log1p(jnp.exp(-jnp.abs(la - lc)))
            logf = jnp.where(valid, logf, 0.0)
            k = jnp.where(valid, 1.0 - jnp.exp(logf), 0.0)
            cum = _dot_split_rhs3(cmat, logf)
            b, bpiv, bhin, bhout = (cum[i * c:(i + 1) * c] for i in range(4))
            btot = b[0:1] if reverse else b[c - 1:c]
            q, v = q_ref[0, rows, :], v_ref[0, rows, :]
            qt = q * jnp.exp(b - bpiv)
            qh = jnp.where(q_half, q * jnp.exp(b - bhin), 0.0)
            kh = jnp.where(q_half, 0.0, k * jnp.exp(bhout - b))
            qe = (q * jnp.exp(b)).astype(BF16)
            k2 = k * jnp.exp(btot - b)
            dec = jnp.exp(btot)
            for h in range(nh):
                hs = slice(h * HG_HEAD_DIM, (h + 1) * HG_HEAD_DIM)
                units.append(dict(
                    d=d, j=j, h=h, rows=rows, hs=hs, qe=qe[:, hs], dec=dec[:, hs],
                    o=_gla_local(q[:, hs], qt[:, hs], qh[:, hs], k[:, hs], kh[:, hs], v[:, hs],
                                 b[:, hs], bpiv[:, hs], reverse),
                    kv=_dot_tn(v[:, hs], k2[:, hs], P_HG)))
                yield

    state = {(d, h): st_scr[d, h] for d in range(2) for h in range(nh)}
    for step in range(nc):
        for u in units:
            if u["j"] != (nc - 1 - step if u["d"] == 1 else step):
                continue
            key = (u["d"], u["h"])
            dirs[u["d"]][4][0, u["rows"], u["hs"]] = u["o"] + _dot_nt(u["qe"], state[key].astype(BF16))
            state[key] = state[key] * u["dec"] + u["kv"]
        yield
    for (d, h), s in state.items():
        st_scr[d, h] = s


def _group_sum(x, bd):
    cols = [_dot_split_lhs(x[:, j:j + V7X_LANES], bd) for j in range(0, x.shape[1], V7X_LANES)]
    return jnp.concatenate(cols, axis=1)


def _rwkv_prep_kernel(z_ref, zp_ref, zn_ref, valid_ref, mu_ref, w0_ref, w2_ref, a0_ref, a2_ref,
                      g2_ref, kk_ref, ka_ref, rk_ref, bd_ref,
                      r_out, v_out, nkk_out, lwf_out, bef_out, kdf_out, lwb_out, beb_out, kdb_out,
                      gate_out, bonus_out):
    u = z_ref[0]
    tr = u.shape[0]
    width = r_out.shape[-1]
    row = lax.broadcasted_iota(jnp.int32, (8, 1), 0)
    down, up = pltpu.roll(u, 1, 0), pltpu.roll(u, tr - 1, 0)
    u_prev = jnp.concatenate([jnp.where(row == 0, zp_ref[0][7:8], down[:8]), down[8:]], axis=0)
    u_next = jnp.concatenate([up[:tr - 8], jnp.where(row == 7, zn_ref[0][0:1], up[tr - 8:])], axis=0)
    mu = mu_ref[...]
    xm = u * (1.0 - mu) + (u_prev + u_next) * (0.5 * mu)
    valid = valid_ref[...] > 0.0
    bd = bd_ref[...]
    r = xm[:, 0:width]
    k = xm[:, width:2 * width]
    v = jnp.where(valid, xm[:, 2 * width:3 * width], 0.0)
    lr = 3 * width
    wl = jnp.tanh(xm[:, lr:lr + V7X_LANES])
    al = xm[:, lr + V7X_LANES:lr + 2 * V7X_LANES]
    gl = xm[:, lr + 2 * V7X_LANES:lr + 3 * V7X_LANES]
    kk = k * kk_ref[...]
    kk = kk / jnp.maximum(jnp.sqrt(_group_sum(kk * kk, bd)), 1e-12)
    kk = jnp.where(valid, kk, 0.0)
    r_out[0] = r.astype(BF16)
    v_out[0] = v.astype(BF16)
    nkk_out[0] = (-kk).astype(BF16)
    kd_sum = jnp.zeros_like(k)
    for d, (lw_out, be_out, kd_out) in enumerate(((lwf_out, bef_out, kdf_out),
                                                   (lwb_out, beb_out, kdb_out))):
        wlog = _log_sigmoid(w0_ref[d:d + 1, :] + _dot(wl, w2_ref[d], P_RW_LR)) - 0.5
        lw_out[0] = jnp.where(valid, -jnp.exp(wlog), 0.0)
        a = _sigmoid(a0_ref[d:d + 1, :] + _dot(al, a2_ref[d], P_RW_LR))
        kd = k * (1.0 + (a - 1.0) * ka_ref[...])
        kd_sum = kd_sum + kd
        kd_out[0] = jnp.where(valid, kd, 0.0).astype(BF16)
        be_out[0] = (kk * a).astype(BF16)
    gate_out[0] = _dot(_sigmoid(gl), g2_ref[...], P_RW_LR).astype(BF16)
    bonus_out[0] = (_group_sum(r * kd_sum * rk_ref[...], bd) * v).astype(BF16)


def rwkv_prep(z_rw, valid2d, mu, w0, w2pad, a0, a2pad, g2, k_k, k_a, r_k, bd_ones, tr):
    bt, p, _ = z_rw.shape
    cols = mu.shape[1]
    width = w0.shape[1]
    nt = p // tr
    r8 = tr // 8
    last8 = p // 8 - 1
    full = lambda a: pl.BlockSpec(a.shape, lambda b, i: (0,) * a.ndim)
    outs = [jax.ShapeDtypeStruct((bt, p, width), F32 if i in (3, 6) else BF16) for i in range(11)]
    ospec = pl.BlockSpec((1, tr, width), lambda b, i: (b, i, 0))
    params = (mu, w0, w2pad, a0, a2pad, g2, k_k, k_a, r_k, bd_ones)
    return pl.pallas_call(
        _rwkv_prep_kernel,
        grid=(bt, nt),
        in_specs=[pl.BlockSpec((1, tr, cols), lambda b, i: (b, i, 0)),
                  pl.BlockSpec((1, 8, cols), lambda b, i: (b, jnp.maximum(i * r8 - 1, 0), 0)),
                  pl.BlockSpec((1, 8, cols), lambda b, i: (b, jnp.minimum((i + 1) * r8, last8), 0)),
                  pl.BlockSpec((tr, 1), lambda b, i: (i, 0))] + [full(a) for a in params],
        out_specs=[ospec] * 11,
        out_shape=outs,
        compiler_params=_params(("parallel", "parallel")),
        name="rwkv_prep",
    )(z_rw, z_rw, z_rw, valid2d, *params)


def _tri_inverse_all(l_list, eye):
    n = range(len(l_list))
    m = l_list[0].shape[0]
    l_hi_lo = [_split_bf16(l) for l in l_list]
    pw = [hl[0] for hl in l_hi_lo]
    tinv = [eye + l for l in l_list]
    levels = int(math.log2(RW_CHUNK))
    nxt = [_dot(pw[u], pw[u]).astype(BF16) for u in n]
    yield
    for j in range(1, levels):
        pw = nxt
        if j < levels - 1:
            both = [_dot(jnp.concatenate([pw[u], tinv[u].astype(BF16)], axis=0), pw[u]) for u in n]
            nxt = [both[u][:m].astype(BF16) for u in n]
            tinv = [tinv[u] + both[u][m:] for u in n]
        else:
            tinv = [tinv[u] + _dot(tinv[u].astype(BF16), pw[u]) for u in n]
        yield
    t_hi_lo = [_split_bf16(t) for t in tinv]
    lt = [_dot(jnp.concatenate(l_hi_lo[u], axis=0), t_hi_lo[u][0]) for u in n]
    yield
    resid = [(eye - tinv[u]) + (lt[u][:m] + (lt[u][m:] + _dot(l_hi_lo[u][0], t_hi_lo[u][1])))
             for u in n]
    yield
    return [tinv[u] + _dot(t_hi_lo[u][0], resid[u].astype(BF16)) for u in n]


def _rwkv_body(rf_ref, vf_ref, af_ref, lwf_ref, bef_ref, kdf_ref,
               rb_ref, vb_ref, ab_ref, lwb_ref, beb_ref, kdb_ref,
               of_ref, ob_ref, h_scr, *, nc):
    c = RW_CHUNK
    n2 = 2 * c
    npair = h_scr.shape[1]
    t_idx = lax.broadcasted_iota(jnp.int32, (c, c), 0)
    s_idx = lax.broadcasted_iota(jnp.int32, (c, c), 1)
    ti = lax.broadcasted_iota(jnp.int32, (n2, n2), 0)
    si = lax.broadcasted_iota(jnp.int32, (n2, n2), 1)
    same = (ti // c) == (si // c)
    tl, sl = ti % c, si % c
    eye = jnp.where(ti == si, 1.0, 0.0)
    first = lax.broadcasted_iota(jnp.int32, (c, V7X_LANES), 1) < RW_HEAD_DIM

    def stack(x):
        return jnp.concatenate([jnp.where(first, x, 0.0), jnp.where(first, 0.0, x)], axis=0)

    dirs = ((rf_ref, vf_ref, af_ref, lwf_ref, bef_ref, kdf_ref, of_ref),
            (rb_ref, vb_ref, ab_ref, lwb_ref, beb_ref, kdb_ref, ob_ref))
    masks = []
    for reverse in (False, True):
        masks.append((((s_idx >= t_idx) if reverse else (s_idx <= t_idx)).astype(F32).astype(BF16),
                      same & ((sl > tl) if reverse else (sl < tl)),
                      same & ((sl >= tl) if reverse else (sl <= tl))))

    units = []
    for step in range(nc):
        for d, (r_ref, v_ref, a_ref, lw_ref, be_ref, kd_ref, _) in enumerate(dirs):
            reverse = d == 1
            tri, strict, incl = masks[d]
            j = nc - 1 - step if reverse else step
            rows = slice(j * c, (j + 1) * c)
            lw = lw_ref[0, rows, :]
            lg = _dot_split_rhs3(tri, lw)
            ltot = lg[0:1] if reverse else lg[c - 1:c]
            ginv = jnp.exp(-lg)
            gend = jnp.exp(ltot - lg)
            be, kd = be_ref[0, rows, :], kd_ref[0, rows, :]
            cols = (a_ref[0, rows, :] * jnp.exp(lg - lw), r_ref[0, rows, :] * jnp.exp(lg),
                    be * ginv, kd * ginv, be * gend, kd * gend, v_ref[0, rows, :])
            gtot = jnp.exp(ltot)
            for p in range(npair):
                ps = slice(p * V7X_LANES, (p + 1) * V7X_LANES)
                a_s, r_s, b_s, k_s, bh_s, kh_s, v_s = (stack(x[:, ps]).astype(BF16) for x in cols)
                units.append(dict(step=step, d=d, p=p, rows=rows, ps=ps, strict=strict, incl=incl,
                                  ar=jnp.concatenate([a_s, r_s], axis=0),
                                  bk=jnp.concatenate([b_s, k_s], axis=0),
                                  bh=bh_s, kh=kh_s, v=v_s, gtot=gtot[:, ps]))
            yield
    for u in units:
        sc = _dot_nt(u["ar"], u["bk"])
        u["l_ab"] = jnp.where(u["strict"], sc[:n2, :n2], 0.0)
        u["l_ak"] = jnp.where(u["strict"], sc[:n2, n2:], 0.0).astype(BF16)
        u["m_rb"] = jnp.where(u["incl"], sc[n2:, :n2], 0.0).astype(BF16)
        u["m_rk"] = jnp.where(u["incl"], sc[n2:, n2:], 0.0).astype(BF16)
    yield
    tinvs = yield from _tri_inverse_all([u["l_ab"] for u in units], eye)
    for u, tinv in zip(units, tinvs):
        u["tinv"] = tinv.astype(BF16)
    yield
    for u in units:
        both = _dot(jnp.concatenate([u["l_ak"], u["m_rk"]], axis=0), u["v"])
        u["lakv"], u["mv"] = both[:n2], both[n2:]
        u["kv"] = _dot_tn(u["v"], u["kh"])
    yield

    state = {(d, p): h_scr[d, p] for d in range(2) for p in range(npair)}
    for step in range(nc):
        live = [u for u in units if u["step"] == step]
        for u in live:
            u["ah"] = _dot_nt(u["ar"], state[u["d"], u["p"]].astype(BF16))
        yield
        for u in live:
            u["u"] = _dot(u["tinv"], (u["ah"][:n2] + u["lakv"]).astype(BF16)).astype(BF16)
        yield
        for u in live:
            o_s = u["ah"][n2:] + u["mv"] + _dot(u["m_rb"], u["u"])
            dirs[u["d"]][6][0, u["rows"], u["ps"]] = o_s[:c] + o_s[c:]
            key = (u["d"], u["p"])
            state[key] = state[key] * u["gtot"] + u["kv"] + _dot_tn(u["u"], u["bh"])
        yield
    for (d, p), h in state.items():
        h_scr[d, p] = h


N_HGRN_IN = 9
N_RWKV_IN = 12


def _scans_kernel(*refs, nc):
    hg_in = refs[:N_HGRN_IN]
    rw_in = refs[N_HGRN_IN:N_HGRN_IN + N_RWKV_IN]
    hg_of, hg_ob, rw_of, rw_ob, st_scr, h_scr = refs[N_HGRN_IN + N_RWKV_IN:]

    @pl.when(pl.program_id(1) == 0)
    def _():
        st_scr[...] = jnp.zeros_like(st_scr)
        h_scr[...] = jnp.zeros_like(h_scr)

    _alternate([_rwkv_body(*rw_in, rw_of, rw_ob, h_scr, nc=nc),
                _hgrn_body(*hg_in, hg_of, hg_ob, st_scr, nc=nc)])


def scans(z_hg, hg_col0, valid_seq, lb, r, v, nkk, lw_f, be_f, kd_f, lw_b, be_b, kd_b, nc):
    assert HG_CHUNK == RW_CHUNK
    bt, p, width = r.shape
    nh = width // HG_HEAD_DIM
    rows = nc * HG_CHUNK
    n = p // rows
    blk = (1, rows, width)

    def spec(group, rev):
        if rev:
            return pl.BlockSpec(blk, lambda b, i: (b, n - 1 - i, group))
        return pl.BlockSpec(blk, lambda b, i: (b, i, group))

    fwd, bwd = spec(0, False), spec(0, True)
    mf = pl.BlockSpec((rows, 1), lambda b, i: (i, 0))
    mb = pl.BlockSpec((rows, 1), lambda b, i: (n - 1 - i, 0))
    out = jax.ShapeDtypeStruct((bt, p, width), F32)
    g0 = hg_col0 // width
    hg_specs = [spec(g0, False), spec(g0 + 1, False), spec(g0 + 3, False), mf,
                spec(g0, True), spec(g0 + 2, True), spec(g0 + 3, True), mb,
                pl.BlockSpec((2, width), lambda b, i: (0, 0))]
    assert len(hg_specs) == N_HGRN_IN
    return pl.pallas_call(
        functools.partial(_scans_kernel, nc=nc),
        grid=(bt, n),
        in_specs=hg_specs + [fwd] * (N_RWKV_IN // 2) + [bwd] * (N_RWKV_IN // 2),
        out_specs=[fwd, bwd, fwd, bwd],
        out_shape=[out] * 4,
        scratch_shapes=[pltpu.VMEM((2, nh, HG_HEAD_DIM, HG_HEAD_DIM), F32),
                        pltpu.VMEM((2, width // V7X_LANES, V7X_LANES, V7X_LANES), F32)],
        compiler_params=_params(("parallel", "arbitrary")),
        name="scans",
    )(z_hg, z_hg, z_hg, valid_seq, z_hg, z_hg, z_hg, valid_seq, lb,
      r, v, nkk, lw_f, be_f, kd_f, r, v, nkk, lw_b, be_b, kd_b)


def _merge_kernel(x_ref, g_ref, valid_ref, hof_ref, hob_ref, hg_ref, ysc_ref, yda_ref,
                  rof_ref, rob_ref, bonus_ref, rgate_ref, onorm_ref, lnxg_ref, lnxb_ref, bd_ref,
                  wg_ref, bp_ref, wo_ref, o_ref):
    x = x_ref[...]
    d = x.shape[1]
    h = _masked_rms_norm(x, g_ref[...], valid_ref[...]).astype(BF16)

    o = hof_ref[...] + hob_ref[...]
    heads = []
    for c in range(0, o.shape[1], HG_HEAD_DIM):
        oh = o[:, c:c + HG_HEAD_DIM]
        heads.append(oh * lax.rsqrt(jnp.mean(oh * oh, axis=-1, keepdims=True) + NORM_EPS))
    hg = hg_ref[...]
    y_hg = jnp.concatenate(heads, axis=1) * onorm_ref[...] * (hg * _sigmoid(hg))

    o = rof_ref[...] + rob_ref[...]
    bd = bd_ref[...]
    inv = 1.0 / RW_HEAD_DIM
    cen = o - _group_sum(o, bd) * inv
    var = _group_sum(cen * cen, bd) * inv
    y_rw = ((cen * lax.rsqrt(var + RW_LNX_EPS) * lnxg_ref[...] + lnxb_ref[...] + bonus_ref[...])
            * rgate_ref[...])

    merged = jnp.zeros_like(x)
    for n, y in enumerate((y_hg, ysc_ref[...], yda_ref[...], y_rw)):
        gate = _sigmoid(_dot(h, wg_ref[:, n * d:(n + 1) * d]))
        merged = merged + gate * _dot(y.astype(BF16), bp_ref[n])
    o_ref[...] = x + _dot(merged.astype(BF16), wo_ref[...])


def merge(x2d, g, valid, hg_of, hg_ob, z_hg2d, hg_col0, y_sc, y_da, rw_of, rw_ob, bonus, rw_gate,
          onorm_g, lnx_g, lnx_b, bd_ones, w_gate, branch_proj, w_out, tm):
    m, d = x2d.shape
    bw = y_sc.shape[1]
    row = lambda width: pl.BlockSpec((tm, width), lambda i: (i, 0))
    full = lambda a: pl.BlockSpec(a.shape, lambda i: (0,) * a.ndim)
    gate_blk = hg_col0 // bw + 4
    hg_gate = pl.BlockSpec((tm, bw), lambda i: (i, gate_blk))
    consts = (onorm_g, lnx_g, lnx_b, bd_ones, w_gate, branch_proj, w_out)
    return pl.pallas_call(
        _merge_kernel,
        grid=(m // tm,),
        in_specs=[row(d), full(g), row(1), row(bw), row(bw), hg_gate] + [row(bw)] * 6
                 + [full(a) for a in consts],
        out_specs=row(d),
        out_shape=jax.ShapeDtypeStruct((m, d), F32),
        compiler_params=_params(("parallel",)),
        name="merge",
    )(x2d, g, valid, hg_of, hg_ob, z_hg2d, y_sc, y_da, rw_of, rw_ob, bonus, rw_gate, *consts)


def _mlp_kernel(x_ref, g_ref, w1_ref, w2_ref, o_ref, h_scr, acc_scr):
    f = pl.program_id(1)

    @pl.when(f == 0)
    def _():
        x = x_ref[...]
        ms = jnp.mean(x * x, axis=-1, keepdims=True)
        h_scr[...] = (x * lax.rsqrt(ms + NORM_EPS) * g_ref[...]).astype(BF16)
        acc_scr[...] = jnp.zeros_like(acc_scr)

    a = jnp.maximum(_dot(h_scr[...], w1_ref[...]), 0.0)
    acc_scr[...] += _dot((a * a).astype(BF16), w2_ref[...])

    @pl.when(f == pl.num_programs(1) - 1)
    def _():
        o_ref[...] = x_ref[...] + acc_scr[...]


def mlp(x2d, g, w1, w2, tm, tf):
    m, d = x2d.shape
    dff = w1.shape[1]
    return pl.pallas_call(
        _mlp_kernel,
        grid=(m // tm, dff // tf),
        in_specs=[pl.BlockSpec((tm, d), lambda i, f: (i, 0)),
                  pl.BlockSpec((1, d), lambda i, f: (0, 0)),
                  pl.BlockSpec((d, tf), lambda i, f: (0, f)),
                  pl.BlockSpec((tf, d), lambda i, f: (f, 0))],
        out_specs=pl.BlockSpec((tm, d), lambda i, f: (i, 0)),
        out_shape=jax.ShapeDtypeStruct((m, d), F32),
        scratch_shapes=[pltpu.VMEM((tm, d), BF16), pltpu.VMEM((tm, d), F32)],
        compiler_params=_params(("parallel", "arbitrary")),
        name="mlp",
    )(x2d, g, w1, w2)


def _rope_tables(p):
    half = ROPE_DIM // 2
    pos = jnp.arange(p, dtype=F32) - FRONT_PAD
    inv = ROPE_THETA ** (-jnp.arange(half, dtype=F32) / half)
    ang = pos[:, None] * inv[None, :]
    cos, sin = jnp.cos(ang), jnp.sin(ang)
    ones = jnp.ones((p, DA_QK_DIM - ROPE_DIM), F32)
    zeros = jnp.zeros((p, DA_QK_DIM - ROPE_DIM), F32)
    zh = jnp.zeros((p, half), F32)
    c = jnp.concatenate([cos, cos, ones], axis=1)
    s1 = jnp.concatenate([-sin, zh, zeros], axis=1)
    s2 = jnp.concatenate([zh, sin, zeros], axis=1)
    return tuple(jnp.tile(t, (1, 2)) for t in (c, s1, s2))


def _block_diag_ones(n, blk):
    i = jnp.arange(n)
    return ((i[:, None] // blk) == (i[None, :] // blk)).astype(F32)


def kernel(x_prompt, x_sample, meta_tokens, norm_mix_g, w_in, hgrn_lb_logits, hgrn_onorm_g, conv_w,
           diff_qnorm_g, diff_knorm_g, diff_lambda, diff_subln_g, rwkv_mu, rwkv_w0, rwkv_w2, rwkv_a0,
           rwkv_a2, rwkv_g2, rwkv_k_k, rwkv_k_a, rwkv_r_k, rwkv_lnx_g, rwkv_lnx_b, w_gate,
           branch_proj, w_out, norm_mlp_g, mlp_w1, mlp_w2):
    assert x_prompt.shape[1:] == x_sample.shape[1:]
    x = jnp.concatenate([x_prompt, x_sample], axis=0)
    bt, seq, d = x.shape
    depth = w_in.shape[0]
    length = N_META + seq
    p = -(-(FRONT_PAD + length) // V7X_LANES) * V7X_LANES
    m = bt * p
    bw = branch_proj.shape[2]
    rw_cols = rwkv_mu.shape[1]
    sizes = (5 * bw, 3 * bw, 3 * bw, rw_cols)
    offs = [0]
    for s in sizes:
        offs.append(offs[-1] + s)
    rw_pad = -(-rw_cols // bw) * bw
    c_hg, c_sc, c_da = rw_pad, rw_pad + sizes[0], rw_pad + sizes[0] + sizes[1]
    zw = c_da + sizes[2]

    meta = jnp.broadcast_to(meta_tokens.astype(x.dtype)[None], (bt, N_META, d))
    xp = jnp.concatenate([jnp.zeros((bt, FRONT_PAD, d), x.dtype), meta, x,
                          jnp.zeros((bt, p - FRONT_PAD - length, d), x.dtype)], axis=1)
    x2d = xp.reshape(m, d)

    rows = jnp.arange(p)
    valid_seq = ((rows >= FRONT_PAD) & (rows < FRONT_PAD + length)).astype(F32)[:, None]
    valid = jnp.tile(valid_seq, (bt, 1))
    key_bias = jnp.where(valid_seq > 0, 0.0, NEG_BIG).astype(F32)
    tables = _rope_tables(p)
    bd64_mean = (_block_diag_ones(V7X_LANES, DA_QK_DIM) / DA_QK_DIM).astype(BF16)
    bd64_ones = _block_diag_ones(V7X_LANES, RW_HEAD_DIM).astype(BF16)

    sm = jax.nn.softmax(hgrn_lb_logits.astype(F32), axis=1)
    lb_all = jnp.cumsum(sm, axis=1) - sm[:, :1]

    tm_proj = _pick(m, 1536, 128)
    tm_merge = _pick(m, 256, 128)
    tm_mlp = _pick(m, 768, 128)
    tq = _pick(p, 384, 128)
    tk = _pick(p, 384, 128)
    assert FRONT_PAD <= tk and p - (FRONT_PAD + length) <= tk
    tr = _pick(p, 384, 8)

    half = V7X_LANES // 2
    for l in range(depth):
        g_mix = norm_mix_g[l][None, :]
        w_l = jnp.concatenate([w_in[l][:, offs[3]:], jnp.zeros((d, rw_pad - rw_cols), w_in.dtype),
                               w_in[l][:, :offs[3]]], axis=1).astype(BF16)
        z = norm_matmul(x2d, g_mix, valid, w_l, tm_proj, _pick(zw, 768, V7X_LANES)).reshape(bt, p, zw)

        y_sc = shortconv(z, c_sc, conv_w[l])

        lam_init = 0.8 - 0.6 * math.exp(-0.3 * l)
        zpad = jnp.zeros((2, half, bw), F32)
        w2pad = jnp.stack([jnp.concatenate([rwkv_w2[l, 0], zpad[0]], 0),
                           jnp.concatenate([zpad[0], rwkv_w2[l, 1]], 0)])
        a2pad = jnp.stack([jnp.concatenate([rwkv_a2[l, 0], zpad[0]], 0),
                           jnp.concatenate([zpad[0], rwkv_a2[l, 1]], 0)])
        y_da = diff_attention(z, c_da, tables, bd64_mean,
                              jnp.tile(diff_qnorm_g[l], 2)[None, :], jnp.tile(diff_knorm_g[l], 2)[None, :],
                              diff_lambda[l], diff_subln_g[l][:, None], key_bias, lam_init, tq, tk)
        (r, v, nkk, lwf, bef, kdf, lwb, beb, kdb, gate, bonus) = rwkv_prep(
            z, valid_seq, rwkv_mu[l][None, :], rwkv_w0[l], w2pad, rwkv_a0[l], a2pad, rwkv_g2[l],
            rwkv_k_k[l][None, :], rwkv_k_a[l][None, :], rwkv_r_k[l].reshape(1, bw), bd64_ones, tr)
        of, ob, orf, orb = scans(z, c_hg, valid_seq, lb_all[:, l], r, v, nkk, lwf, bef, kdf, lwb, beb,
                                 kdb, _pick(p // RW_CHUNK, SCAN_CHUNKS_PER_STEP, 1))

        flat = lambda a: a.reshape(m, a.shape[-1])
        x2d = merge(x2d, g_mix, valid, flat(of), flat(ob), flat(z), c_hg, flat(y_sc), flat(y_da),
                    flat(orf), flat(orb), flat(bonus), flat(gate), hgrn_onorm_g[l][None, :],
                    rwkv_lnx_g[l][None, :], rwkv_lnx_b[l][None, :], bd64_ones,
                    w_gate[l].astype(BF16), branch_proj[l].astype(BF16), w_out[l].astype(BF16), tm_merge)
        x2d = mlp(x2d, norm_mlp_g[l][None, :], mlp_w1[l].astype(BF16), mlp_w2[l].astype(BF16),
                  tm_mlp, _pick(mlp_w1.shape[2], 1024, V7X_LANES))

    y = x2d.reshape(bt, p, d)[:, FRONT_PAD + N_META:FRONT_PAD + length]
    nb = x_prompt.shape[0]
    return (y[:nb], y[nb:])
```

```python
import functools
import math

import jax
import jax.numpy as jnp
from jax import lax
from jax.experimental import pallas as pl
from jax.experimental.pallas import tpu as pltpu

F32 = jnp.float32
BF16 = jnp.bfloat16
HI = lax.Precision.HIGHEST

V7X_LANES = 128
V7X_VMEM_BYTES = 64 * 1024 * 1024
VMEM_LIMIT = V7X_VMEM_BYTES - 8 * 1024 * 1024

N_META = 16
NORM_EPS = 1e-6
N_BRANCH = 4
HG_HEAD_DIM = 128
HG_CHUNK = 64
HG_SUB = 16
LB_FLOOR = 1e-20
SC_KSIZE = 3
DA_HEADS = 4
DA_QK_DIM = 64
DA_V_DIM = 128
ROPE_THETA = 500000.0
ROPE_DIM = DA_QK_DIM // 4
SUBLN_EPS = 1e-5
RW_HEAD_DIM = 64
RW_CHUNK = 64
RW_LNX_EPS = 64e-5
FRONT_PAD = (-N_META) % HG_CHUNK
NEG_BIG = -1e30
LOG2_E = 1.4426950408889634

P_HG = "bf16"
P_RW_LR = "bf16"
SCAN_CHUNKS_PER_STEP = 3
ATTN_HEADS_PER_STEP = 2


def _split_bf16(a):
    hi = a.astype(BF16)
    return hi, (a - hi.astype(F32)).astype(BF16)


def _mm(a, b, dims, precision):
    dg = functools.partial(lax.dot_general, dimension_numbers=(dims, ((), ())),
                           preferred_element_type=F32)
    if precision == "bf16":
        return dg(a.astype(BF16), b.astype(BF16))
    if precision == "x3":
        ah, al = _split_bf16(a)
        bh, bl = _split_bf16(b)
        return dg(ah, bh) + (dg(al, bh) + dg(ah, bl))
    return dg(a, b, precision=precision)


def _dot(a, b, precision=None):
    return _mm(a, b, ((1,), (0,)), precision)


def _dot_split_rhs3(a_bf16, b):
    b1 = b.astype(BF16)
    r1 = b - b1.astype(F32)
    b2 = r1.astype(BF16)
    b3 = (r1 - b2.astype(F32)).astype(BF16)
    return _dot(a_bf16, b1) + (_dot(a_bf16, b2) + _dot(a_bf16, b3))


def _dot_split_lhs(a, b_bf16):
    hi, lo = _split_bf16(a)
    return _dot(hi, b_bf16) + _dot(lo, b_bf16)


def _dot_nt(a, b, precision=None):
    return _mm(a, b, ((1,), (1,)), precision)


def _dot_tn(a, b, precision=None):
    return _mm(a, b, ((0,), (0,)), precision)


def _pick(n, target, mult):
    best = None
    for d in range(mult, min(n, target) + 1, mult):
        if n % d == 0:
            best = d
    assert best is not None, (n, target, mult)
    return best


def _params(sem):
    return pltpu.CompilerParams(dimension_semantics=sem, vmem_limit_bytes=VMEM_LIMIT)


def _log_sigmoid(x):
    return jnp.minimum(x, 0.0) - jnp.log1p(jnp.exp(-jnp.abs(x)))


def _sigmoid(x):
    return 1.0 / (1.0 + jnp.exp(-x))


def _masked_rms_norm(x, g, valid):
    ms = jnp.mean(x * x, axis=-1, keepdims=True)
    h = x * lax.rsqrt(ms + NORM_EPS) * g
    return jnp.where(valid > 0.0, h, 0.0)


def _norm_matmul_kernel(x_ref, g_ref, valid_ref, w_ref, o_ref, h_scr):
    @pl.when(pl.program_id(1) == 0)
    def _():
        h_scr[...] = _masked_rms_norm(x_ref[...], g_ref[...], valid_ref[...]).astype(BF16)

    o_ref[...] = _dot(h_scr[...], w_ref[...])


def norm_matmul(x2d, g, valid, w_bf16, tm, tn):
    m, d = x2d.shape
    n = w_bf16.shape[1]
    return pl.pallas_call(
        _norm_matmul_kernel,
        grid=(m // tm, n // tn),
        in_specs=[
            pl.BlockSpec((tm, d), lambda i, j: (i, 0)),
            pl.BlockSpec((1, d), lambda i, j: (0, 0)),
            pl.BlockSpec((tm, 1), lambda i, j: (i, 0)),
            pl.BlockSpec((d, tn), lambda i, j: (0, j)),
        ],
        out_specs=pl.BlockSpec((tm, tn), lambda i, j: (i, j)),
        out_shape=jax.ShapeDtypeStruct((m, n), F32),
        scratch_shapes=[pltpu.VMEM((tm, d), BF16)],
        compiler_params=_params(("parallel", "arbitrary")),
        name="norm_matmul",
    )(x2d, g, valid, w_bf16)


def _shortconv_kernel(b_ref, c_ref, h_ref, w_ref, o_ref):
    u = c_ref[0] * h_ref[0]
    p = u.shape[0]
    w = w_ref[...]
    y = pltpu.roll(u, 1, 0) * w[0:1] + u * w[1:2] + pltpu.roll(u, p - 1, 0) * w[2:3]
    o_ref[0] = b_ref[0] * y


def shortconv(z, col0, conv_w):
    bt, p, _ = z.shape
    width = conv_w.shape[1]
    nb = width // V7X_LANES
    c0 = col0 // V7X_LANES
    blk = (1, p, V7X_LANES)
    z_sc = z
    return pl.pallas_call(
        _shortconv_kernel,
        grid=(bt, nb),
        in_specs=[
            pl.BlockSpec(blk, lambda b, c: (b, 0, c0 + c)),
            pl.BlockSpec(blk, lambda b, c: (b, 0, c0 + nb + c)),
            pl.BlockSpec(blk, lambda b, c: (b, 0, c0 + 2 * nb + c)),
            pl.BlockSpec((SC_KSIZE, V7X_LANES), lambda b, c: (0, c)),
        ],
        out_specs=pl.BlockSpec(blk, lambda b, c: (b, 0, c)),
        out_shape=jax.ShapeDtypeStruct((bt, p, width), F32),
        compiler_params=_params(("parallel", "parallel")),
        name="shortconv",
    )(z_sc, z_sc, z_sc, conv_w)


def _attn_kernel(q_ref, k_ref, v_ref, cq_ref, s1q_ref, s2q_ref, ck_ref, s1k_ref, s2k_ref,
                 bd_ref, qg_ref, kg_ref, lam_ref, sg_ref, bias_ref, o_ref, k_scr, v_scr, s_scr,
                 *, lam_init, tk):
    bd = bd_ref[...]

    def norm_rope(x, g, c, s1, s2):
        ms = _dot_split_lhs(x * x, bd)
        y = x * lax.rsqrt(ms + NORM_EPS) * g
        return (y * c + pltpu.roll(y, V7X_LANES - ROPE_DIM // 2, 1) * s1
                + pltpu.roll(y, ROPE_DIM // 2, 1) * s2)

    nheads = k_scr.shape[0]
    nk = k_scr.shape[1] // tk
    lanes = lambda h: slice(h * V7X_LANES, (h + 1) * V7X_LANES)

    @pl.when(pl.program_id(2) == 0)
    def _():
        for h in range(nheads):
            k = norm_rope(k_ref[0, :, lanes(h)], kg_ref[...], ck_ref[...], s1k_ref[...], s2k_ref[...])
            k_scr[h] = k.astype(BF16)
            v_scr[h] = v_ref[0, :, lanes(h)].T.astype(BF16)

    lp = lam_ref[...]
    lam = (jnp.exp(jnp.sum(lp[0:1] * lp[1:2], axis=-1, keepdims=True))
           - jnp.exp(jnp.sum(lp[2:3] * lp[3:4], axis=-1, keepdims=True)) + lam_init)

    def head(h):
        q = norm_rope(q_ref[0, :, lanes(h)], qg_ref[...], cq_ref[...], s1q_ref[...], s2q_ref[...])
        q = q * (DA_QK_DIM ** -0.5 * LOG2_E)
        tq = q.shape[0]
        first = lax.broadcasted_iota(jnp.int32, q.shape, 1) < DA_QK_DIM
        qst = jnp.concatenate([jnp.where(first, q, 0.0), jnp.where(first, 0.0, q)], axis=0).T.astype(BF16)
        slabs = range(0, tk, 8)
        mrow = None
        for j in range(nk):
            rows = slice(j * tk, (j + 1) * tk)
            s_j = _dot(k_scr[h, rows, :], qst)
            if j in (0, nk - 1):
                s_j = s_j + bias_ref[rows, :]
            s_scr[rows, :] = s_j
            mt = functools.reduce(jnp.maximum, [s_j[r:r + 8] for r in slabs])
            mrow = mt if mrow is None else jnp.maximum(mrow, mt)
            yield
        m = jnp.max(mrow, axis=0, keepdims=True)
        lrow = jnp.zeros_like(mrow)
        o2 = jnp.zeros((DA_V_DIM, 2 * tq), F32)
        for j in range(nk):
            rows = slice(j * tk, (j + 1) * tk)
            pr = jnp.exp2(s_scr[rows, :] - m)
            lrow = lrow + functools.reduce(jnp.add, [pr[r:r + 8] for r in slabs])
            o2 = o2 + _dot(v_scr[h, :, rows], pr.astype(BF16))
            yield
        o2 = o2 * (1.0 / jnp.sum(lrow, axis=0, keepdims=True))
        o = o2[:, :tq] - lam * o2[:, tq:]
        ms = jnp.mean(o * o, axis=0, keepdims=True)
        o_ref[0, :, lanes(h)] = (o * lax.rsqrt(ms + SUBLN_EPS) * sg_ref[...] * (1.0 - lam_init)).T

    _alternate([head(h) for h in range(nheads)], delays=[h * nk for h in range(nheads)])


def _alternate(streams, delays=None):
    pending = list(zip(delays or [0] * len(streams), streams))
    rounds = 0
    while pending:
        for item in list(pending):
            if item[0] <= rounds and next(item[1], StopIteration) is StopIteration:
                pending.remove(item)
        rounds += 1


def diff_attention(z, col0, tables, bd64, qn_g, kn_g, lam_p, subln_g, key_bias, lam_init, tq, tk):
    bt, p, _ = z.shape
    z_da = z
    c_t, s1_t, s2_t = tables
    nh = DA_HEADS
    hps = ATTN_HEADS_PER_STEP
    wide = hps * V7X_LANES
    assert nh % hps == 0 and col0 % wide == 0
    c0 = col0 // wide
    ng = nh // hps
    qspec = pl.BlockSpec((1, tq, wide), lambda b, h, i: (b, i, c0 + h))
    kspec = pl.BlockSpec((1, p, wide), lambda b, h, i: (b, 0, c0 + ng + h))
    vspec = pl.BlockSpec((1, p, wide), lambda b, h, i: (b, 0, c0 + 2 * ng + h))
    tq_spec = pl.BlockSpec((tq, V7X_LANES), lambda b, h, i: (i, 0))
    once = pl.Buffered(1)
    tk_spec = pl.BlockSpec((p, V7X_LANES), lambda b, h, i: (0, 0), pipeline_mode=once)
    full = lambda shape: pl.BlockSpec(shape, lambda b, h, i: (0,) * len(shape))
    return pl.pallas_call(
        functools.partial(_attn_kernel, lam_init=lam_init, tk=tk),
        grid=(bt, ng, p // tq),
        in_specs=[qspec, kspec, vspec, tq_spec, tq_spec, tq_spec, tk_spec, tk_spec, tk_spec,
                  full((V7X_LANES, V7X_LANES)), full((1, V7X_LANES)), full((1, V7X_LANES)),
                  full((4, DA_QK_DIM)), full((DA_V_DIM, 1)),
                  pl.BlockSpec((p, 1), lambda b, h, i: (0, 0), pipeline_mode=once)],
        out_specs=pl.BlockSpec((1, tq, wide), lambda b, h, i: (b, i, h)),
        out_shape=jax.ShapeDtypeStruct((bt, p, nh * DA_V_DIM), F32),
        scratch_shapes=[pltpu.VMEM((hps, p, V7X_LANES), BF16), pltpu.VMEM((hps, DA_V_DIM, p), BF16),
                        pltpu.VMEM((p, 2 * tq), F32)],
        compiler_params=_params(("parallel", "parallel", "arbitrary")),
        name="diff_attention",
    )(z_da, z_da, z_da, c_t, s1_t, s2_t, c_t, s1_t, s2_t, bd64, qn_g, kn_g, lam_p, subln_g,
      key_bias)


def _gla_local(q, qt, qh, k, kh, v, b, bpiv, reverse):
    c, sub, half = HG_CHUNK, HG_SUB, HG_SUB // 2
    nsub = c // sub
    row = lax.broadcasted_iota(jnp.int32, (c, 1), 0)
    attn = jnp.zeros((c, c), F32)
    for i in range(nsub):
        if (reverse and i == nsub - 1) or (not reverse and i == 0):
            continue
        bi = bpiv[i * sub:i * sub + 1]
        kmask = (row >= (i + 1) * sub) if reverse else (row < i * sub)
        kt = jnp.where(kmask, k * jnp.exp(jnp.minimum(bi - b, 0.0)), 0.0)
        attn = jnp.where(row // sub == i, _dot_nt(qt, kt, P_HG), attn)
    t_idx = lax.broadcasted_iota(jnp.int32, (c, c), 0)
    s_idx = lax.broadcasted_iota(jnp.int32, (c, c), 1)
    attn = jnp.where(t_idx // sub == s_idx // sub, _dot_nt(qh, kh, P_HG), attn)

    b2 = b * LOG2_E
    tloc = lax.broadcasted_iota(jnp.int32, (half, c), 0)
    scol = lax.broadcasted_iota(jnp.int32, (half, c), 1)
    blocks = []
    for i in range(c // half):
        bt, qi = b2[i * half:(i + 1) * half], q[i * half:(i + 1) * half]
        blk = attn[i * half:(i + 1) * half]
        for j in range(half):
            s = i * half + j
            a = jnp.sum(qi * k[s:s + 1] * jnp.exp2(bt - b2[s:s + 1]), axis=-1, keepdims=True)
            keep = (scol == s) & ((tloc <= j) if reverse else (tloc >= j))
            blk = jnp.where(keep, a, blk)
        blocks.append(blk)
    return _dot(jnp.concatenate(blocks, axis=0), v, P_HG)


def _hgrn_body(qf_ref, ff_ref, vf_ref, mf_ref, qb_ref, fb_ref, vb_ref, mb_ref, lb_ref,
               of_ref, ob_ref, st_scr, *, nc):
    c, sub, half = HG_CHUNK, HG_SUB, HG_SUB // 2
    nh = st_scr.shape[1]
    t_idx = lax.broadcasted_iota(jnp.int32, (c, c), 0)
    s_idx = lax.broadcasted_iota(jnp.int32, (c, c), 1)
    second = (lax.broadcasted_iota(jnp.int32, (c, 1), 0) % sub) >= half
    dirs = ((qf_ref, ff_ref, vf_ref, mf_ref, of_ref), (qb_ref, fb_ref, vb_ref, mb_ref, ob_ref))
    units = []
    for d, (q_ref, f_ref, v_ref, m_ref, _) in enumerate(dirs):
        reverse = d == 1
        if reverse:
            mats = (s_idx >= t_idx, s_idx >= (t_idx // sub + 1) * sub,
                    s_idx >= (t_idx // half + 1) * half, s_idx >= (t_idx // half) * half)
        else:
            mats = (s_idx <= t_idx, s_idx < (t_idx // sub) * sub,
                    s_idx < (t_idx // half) * half, s_idx < (t_idx // half + 1) * half)
        cmat = jnp.concatenate([x.astype(F32) for x in mats], axis=0).astype(BF16)
        q_half = jnp.logical_not(second) if reverse else second
        lb = lb_ref[d:d + 1, :]
        la = jnp.log(jnp.maximum(lb, LB_FLOOR))
        l1 = jnp.log1p(-lb)
        for j in range(nc):
            rows = slice(j * c, (j + 1) * c)
            valid = m_ref[rows, :] > 0.0
            lc = l1 + _log_sigmoid(f_ref[0, rows, :])
            logf = jnp.maximum(la, lc) + jnp.log1p(jnp.exp(-jnp.abs(la - lc)))
            logf = jnp.where(valid, logf, 0.0)
            k = jnp.where(valid, 1.0 - jnp.exp(logf), 0.0)
            cum = _dot_split_rhs3(cmat, logf)
            b, bpiv, bhin, bhout = (cum[i * c:(i + 1) * c] for i in range(4))
            btot = b[0:1] if reverse else b[c - 1:c]
            q, v = q_ref[0, rows, :], v_ref[0, rows, :]
            qt = q * jnp.exp(b - bpiv)
            qh = jnp.where(q_half, q * jnp.exp(b - bhin), 0.0)
            kh = jnp.where(q_half, 0.0, k * jnp.exp(bhout - b))
            qe = (q * jnp.exp(b)).astype(BF16)
            k2 = k * jnp.exp(btot - b)
            dec = jnp.exp(btot)
            for h in range(nh):
                hs = slice(h * HG_HEAD_DIM, (h + 1) * HG_HEAD_DIM)
                units.append(dict(
                    d=d, j=j, h=h, rows=rows, hs=hs, qe=qe[:, hs], dec=dec[:, hs],
                    o=_gla_local(q[:, hs], qt[:, hs], qh[:, hs], k[:, hs], kh[:, hs], v[:, hs],
                                 b[:, hs], bpiv[:, hs], reverse),
                    kv=_dot_tn(v[:, hs], k2[:, hs], P_HG)))
                yield

    state = {(d, h): st_scr[d, h] for d in range(2) for h in range(nh)}
    for step in range(nc):
        for u in units:
            if u["j"] != (nc - 1 - step if u["d"] == 1 else step):
                continue
            key = (u["d"], u["h"])
            dirs[u["d"]][4][0, u["rows"], u["hs"]] = u["o"] + _dot_nt(u["qe"], state[key].astype(BF16))
            state[key] = state[key] * u["dec"] + u["kv"]
        yield
    for (d, h), s in state.items():
        st_scr[d, h] = s


def _group_sum(x, bd):
    cols = [_dot_split_lhs(x[:, j:j + V7X_LANES], bd) for j in range(0, x.shape[1], V7X_LANES)]
    return jnp.concatenate(cols, axis=1)


def _rwkv_prep_kernel(z_ref, zp_ref, zn_ref, valid_ref, mu_ref, w0_ref, w2_ref, a0_ref, a2_ref,
                      g2_ref, kk_ref, ka_ref, rk_ref, bd_ref,
                      r_out, v_out, nkk_out, lwf_out, bef_out, kdf_out, lwb_out, beb_out, kdb_out,
                      gate_out, bonus_out):
    u = z_ref[0]
    tr = u.shape[0]
    width = r_out.shape[-1]
    row = lax.broadcasted_iota(jnp.int32, (8, 1), 0)
    down, up = pltpu.roll(u, 1, 0), pltpu.roll(u, tr - 1, 0)
    u_prev = jnp.concatenate([jnp.where(row == 0, zp_ref[0][7:8], down[:8]), down[8:]], axis=0)
    u_next = jnp.concatenate([up[:tr - 8], jnp.where(row == 7, zn_ref[0][0:1], up[tr - 8:])], axis=0)
    mu = mu_ref[...]
    xm = u * (1.0 - mu) + (u_prev + u_next) * (0.5 * mu)
    valid = valid_ref[...] > 0.0
    bd = bd_ref[...]
    r = xm[:, 0:width]
    k = xm[:, width:2 * width]
    v = jnp.where(valid, xm[:, 2 * width:3 * width], 0.0)
    lr = 3 * width
    wl = jnp.tanh(xm[:, lr:lr + V7X_LANES])
    al = xm[:, lr + V7X_LANES:lr + 2 * V7X_LANES]
    gl = xm[:, lr + 2 * V7X_LANES:lr + 3 * V7X_LANES]
    kk = k * kk_ref[...]
    kk = kk / jnp.maximum(jnp.sqrt(_group_sum(kk * kk, bd)), 1e-12)
    kk = jnp.where(valid, kk, 0.0)
    r_out[0] = r.astype(BF16)
    v_out[0] = v.astype(BF16)
    nkk_out[0] = (-kk).astype(BF16)
    kd_sum = jnp.zeros_like(k)
    for d, (lw_out, be_out, kd_out) in enumerate(((lwf_out, bef_out, kdf_out),
                                                   (lwb_out, beb_out, kdb_out))):
        wlog = _log_sigmoid(w0_ref[d:d + 1, :] + _dot(wl, w2_ref[d], P_RW_LR)) - 0.5
        lw_out[0] = jnp.where(valid, -jnp.exp(wlog), 0.0)
        a = _sigmoid(a0_ref[d:d + 1, :] + _dot(al, a2_ref[d], P_RW_LR))
        kd = k * (1.0 + (a - 1.0) * ka_ref[...])
        kd_sum = kd_sum + kd
        kd_out[0] = jnp.where(valid, kd, 0.0).astype(BF16)
        be_out[0] = (kk * a).astype(BF16)
    gate_out[0] = _dot(_sigmoid(gl), g2_ref[...], P_RW_LR).astype(BF16)
    bonus_out[0] = (_group_sum(r * kd_sum * rk_ref[...], bd) * v).astype(BF16)


def rwkv_prep(z_rw, valid2d, mu, w0, w2pad, a0, a2pad, g2, k_k, k_a, r_k, bd_ones, tr):
    bt, p, _ = z_rw.shape
    cols = mu.shape[1]
    width = w0.shape[1]
    nt = p // tr
    r8 = tr // 8
    last8 = p // 8 - 1
    full = lambda a: pl.BlockSpec(a.shape, lambda b, i: (0,) * a.ndim)
    outs = [jax.ShapeDtypeStruct((bt, p, width), F32 if i in (3, 6) else BF16) for i in range(11)]
    ospec = pl.BlockSpec((1, tr, width), lambda b, i: (b, i, 0))
    params = (mu, w0, w2pad, a0, a2pad, g2, k_k, k_a, r_k, bd_ones)
    return pl.pallas_call(
        _rwkv_prep_kernel,
        grid=(bt, nt),
        in_specs=[pl.BlockSpec((1, tr, cols), lambda b, i: (b, i, 0)),
                  pl.BlockSpec((1, 8, cols), lambda b, i: (b, jnp.maximum(i * r8 - 1, 0), 0)),
                  pl.BlockSpec((1, 8, cols), lambda b, i: (b, jnp.minimum((i + 1) * r8, last8), 0)),
                  pl.BlockSpec((tr, 1), lambda b, i: (i, 0))] + [full(a) for a in params],
        out_specs=[ospec] * 11,
        out_shape=outs,
        compiler_params=_params(("parallel", "parallel")),
        name="rwkv_prep",
    )(z_rw, z_rw, z_rw, valid2d, *params)


def _tri_inverse_all(l_list, eye):
    n = range(len(l_list))
    m = l_list[0].shape[0]
    l_hi_lo = [_split_bf16(l) for l in l_list]
    pw = [hl[0] for hl in l_hi_lo]
    tinv = [eye + l for l in l_list]
    levels = int(math.log2(RW_CHUNK))
    nxt = [_dot(pw[u], pw[u]).astype(BF16) for u in n]
    yield
    for j in range(1, levels):
        pw = nxt
        if j < levels - 1:
            both = [_dot(jnp.concatenate([pw[u], tinv[u].astype(BF16)], axis=0), pw[u]) for u in n]
            nxt = [both[u][:m].astype(BF16) for u in n]
            tinv = [tinv[u] + both[u][m:] for u in n]
        else:
            tinv = [tinv[u] + _dot(tinv[u].astype(BF16), pw[u]) for u in n]
        yield
    t_hi_lo = [_split_bf16(t) for t in tinv]
    lt = [_dot(jnp.concatenate(l_hi_lo[u], axis=0), t_hi_lo[u][0]) for u in n]
    yield
    resid = [(eye - tinv[u]) + (lt[u][:m] + (lt[u][m:] + _dot(l_hi_lo[u][0], t_hi_lo[u][1])))
             for u in n]
    yield
    return [tinv[u] + _dot(t_hi_lo[u][0], resid[u].astype(BF16)) for u in n]


def _rwkv_body(rf_ref, vf_ref, af_ref, lwf_ref, bef_ref, kdf_ref,
               rb_ref, vb_ref, ab_ref, lwb_ref, beb_ref, kdb_ref,
               of_ref, ob_ref, h_scr, *, nc):
    c = RW_CHUNK
    n2 = 2 * c
    npair = h_scr.shape[1]
    t_idx = lax.broadcasted_iota(jnp.int32, (c, c), 0)
    s_idx = lax.broadcasted_iota(jnp.int32, (c, c), 1)
    ti = lax.broadcasted_iota(jnp.int32, (n2, n2), 0)
    si = lax.broadcasted_iota(jnp.int32, (n2, n2), 1)
    same = (ti // c) == (si // c)
    tl, sl = ti % c, si % c
    eye = jnp.where(ti == si, 1.0, 0.0)
    first = lax.broadcasted_iota(jnp.int32, (c, V7X_LANES), 1) < RW_HEAD_DIM

    def stack(x):
        return jnp.concatenate([jnp.where(first, x, 0.0), jnp.where(first, 0.0, x)], axis=0)

    dirs = ((rf_ref, vf_ref, af_ref, lwf_ref, bef_ref, kdf_ref, of_ref),
            (rb_ref, vb_ref, ab_ref, lwb_ref, beb_ref, kdb_ref, ob_ref))
    masks = []
    for reverse in (False, True):
        masks.append((((s_idx >= t_idx) if reverse else (s_idx <= t_idx)).astype(F32).astype(BF16),
                      same & ((sl > tl) if reverse else (sl < tl)),
                      same & ((sl >= tl) if reverse else (sl <= tl))))

    units = []
    for step in range(nc):
        for d, (r_ref, v_ref, a_ref, lw_ref, be_ref, kd_ref, _) in enumerate(dirs):
            reverse = d == 1
            tri, strict, incl = masks[d]
            j = nc - 1 - step if reverse else step
            rows = slice(j * c, (j + 1) * c)
            lw = lw_ref[0, rows, :]
            lg = _dot_split_rhs3(tri, lw)
            ltot = lg[0:1] if reverse else lg[c - 1:c]
            ginv = jnp.exp(-lg)
            gend = jnp.exp(ltot - lg)
            be, kd = be_ref[0, rows, :], kd_ref[0, rows, :]
            cols = (a_ref[0, rows, :] * jnp.exp(lg - lw), r_ref[0, rows, :] * jnp.exp(lg),
                    be * ginv, kd * ginv, be * gend, kd * gend, v_ref[0, rows, :])
            gtot = jnp.exp(ltot)
            for p in range(npair):
                ps = slice(p * V7X_LANES, (p + 1) * V7X_LANES)
                a_s, r_s, b_s, k_s, bh_s, kh_s, v_s = (stack(x[:, ps]).astype(BF16) for x in cols)
                units.append(dict(step=step, d=d, p=p, rows=rows, ps=ps, strict=strict, incl=incl,
                                  ar=jnp.concatenate([a_s, r_s], axis=0),
                                  bk=jnp.concatenate([b_s, k_s], axis=0),
                                  bh=bh_s, kh=kh_s, v=v_s, gtot=gtot[:, ps]))
            yield
    for u in units:
        sc = _dot_nt(u["ar"], u["bk"])
        u["l_ab"] = jnp.where(u["strict"], sc[:n2, :n2], 0.0)
        u["l_ak"] = jnp.where(u["strict"], sc[:n2, n2:], 0.0).astype(BF16)
        u["m_rb"] = jnp.where(u["incl"], sc[n2:, :n2], 0.0).astype(BF16)
        u["m_rk"] = jnp.where(u["incl"], sc[n2:, n2:], 0.0).astype(BF16)
    yield
    tinvs = yield from _tri_inverse_all([u["l_ab"] for u in units], eye)
    for u, tinv in zip(units, tinvs):
        u["tinv"] = tinv.astype(BF16)
    yield
    for u in units:
        both = _dot(jnp.concatenate([u["l_ak"], u["m_rk"]], axis=0), u["v"])
        u["lakv"], u["mv"] = both[:n2], both[n2:]
        u["kv"] = _dot_tn(u["v"], u["kh"])
    yield

    state = {(d, p): h_scr[d, p] for d in range(2) for p in range(npair)}
    for step in range(nc):
        live = [u for u in units if u["step"] == step]
        for u in live:
            u["ah"] = _dot_nt(u["ar"], state[u["d"], u["p"]].astype(BF16))
        yield
        for u in live:
            u["u"] = _dot(u["tinv"], (u["ah"][:n2] + u["lakv"]).astype(BF16)).astype(BF16)
        yield
        for u in live:
            o_s = u["ah"][n2:] + u["mv"] + _dot(u["m_rb"], u["u"])
            dirs[u["d"]][6][0, u["rows"], u["ps"]] = o_s[:c] + o_s[c:]
            key = (u["d"], u["p"])
            state[key] = state[key] * u["gtot"] + u["kv"] + _dot_tn(u["u"], u["bh"])
        yield
    for (d, p), h in state.items():
        h_scr[d, p] = h


N_HGRN_IN = 9
N_RWKV_IN = 12


def _scans_kernel(*refs, nc):
    hg_in = refs[:N_HGRN_IN]
    rw_in = refs[N_HGRN_IN:N_HGRN_IN + N_RWKV_IN]
    hg_of, hg_ob, rw_of, rw_ob, st_scr, h_scr = refs[N_HGRN_IN + N_RWKV_IN:]

    @pl.when(pl.program_id(1) == 0)
    def _():
        st_scr[...] = jnp.zeros_like(st_scr)
        h_scr[...] = jnp.zeros_like(h_scr)

    _alternate([_rwkv_body(*rw_in, rw_of, rw_ob, h_scr, nc=nc),
                _hgrn_body(*hg_in, hg_of, hg_ob, st_scr, nc=nc)])


def scans(z_hg, hg_col0, valid_seq, lb, r, v, nkk, lw_f, be_f, kd_f, lw_b, be_b, kd_b, nc):
    assert HG_CHUNK == RW_CHUNK
    bt, p, width = r.shape
    nh = width // HG_HEAD_DIM
    rows = nc * HG_CHUNK
    n = p // rows
    blk = (1, rows, width)

    def spec(group, rev):
        if rev:
            return pl.BlockSpec(blk, lambda b, i: (b, n - 1 - i, group))
        return pl.BlockSpec(blk, lambda b, i: (b, i, group))

    fwd, bwd = spec(0, False), spec(0, True)
    mf = pl.BlockSpec((rows, 1), lambda b, i: (i, 0))
    mb = pl.BlockSpec((rows, 1), lambda b, i: (n - 1 - i, 0))
    out = jax.ShapeDtypeStruct((bt, p, width), F32)
    g0 = hg_col0 // width
    hg_specs = [spec(g0, False), spec(g0 + 1, False), spec(g0 + 3, False), mf,
                spec(g0, True), spec(g0 + 2, True), spec(g0 + 3, True), mb,
                pl.BlockSpec((2, width), lambda b, i: (0, 0))]
    assert len(hg_specs) == N_HGRN_IN
    return pl.pallas_call(
        functools.partial(_scans_kernel, nc=nc),
        grid=(bt, n),
        in_specs=hg_specs + [fwd] * (N_RWKV_IN // 2) + [bwd] * (N_RWKV_IN // 2),
        out_specs=[fwd, bwd, fwd, bwd],
        out_shape=[out] * 4,
        scratch_shapes=[pltpu.VMEM((2, nh, HG_HEAD_DIM, HG_HEAD_DIM), F32),
                        pltpu.VMEM((2, width // V7X_LANES, V7X_LANES, V7X_LANES), F32)],
        compiler_params=_params(("parallel", "arbitrary")),
        name="scans",
    )(z_hg, z_hg, z_hg, valid_seq, z_hg, z_hg, z_hg, valid_seq, lb,
      r, v, nkk, lw_f, be_f, kd_f, r, v, nkk, lw_b, be_b, kd_b)


def _merge_kernel(x_ref, g_ref, valid_ref, hof_ref, hob_ref, hg_ref, ysc_ref, yda_ref,
                  rof_ref, rob_ref, bonus_ref, rgate_ref, onorm_ref, lnxg_ref, lnxb_ref, bd_ref,
                  wg_ref, bp_ref, wo_ref, o_ref):
    x = x_ref[...]
    d = x.shape[1]
    h = _masked_rms_norm(x, g_ref[...], valid_ref[...]).astype(BF16)

    o = hof_ref[...] + hob_ref[...]
    heads = []
    for c in range(0, o.shape[1], HG_HEAD_DIM):
        oh = o[:, c:c + HG_HEAD_DIM]
        heads.append(oh * lax.rsqrt(jnp.mean(oh * oh, axis=-1, keepdims=True) + NORM_EPS))
    hg = hg_ref[...]
    y_hg = jnp.concatenate(heads, axis=1) * onorm_ref[...] * (hg * _sigmoid(hg))

    o = rof_ref[...] + rob_ref[...]
    bd = bd_ref[...]
    inv = 1.0 / RW_HEAD_DIM
    cen = o - _group_sum(o, bd) * inv
    var = _group_sum(cen * cen, bd) * inv
    y_rw = ((cen * lax.rsqrt(var + RW_LNX_EPS) * lnxg_ref[...] + lnxb_ref[...] + bonus_ref[...])
            * rgate_ref[...])

    merged = jnp.zeros_like(x)
    for n, y in enumerate((y_hg, ysc_ref[...], yda_ref[...], y_rw)):
        gate = _sigmoid(_dot(h, wg_ref[:, n * d:(n + 1) * d]))
        merged = merged + gate * _dot(y.astype(BF16), bp_ref[n])
    o_ref[...] = x + _dot(merged.astype(BF16), wo_ref[...])


def merge(x2d, g, valid, hg_of, hg_ob, z_hg2d, hg_col0, y_sc, y_da, rw_of, rw_ob, bonus, rw_gate,
          onorm_g, lnx_g, lnx_b, bd_ones, w_gate, branch_proj, w_out, tm):
    m, d = x2d.shape
    bw = y_sc.shape[1]
    row = lambda width: pl.BlockSpec((tm, width), lambda i: (i, 0))
    full = lambda a: pl.BlockSpec(a.shape, lambda i: (0,) * a.ndim)
    gate_blk = hg_col0 // bw + 4
    hg_gate = pl.BlockSpec((tm, bw), lambda i: (i, gate_blk))
    consts = (onorm_g, lnx_g, lnx_b, bd_ones, w_gate, branch_proj, w_out)
    return pl.pallas_call(
        _merge_kernel,
        grid=(m // tm,),
        in_specs=[row(d), full(g), row(1), row(bw), row(bw), hg_gate] + [row(bw)] * 6
                 + [full(a) for a in consts],
        out_specs=row(d),
        out_shape=jax.ShapeDtypeStruct((m, d), F32),
        compiler_params=_params(("parallel",)),
        name="merge",
    )(x2d, g, valid, hg_of, hg_ob, z_hg2d, y_sc, y_da, rw_of, rw_ob, bonus, rw_gate, *consts)


def _mlp_kernel(x_ref, g_ref, w1_ref, w2_ref, o_ref, h_scr, acc_scr):
    f = pl.program_id(1)

    @pl.when(f == 0)
    def _():
        x = x_ref[...]
        ms = jnp.mean(x * x, axis=-1, keepdims=True)
        h_scr[...] = (x * lax.rsqrt(ms + NORM_EPS) * g_ref[...]).astype(BF16)
        acc_scr[...] = jnp.zeros_like(acc_scr)

    a = jnp.maximum(_dot(h_scr[...], w1_ref[...]), 0.0)
    acc_scr[...] += _dot((a * a).astype(BF16), w2_ref[...])

    @pl.when(f == pl.num_programs(1) - 1)
    def _():
        o_ref[...] = x_ref[...] + acc_scr[...]


def mlp(x2d, g, w1, w2, tm, tf):
    m, d = x2d.shape
    dff = w1.shape[1]
    return pl.pallas_call(
        _mlp_kernel,
        grid=(m // tm, dff // tf),
        in_specs=[pl.BlockSpec((tm, d), lambda i, f: (i, 0)),
                  pl.BlockSpec((1, d), lambda i, f: (0, 0)),
                  pl.BlockSpec((d, tf), lambda i, f: (0, f)),
                  pl.BlockSpec((tf, d), lambda i, f: (f, 0))],
        out_specs=pl.BlockSpec((tm, d), lambda i, f: (i, 0)),
        out_shape=jax.ShapeDtypeStruct((m, d), F32),
        scratch_shapes=[pltpu.VMEM((tm, d), BF16), pltpu.VMEM((tm, d), F32)],
        compiler_params=_params(("parallel", "arbitrary")),
        name="mlp",
    )(x2d, g, w1, w2)


def _rope_tables(p):
    half = ROPE_DIM // 2
    pos = jnp.arange(p, dtype=F32) - FRONT_PAD
    inv = ROPE_THETA ** (-jnp.arange(half, dtype=F32) / half)
    ang = pos[:, None] * inv[None, :]
    cos, sin = jnp.cos(ang), jnp.sin(ang)
    ones = jnp.ones((p, DA_QK_DIM - ROPE_DIM), F32)
    zeros = jnp.zeros((p, DA_QK_DIM - ROPE_DIM), F32)
    zh = jnp.zeros((p, half), F32)
    c = jnp.concatenate([cos, cos, ones], axis=1)
    s1 = jnp.concatenate([-sin, zh, zeros], axis=1)
    s2 = jnp.concatenate([zh, sin, zeros], axis=1)
    return tuple(jnp.tile(t, (1, 2)) for t in (c, s1, s2))


def _block_diag_ones(n, blk):
    i = jnp.arange(n)
    return ((i[:, None] // blk) == (i[None, :] // blk)).astype(F32)


def kernel(x_prompt, x_sample, meta_tokens, norm_mix_g, w_in, hgrn_lb_logits, hgrn_onorm_g, conv_w,
           diff_qnorm_g, diff_knorm_g, diff_lambda, diff_subln_g, rwkv_mu, rwkv_w0, rwkv_w2, rwkv_a0,
           rwkv_a2, rwkv_g2, rwkv_k_k, rwkv_k_a, rwkv_r_k, rwkv_lnx_g, rwkv_lnx_b, w_gate,
           branch_proj, w_out, norm_mlp_g, mlp_w1, mlp_w2):
    assert x_prompt.shape[1:] == x_sample.shape[1:]
    x = jnp.concatenate([x_prompt, x_sample], axis=0)
    bt, seq, d = x.shape
    depth = w_in.shape[0]
    length = N_META + seq
    p = -(-(FRONT_PAD + length) // V7X_LANES) * V7X_LANES
    m = bt * p
    bw = branch_proj.shape[2]
    rw_cols = rwkv_mu.shape[1]
    sizes = (5 * bw, 3 * bw, 3 * bw, rw_cols)
    offs = [0]
    for s in sizes:
        offs.append(offs[-1] + s)
    rw_pad = -(-rw_cols // bw) * bw
    c_hg, c_sc, c_da = rw_pad, rw_pad + sizes[0], rw_pad + sizes[0] + sizes[1]
    zw = c_da + sizes[2]

    meta = jnp.broadcast_to(meta_tokens.astype(x.dtype)[None], (bt, N_META, d))
    xp = jnp.concatenate([jnp.zeros((bt, FRONT_PAD, d), x.dtype), meta, x,
                          jnp.zeros((bt, p - FRONT_PAD - length, d), x.dtype)], axis=1)
    x2d = xp.reshape(m, d)

    rows = jnp.arange(p)
    valid_seq = ((rows >= FRONT_PAD) & (rows < FRONT_PAD + length)).astype(F32)[:, None]
    valid = jnp.tile(valid_seq, (bt, 1))
    key_bias = jnp.where(valid_seq > 0, 0.0, NEG_BIG).astype(F32)
    tables = _rope_tables(p)
    bd64_mean = (_block_diag_ones(V7X_LANES, DA_QK_DIM) / DA_QK_DIM).astype(BF16)
    bd64_ones = _block_diag_ones(V7X_LANES, RW_HEAD_DIM).astype(BF16)

    sm = jax.nn.softmax(hgrn_lb_logits.astype(F32), axis=1)
    lb_all = jnp.cumsum(sm, axis=1) - sm[:, :1]

    tm_proj = _pick(m, 1536, 128)
    tm_merge = _pick(m, 256, 128)
    tm_mlp = _pick(m, 768, 128)
    tq = _pick(p, 384, 128)
    tk = _pick(p, 384, 128)
    assert FRONT_PAD <= tk and p - (FRONT_PAD + length) <= tk
    tr = _pick(p, 384, 8)

    half = V7X_LANES // 2
    for l in range(depth):
        g_mix = norm_mix_g[l][None, :]
        w_l = jnp.concatenate([w_in[l][:, offs[3]:], jnp.zeros((d, rw_pad - rw_cols), w_in.dtype),
                               w_in[l][:, :offs[3]]], axis=1).astype(BF16)
        z = norm_matmul(x2d, g_mix, valid, w_l, tm_proj, _pick(zw, 768, V7X_LANES)).reshape(bt, p, zw)

        y_sc = shortconv(z, c_sc, conv_w[l])

        lam_init = 0.8 - 0.6 * math.exp(-0.3 * l)
        zpad = jnp.zeros((2, half, bw), F32)
        w2pad = jnp.stack([jnp.concatenate([rwkv_w2[l, 0], zpad[0]], 0),
                           jnp.concatenate([zpad[0], rwkv_w2[l, 1]], 0)])
        a2pad = jnp.stack([jnp.concatenate([rwkv_a2[l, 0], zpad[0]], 0),
                           jnp.concatenate([zpad[0], rwkv_a2[l, 1]], 0)])
        y_da = diff_attention(z, c_da, tables, bd64_mean,
                              jnp.tile(diff_qnorm_g[l], 2)[None, :], jnp.tile(diff_knorm_g[l], 2)[None, :],
                              diff_lambda[l], diff_subln_g[l][:, None], key_bias, lam_init, tq, tk)
        (r, v, nkk, lwf, bef, kdf, lwb, beb, kdb, gate, bonus) = rwkv_prep(
            z, valid_seq, rwkv_mu[l][None, :], rwkv_w0[l], w2pad, rwkv_a0[l], a2pad, rwkv_g2[l],
            rwkv_k_k[l][None, :], rwkv_k_a[l][None, :], rwkv_r_k[l].reshape(1, bw), bd64_ones, tr)
        of, ob, orf, orb = scans(z, c_hg, valid_seq, lb_all[:, l], r, v, nkk, lwf, bef, kdf, lwb, beb,
                                 kdb, _pick(p // RW_CHUNK, SCAN_CHUNKS_PER_STEP, 1))

        flat = lambda a: a.reshape(m, a.shape[-1])
        x2d = merge(x2d, g_mix, valid, flat(of), flat(ob), flat(z), c_hg, flat(y_sc), flat(y_da),
                    flat(orf), flat(orb), flat(bonus), flat(gate), hgrn_onorm_g[l][None, :],
                    rwkv_lnx_g[l][None, :], rwkv_lnx_b[l][None, :], bd64_ones,
                    w_gate[l].astype(BF16), branch_proj[l].astype(BF16), w_out[l].astype(BF16), tm_merge)
        x2d = mlp(x2d, norm_mlp_g[l][None, :], mlp_w1[l].astype(BF16), mlp_w2[l].astype(BF16),
                  tm_mlp, _pick(mlp_w1.shape[2], 1024, V7X_LANES))

    y = x2d.reshape(bt, p, d)[:, FRONT_PAD + N_META:FRONT_PAD + length]
    nb = x_prompt.shape[0]
    return (y[:nb], y[nb:])
```
